```python
import math
import jax, jax.numpy as jnp
from jax import lax
import numpy as np

D_MODEL = 1024
BATCH = 8
SEQ = 4096
DEPTH = 2

N_META = 16
D_CONV = D_MODEL // 2
CONV_WIDTH = 3
D_ATTN = D_MODEL - D_CONV
ATTN_HEAD_DIM = 64
N_ATTN_HEADS = D_ATTN // (2 * ATTN_HEAD_DIM)
QK_W = N_ATTN_HEADS * 2 * ATTN_HEAD_DIM
V_W = N_ATTN_HEADS * 2 * ATTN_HEAD_DIM
D_IN = 3 * D_CONV + 2 * QK_W + V_W
ROPE_THETA = 10000.0
Q_BLOCK = 128
D_FF_DENSE = 2816
N_EXPERTS = 8
TOP_K = 2
D_FF_EXPERT = 3584
N_DENSE_LAYERS = (DEPTH + 1) // 2
N_MOE_LAYERS = DEPTH // 2
LN_EPS = 1e-5
RMS_EPS = 1e-5
DEEPNORM_ALPHA = (2 * DEPTH) ** 0.25
DEEPNORM_BETA = (8 * DEPTH) ** -0.25

kernel_name = 'hybrid_conv_diffattn_deepnorm_moe'


def layer_norm(x, g, b):
    xf = x.astype(jnp.float32)
    mu = jnp.mean(xf, axis=-1, keepdims=True)
    xc = xf - mu
    var = jnp.mean(xc * xc, axis=-1, keepdims=True)
    y = xc * lax.rsqrt(var + LN_EPS) * g.astype(jnp.float32) + b.astype(jnp.float32)
    return y.astype(x.dtype)


def rms_norm(x, g):
    xf = x.astype(jnp.float32)
    y = xf * lax.rsqrt(jnp.mean(xf * xf, axis=-1, keepdims=True) + RMS_EPS) * g.astype(jnp.float32)
    return y.astype(x.dtype)


def rope_tables(length):
    inv = 1.0 / (ROPE_THETA ** (jnp.arange(0, ATTN_HEAD_DIM, 2, dtype=jnp.float32) / ATTN_HEAD_DIM))
    ang = jnp.arange(length, dtype=jnp.float32)[:, None] * inv[None, :]
    ang = jnp.concatenate([ang, ang], axis=-1)
    return jnp.cos(ang), jnp.sin(ang)


def apply_rope(x, cos, sin):
    c = cos[None, :, None, None, :].astype(x.dtype)
    s = sin[None, :, None, None, :].astype(x.dtype)
    x1, x2 = jnp.split(x, 2, axis=-1)
    return x * c + jnp.concatenate([-x2, x1], axis=-1) * s


def short_gated_conv(u_b, u_c, u_h, conv_w):
    v = u_c * u_h
    y = lax.conv_general_dilated(
        v, conv_w[:, None, :].astype(v.dtype), window_strides=(1,),
        padding=[(CONV_WIDTH - 1, 0)], dimension_numbers=('NWC', 'WIO', 'NWC'),
        feature_group_count=D_CONV)
    return u_b * y


def diff_attention(q, k, v, lam, lambda_init, subln_g, cos, sin):
    bsz, length = q.shape[0], q.shape[1]
    n_blk = -(-length // Q_BLOCK)
    lp = n_blk * Q_BLOCK
    pad = lp - length
    q = jnp.pad(apply_rope(q, cos, sin), ((0, 0), (0, pad), (0, 0), (0, 0), (0, 0)))
    k = jnp.pad(apply_rope(k, cos, sin), ((0, 0), (0, pad), (0, 0), (0, 0), (0, 0)))
    v = jnp.pad(v, ((0, 0), (0, pad), (0, 0), (0, 0)))
    q_blocks = q.reshape(bsz, n_blk, Q_BLOCK, N_ATTN_HEADS, 2, ATTN_HEAD_DIM).transpose(1, 0, 2, 3, 4, 5)
    kpos = jnp.arange(lp)
    scale = ATTN_HEAD_DIM ** -0.5

    def attend(args):
        qb, blk = args
        s = jnp.einsum('bqhcd,bkhcd->bhcqk', qb, k).astype(jnp.float32) * scale
        qpos = blk * Q_BLOCK + jnp.arange(Q_BLOCK)
        causal = kpos[None, :] <= qpos[:, None]
        p = jax.nn.softmax(jnp.where(causal, s, -jnp.inf), axis=-1)
        a = p[:, :, 0] - lam * p[:, :, 1]
        return jnp.einsum('bhqk,bkhe->bqhe', a.astype(v.dtype), v)

    o = lax.map(attend, (q_blocks, jnp.arange(n_blk)))
    o = o.transpose(1, 0, 2, 3, 4).reshape(bsz, lp, N_ATTN_HEADS, 2 * ATTN_HEAD_DIM)[:, :length]
    o = rms_norm(o, subln_g) * (1.0 - lambda_init)
    return o.reshape(bsz, length, D_ATTN)


def hybrid_mixer(h, w_in, conv_w, lq1, lk1, lq2, lk2, subln_g, w_out, lambda_init, cos, sin):
    bsz, length, _ = h.shape
    z = jnp.einsum('bld,de->ble', h, w_in)
    cuts = [D_CONV, 2 * D_CONV, 3 * D_CONV, 3 * D_CONV + QK_W, 3 * D_CONV + 2 * QK_W]
    u_b, u_c, u_h, q, k, v = jnp.split(z, cuts, axis=-1)
    conv_out = short_gated_conv(u_b, u_c, u_h, conv_w)
    q = q.reshape(bsz, length, N_ATTN_HEADS, 2, ATTN_HEAD_DIM)
    k = k.reshape(bsz, length, N_ATTN_HEADS, 2, ATTN_HEAD_DIM)
    v = v.reshape(bsz, length, N_ATTN_HEADS, 2 * ATTN_HEAD_DIM)
    lam = (jnp.exp(jnp.sum((lq1 * lk1).astype(jnp.float32)))
           - jnp.exp(jnp.sum((lq2 * lk2).astype(jnp.float32))) + lambda_init)
    attn_out = diff_attention(q, k, v, lam, lambda_init, subln_g, cos, sin)
    return jnp.einsum('ble,ed->bld', jnp.concatenate([conv_out, attn_out], axis=-1), w_out)


def swiglu(h, w_gate, w_up, w_down):
    return jnp.matmul(jax.nn.silu(jnp.matmul(h, w_gate)) * jnp.matmul(h, w_up), w_down)


def moe_swiglu(h, w_router, w_gate, w_up, w_down):
    logits = jnp.einsum('bld,de->ble', h, w_router).astype(jnp.float32)
    top_val, top_idx = lax.top_k(logits, TOP_K)
    top_w = jax.nn.softmax(top_val, axis=-1)
    gate = jnp.sum(jax.nn.one_hot(top_idx, N_EXPERTS, dtype=jnp.float32) * top_w[..., None], axis=-2)
    gate = gate.astype(h.dtype)
    out = jnp.zeros_like(h)
    for e in range(N_EXPERTS):
        out = out + gate[..., e:e + 1] * swiglu(h, w_gate[e], w_up[e], w_down[e])
    return out


def setup_inputs(seed: int = 0) -> dict:
    key = jax.random.key(seed)
    ks = jax.random.split(key, 26)
    f32 = jnp.float32

    def nrm(k, shape, s):
        return jax.random.normal(k, shape, f32) * s

    def gain(k, shape):
        return 1.0 + 0.01 * jax.random.normal(k, shape, f32)

    return {
        'x': nrm(ks[0], (BATCH, SEQ, D_MODEL), 1.0),
        'meta_tokens': nrm(ks[1], (N_META, D_MODEL), 1.0),
        'ln_emb_g': gain(ks[2], (D_MODEL,)),
        'ln_emb_b': nrm(ks[3], (D_MODEL,), 0.01),
        'w_in': nrm(ks[4], (DEPTH, D_MODEL, D_IN), D_MODEL ** -0.5),
        'conv_w': nrm(ks[5], (DEPTH, CONV_WIDTH, D_CONV), CONV_WIDTH ** -0.5),
        'lambda_q1': nrm(ks[6], (DEPTH, ATTN_HEAD_DIM), 0.1),
        'lambda_k1': nrm(ks[7], (DEPTH, ATTN_HEAD_DIM), 0.1),
        'lambda_q2': nrm(ks[8], (DEPTH, ATTN_HEAD_DIM), 0.1),
        'lambda_k2': nrm(ks[9], (DEPTH, ATTN_HEAD_DIM), 0.1),
        'subln_g': gain(ks[10], (DEPTH, 2 * ATTN_HEAD_DIM)),
        'w_out': nrm(ks[11], (DEPTH, D_MODEL, D_MODEL), D_MODEL ** -0.5 * DEEPNORM_BETA),
        'ln_mix_g': gain(ks[12], (DEPTH, D_MODEL)),
        'ln_mix_b': nrm(ks[13], (DEPTH, D_MODEL), 0.01),
        'ln_ffn_g': gain(ks[14], (DEPTH, D_MODEL)),
        'ln_ffn_b': nrm(ks[15], (DEPTH, D_MODEL), 0.01),
        'w_gate_dense': nrm(ks[16], (N_DENSE_LAYERS, D_MODEL, D_FF_DENSE), D_MODEL ** -0.5),
        'w_up_dense': nrm(ks[17], (N_DENSE_LAYERS, D_MODEL, D_FF_DENSE), D_MODEL ** -0.5),
        'w_down_dense': nrm(ks[18], (N_DENSE_LAYERS, D_FF_DENSE, D_MODEL), D_FF_DENSE ** -0.5 * DEEPNORM_BETA),
        'w_router': nrm(ks[19], (N_MOE_LAYERS, D_MODEL, N_EXPERTS), D_MODEL ** -0.5),
        'w_gate_moe': nrm(ks[20], (N_MOE_LAYERS, N_EXPERTS, D_MODEL, D_FF_EXPERT), D_MODEL ** -0.5),
        'w_up_moe': nrm(ks[21], (N_MOE_LAYERS, N_EXPERTS, D_MODEL, D_FF_EXPERT), D_MODEL ** -0.5),
        'w_down_moe': nrm(ks[22], (N_MOE_LAYERS, N_EXPERTS, D_FF_EXPERT, D_MODEL), D_FF_EXPERT ** -0.5 * DEEPNORM_BETA),
    }


def reference(x, meta_tokens, ln_emb_g, ln_emb_b, w_in, conv_w, lambda_q1, lambda_k1,
              lambda_q2, lambda_k2, subln_g, w_out, ln_mix_g, ln_mix_b, ln_ffn_g, ln_ffn_b,
              w_gate_dense, w_up_dense, w_down_dense, w_router, w_gate_moe, w_up_moe, w_down_moe):
    bsz = x.shape[0]
    meta = jnp.broadcast_to(meta_tokens[None].astype(x.dtype), (bsz, N_META, D_MODEL))
    h = jnp.concatenate([meta, x], axis=1)
    cos, sin = rope_tables(h.shape[1])
    h = layer_norm(h, ln_emb_g, ln_emb_b)
    for layer in range(DEPTH):
        lambda_init = 0.8 - 0.6 * math.exp(-0.3 * layer)
        mix = hybrid_mixer(h, w_in[layer], conv_w[layer], lambda_q1[layer], lambda_k1[layer],
                           lambda_q2[layer], lambda_k2[layer], subln_g[layer], w_out[layer],
                           lambda_init, cos, sin)
        h = layer_norm(DEEPNORM_ALPHA * h + mix, ln_mix_g[layer], ln_mix_b[layer])
        idx = layer // 2
        if layer % 2 == 0:
            ffn = swiglu(h, w_gate_dense[idx], w_up_dense[idx], w_down_dense[idx])
        else:
            ffn = moe_swiglu(h, w_router[idx], w_gate_moe[idx], w_up_moe[idx], w_down_moe[idx])
        h = layer_norm(DEEPNORM_ALPHA * h + ffn, ln_ffn_g[layer], ln_ffn_b[layer])
    return h[:, N_META:]
```

```python
import dataclasses
import functools
import math

import jax
import jax.numpy as jnp
from jax import lax
from jax.experimental import pallas as pl
from jax.experimental.pallas import tpu as pltpu

F32 = jnp.float32
BF16 = jnp.bfloat16

LANE = 128
D_MODEL = 1024
N_META = 16
PAD = LANE - N_META
D_CONV = D_MODEL // 2
CONV_WIDTH = 3
HEAD_DIM = 64
N_HEADS = 4
V_DIM = 2 * HEAD_DIM
QK_W = N_HEADS * 2 * HEAD_DIM
GROUP_W = 512
ROPE_THETA = 10000.0
TOP_K = 2
LN_EPS = 1e-5
RMS_EPS = 1e-5
NEG_BIG = -1e30
VMEM_LIMIT = 56 * 1024 * 1024


@dataclasses.dataclass(frozen=True)
class Cfg:
    batch: int
    seq: int
    depth: int
    d_ff_dense: int
    d_ff_expert: int
    n_experts: int
    t_embed: int
    t_proj: int
    t_q: int
    t_k: int
    t_tok: int
    t_ff_dense: int
    t_ff_expert: int

    @property
    def lp(self):
        return PAD + N_META + self.seq

    @property
    def tokens(self):
        return self.batch * self.lp


def _params(semantics):
    return pltpu.CompilerParams(dimension_semantics=semantics, vmem_limit_bytes=VMEM_LIMIT)


def _layer_norm(y, g, b):
    mu = jnp.mean(y, axis=-1, keepdims=True)
    yc = y - mu
    var = jnp.mean(yc * yc, axis=-1, keepdims=True)
    return yc * lax.rsqrt(var + LN_EPS) * g + b


def _embed_kernel(x0_ref, x1_ref, x2_ref, meta_ref, g_ref, b_ref, h_ref, hb_ref):
    j = pl.program_id(1)
    first = jnp.where(j == 0, meta_ref[...], x0_ref[0])
    rows = jnp.concatenate([first, x1_ref[0], x2_ref[0]], axis=0)
    y = _layer_norm(rows, g_ref[...], b_ref[...])
    h_ref[0] = y
    hb_ref[0] = y.astype(BF16)


def _embed(x, meta_block, g, b, cfg):
    te = cfg.t_embed
    steps = cfg.lp // (3 * te)

    def x_spec(i):
        return pl.BlockSpec((1, te, D_MODEL), lambda bb, j: (bb, jnp.maximum(3 * j + i - 1, 0), 0))

    const = lambda bb, j: (0, 0)
    out_spec = pl.BlockSpec((1, 3 * te, D_MODEL), lambda bb, j: (bb, j, 0))
    return pl.pallas_call(
        _embed_kernel,
        grid=(cfg.batch, steps),
        in_specs=[x_spec(0), x_spec(1), x_spec(2),
                  pl.BlockSpec((te, D_MODEL), const),
                  pl.BlockSpec((1, D_MODEL), const),
                  pl.BlockSpec((1, D_MODEL), const)],
        out_specs=[out_spec, out_spec],
        out_shape=[jax.ShapeDtypeStruct((cfg.batch, cfg.lp, D_MODEL), F32),
                   jax.ShapeDtypeStruct((cfg.batch, cfg.lp, D_MODEL), BF16)],
        compiler_params=_params(("parallel", "arbitrary")),
        name="embed_ln",
    )(x, x, x, meta_block, g, b)


def _inproj_kernel(hb_ref, w_ref, cw_ref, cos_ref, sin_ref,
                   conv_ref, qa_ref, qb_ref, k_ref, v_ref, carry_ref, *, t_proj):
    j = pl.program_id(1)
    x = hb_ref[0]

    def proj(g):
        return jnp.dot(x, w_ref[:, g * GROUP_W:(g + 1) * GROUP_W], preferred_element_type=F32)

    rid = lax.broadcasted_iota(jnp.int32, (t_proj, 1), 0)
    gated = jnp.where(j * t_proj + rid >= PAD, proj(1) * proj(2), 0.0)

    @pl.when(j == 0)
    def _():
        carry_ref[...] = jnp.zeros_like(carry_ref)

    prev = carry_ref[...]
    back1 = jnp.where(rid == 0, prev[7:8], pltpu.roll(gated, 1, 0))
    back2 = jnp.where(rid == 0, prev[6:7], jnp.where(rid == 1, prev[7:8], pltpu.roll(gated, 2, 0)))
    carry_ref[...] = gated[t_proj - 8:]
    cw = cw_ref[...]
    conv = cw[0:1] * back2 + cw[1:2] * back1 + cw[2:3] * gated
    conv_ref[0] = (proj(0) * conv).astype(BF16)

    cos = cos_ref[...]
    sin = sin_ref[...]
    lane = lax.broadcasted_iota(jnp.int32, (1, LANE), 1)
    low_half = (lane % HEAD_DIM) < HEAD_DIM // 2
    first_map = lane < HEAD_DIM

    def rope(z):
        partner = jnp.where(low_half, pltpu.roll(z, LANE - HEAD_DIM // 2, 1),
                            pltpu.roll(z, HEAD_DIM // 2, 1))
        return z * cos + partner * sin

    zq = proj(3)
    zk = proj(4)
    scale = HEAD_DIM ** -0.5
    for h in range(N_HEADS):
        sl = slice(h * LANE, (h + 1) * LANE)
        q = rope(zq[:, sl]) * scale
        qa_ref[0, :, sl] = jnp.where(first_map, q, 0.0).astype(BF16)
        qb_ref[0, :, sl] = jnp.where(first_map, 0.0, q).astype(BF16)
        k_ref[0, :, sl] = rope(zk[:, sl]).astype(BF16)
    v_ref[0] = proj(5).astype(BF16)


def _inproj(hb, w_in, conv_w, cos, sin, cfg):
    tp = cfg.t_proj
    row = lambda bb, j: (bb, j, 0)
    const = lambda bb, j: (0, 0)
    out_spec = pl.BlockSpec((1, tp, GROUP_W), row)
    out_shape = jax.ShapeDtypeStruct((cfg.batch, cfg.lp, GROUP_W), BF16)
    return pl.pallas_call(
        functools.partial(_inproj_kernel, t_proj=tp),
        grid=(cfg.batch, cfg.lp // tp),
        in_specs=[pl.BlockSpec((1, tp, D_MODEL), row),
                  pl.BlockSpec((D_MODEL, 6 * GROUP_W), const),
                  pl.BlockSpec((CONV_WIDTH, D_CONV), const),
                  pl.BlockSpec((tp, LANE), lambda bb, j: (j, 0)),
                  pl.BlockSpec((tp, LANE), lambda bb, j: (j, 0))],
        out_specs=[out_spec] * 5,
        out_shape=[out_shape] * 5,
        scratch_shapes=[pltpu.VMEM((8, D_CONV), F32)],
        compiler_params=_params(("parallel", "arbitrary")),
        name="inproj_conv_rope",
    )(hb, w_in, conv_w, cos, sin)


def _attn_kernel(lq1_ref, lk1_ref, lq2_ref, lk2_ref, g_ref, qa_ref, qb_ref, k_ref, v_ref,
                 o_ref, m_ref, l_ref, acc_ref, *, t_q, t_k, lambda_init):
    j = pl.program_id(2)
    q2 = jnp.concatenate([qa_ref[0], qb_ref[0]], axis=0)
    m_ref[...] = jnp.full_like(m_ref, NEG_BIG)
    l_ref[...] = jnp.zeros_like(l_ref)
    acc_ref[...] = jnp.zeros_like(acc_ref)

    def step(c, masked):
        start = pl.multiple_of(c * t_k, t_k)
        k = k_ref[0, pl.ds(start, t_k), :]
        v = v_ref[0, pl.ds(start, t_k), :]
        s = lax.dot_general(q2, k, (((1,), (1,)), ((), ())), preferred_element_type=F32)
        if masked:
            qpos = j * t_q + lax.broadcasted_iota(jnp.int32, (2 * t_q, 1), 0) % t_q
            kpos = c * t_k + lax.broadcasted_iota(jnp.int32, (1, t_k), 1)
            s = jnp.where(kpos <= qpos, jnp.where(kpos >= PAD, s, NEG_BIG), NEG_BIG)
        m_prev = m_ref[...]
        m_new = jnp.maximum(m_prev, jnp.max(s, axis=-1, keepdims=True))
        alpha = jnp.exp(m_prev - m_new)
        p = jnp.exp(s - jnp.concatenate([m_new] * (t_k // LANE), axis=1))
        l_ref[...] = alpha * l_ref[...] + jnp.sum(p, axis=-1, keepdims=True)
        acc_ref[...] = alpha * acc_ref[...] + jnp.dot(p.astype(BF16), v, preferred_element_type=F32)
        m_ref[...] = m_new

    step(0, True)

    def body(c, carry):
        step(c, False)
        return carry

    lax.fori_loop(1, j, body, 0)

    @pl.when(j > 0)
    def _():
        step(j, True)

    lam = (jnp.exp(jnp.sum(lq1_ref[...] * lk1_ref[...], axis=-1, keepdims=True))
           - jnp.exp(jnp.sum(lq2_ref[...] * lk2_ref[...], axis=-1, keepdims=True)) + lambda_init)
    o = acc_ref[...] / l_ref[...]
    d = o[:t_q] - lam * o[t_q:]
    d = d * lax.rsqrt(jnp.mean(d * d, axis=-1, keepdims=True) + RMS_EPS) * g_ref[...]
    o_ref[0] = (d * (1.0 - lambda_init)).astype(BF16)


def _attention(qa, qb, k, v, lq1, lk1, lq2, lk2, subln_g, lambda_init, cfg):
    tq = cfg.t_q
    small = lambda bb, h, j: (0, 0)
    q_spec = pl.BlockSpec((1, tq, LANE), lambda bb, h, j: (bb, j, h))
    kv_spec = pl.BlockSpec((1, cfg.lp, LANE), lambda bb, h, j: (bb, 0, h))
    return pl.pallas_call(
        functools.partial(_attn_kernel, t_q=tq, t_k=cfg.t_k, lambda_init=lambda_init),
        grid=(cfg.batch, N_HEADS, cfg.lp // tq),
        in_specs=[pl.BlockSpec((1, HEAD_DIM), small)] * 4
        + [pl.BlockSpec((1, V_DIM), small), q_spec, q_spec, kv_spec, kv_spec],
        out_specs=q_spec,
        out_shape=jax.ShapeDtypeStruct((cfg.batch, cfg.lp, N_HEADS * V_DIM), BF16),
        scratch_shapes=[pltpu.VMEM((2 * tq, LANE), F32),
                        pltpu.VMEM((2 * tq, LANE), F32),
                        pltpu.VMEM((2 * tq, V_DIM), F32)],
        compiler_params=_params(("parallel", "parallel", "arbitrary")),
        name="diff_attention",
    )(lq1, lk1, lq2, lk2, subln_g, qa, qb, k, v)


def _outproj_kernel(conv_ref, attn_ref, w_ref, h_ref, g_ref, b_ref, ho_ref, hbo_ref, *, alpha):
    mix = (jnp.dot(conv_ref[...], w_ref[:D_CONV], preferred_element_type=F32)
           + jnp.dot(attn_ref[...], w_ref[D_CONV:], preferred_element_type=F32))
    y = _layer_norm(alpha * h_ref[...] + mix, g_ref[...], b_ref[...])
    ho_ref[...] = y
    hbo_ref[...] = y.astype(BF16)


def _outproj(conv, attn, w_out, h, g, b, cfg, alpha):
    tt = cfg.t_tok
    row = lambda i: (i, 0)
    const = lambda i: (0, 0)
    return pl.pallas_call(
        functools.partial(_outproj_kernel, alpha=alpha),
        grid=(cfg.tokens // tt,),
        in_specs=[pl.BlockSpec((tt, D_CONV), row),
                  pl.BlockSpec((tt, D_MODEL - D_CONV), row),
                  pl.BlockSpec((D_MODEL, D_MODEL), const),
                  pl.BlockSpec((tt, D_MODEL), row),
                  pl.BlockSpec((1, D_MODEL), const),
                  pl.BlockSpec((1, D_MODEL), const)],
        out_specs=[pl.BlockSpec((tt, D_MODEL), row)] * 2,
        out_shape=[jax.ShapeDtypeStruct((cfg.tokens, D_MODEL), F32),
                   jax.ShapeDtypeStruct((cfg.tokens, D_MODEL), BF16)],
        compiler_params=_params(("parallel",)),
        name="outproj_ln",
    )(conv, attn, w_out, h, g, b)


def _swiglu_contrib(x, wg_ref, wu_ref, wd_ref):
    gate = jnp.dot(x, wg_ref[...], preferred_element_type=F32)
    up = jnp.dot(x, wu_ref[...], preferred_element_type=F32)
    act = gate * (1.0 / (1.0 + jnp.exp(-gate))) * up
    return jnp.dot(act.astype(BF16), wd_ref[...], preferred_element_type=F32)


def _ffn_kernel(x_ref, wg_ref, wu_ref, wd_ref, h_ref, g_ref, b_ref, ho_ref, hbo_ref, *, alpha, t_ff):
    x = x_ref[...]
    acc = None
    for c in range(wg_ref.shape[1] // t_ff):
        cols = pl.ds(c * t_ff, t_ff)
        contrib = _swiglu_contrib(x, wg_ref.at[:, cols], wu_ref.at[:, cols], wd_ref.at[cols, :])
        acc = contrib if acc is None else acc + contrib
    y = _layer_norm(alpha * h_ref[...] + acc, g_ref[...], b_ref[...])
    ho_ref[...] = y
    hbo_ref[...] = y.astype(BF16)


def _ffn(hb, wg, wu, wd, h, g, b, cfg, alpha):
    tt = cfg.t_tok
    row = lambda i: (i, 0)
    const = lambda i: (0, 0)
    resident = lambda shape: pl.BlockSpec(shape, const, pipeline_mode=pl.Buffered(1))
    return pl.pallas_call(
        functools.partial(_ffn_kernel, alpha=alpha, t_ff=cfg.t_ff_dense),
        grid=(cfg.tokens // tt,),
        in_specs=[pl.BlockSpec((tt, D_MODEL), row),
                  resident((D_MODEL, cfg.d_ff_dense)),
                  resident((D_MODEL, cfg.d_ff_dense)),
                  resident((cfg.d_ff_dense, D_MODEL)),
                  pl.BlockSpec((tt, D_MODEL), row),
                  pl.BlockSpec((1, D_MODEL), const),
                  pl.BlockSpec((1, D_MODEL), const)],
        out_specs=[pl.BlockSpec((tt, D_MODEL), row)] * 2,
        out_shape=[jax.ShapeDtypeStruct((cfg.tokens, D_MODEL), F32),
                   jax.ShapeDtypeStruct((cfg.tokens, D_MODEL), BF16)],
        compiler_params=_params(("parallel",)),
        name="dense_ffn_ln",
    )(hb, wg, wu, wd, h, g, b)


def _router_kernel(h_ref, wr_ref, gate_ref, *, n_experts):
    logits = jnp.dot(h_ref[...], wr_ref[...], preferred_element_type=F32,
                     precision=lax.Precision.HIGHEST)
    lane = lax.broadcasted_iota(jnp.int32, logits.shape, 1).astype(F32)
    logits = jnp.where(lane < n_experts, logits, -jnp.inf)
    v1 = jnp.max(logits, axis=-1, keepdims=True)
    i1 = jnp.min(jnp.where(logits == v1, lane, float(LANE)), axis=-1, keepdims=True)
    rest = jnp.where(lane == i1, -jnp.inf, logits)
    v2 = jnp.max(rest, axis=-1, keepdims=True)
    i2 = jnp.min(jnp.where(rest == v2, lane, float(LANE)), axis=-1, keepdims=True)
    e2 = jnp.exp(v2 - v1)
    w1 = 1.0 / (1.0 + e2)
    w2 = e2 / (1.0 + e2)
    gate_ref[...] = jnp.where(lane == i1, w1, 0.0) + jnp.where(lane == i2, w2, 0.0)


def _router(h, wr_pad, cfg):
    tt = cfg.t_tok
    return pl.pallas_call(
        functools.partial(_router_kernel, n_experts=cfg.n_experts),
        grid=(cfg.tokens // tt,),
        in_specs=[pl.BlockSpec((tt, D_MODEL), lambda i: (i, 0)),
                  pl.BlockSpec((D_MODEL, LANE), lambda i: (0, 0))],
        out_specs=pl.BlockSpec((tt, LANE), lambda i: (i, 0)),
        out_shape=jax.ShapeDtypeStruct((cfg.tokens, LANE), F32),
        compiler_params=_params(("parallel",)),
        name="router_top2",
    )(h, wr_pad)


def _moe_kernel(x_ref, gate_ref, wg_ref, wu_ref, wd_ref, h_ref, g_ref, b_ref, ho_ref, acc_ref, *, alpha):
    e = pl.program_id(1)
    f = pl.program_id(2)
    lane = lax.broadcasted_iota(jnp.int32, gate_ref.shape, 1)
    w = jnp.sum(jnp.where(lane == e, gate_ref[...], 0.0), axis=-1, keepdims=True)
    contrib = w * _swiglu_contrib(x_ref[...], wg_ref.at[0], wu_ref.at[0], wd_ref.at[0])
    first = jnp.logical_and(e == 0, f == 0)

    @pl.when(first)
    def _():
        acc_ref[...] = contrib

    @pl.when(jnp.logical_not(first))
    def _():
        acc_ref[...] += contrib

    @pl.when(jnp.logical_and(e == pl.num_programs(1) - 1, f == pl.num_programs(2) - 1))
    def _():
        ho_ref[...] = _layer_norm(alpha * h_ref[...] + acc_ref[...], g_ref[...], b_ref[...])


def _moe(hb, gate, wg, wu, wd, h, g, b, cfg, alpha):
    tt, tf = cfg.t_tok, cfg.t_ff_expert
    row = lambda i, e, f: (i, 0)
    const = lambda i, e, f: (0, 0)
    return pl.pallas_call(
        functools.partial(_moe_kernel, alpha=alpha),
        grid=(cfg.tokens // tt, cfg.n_experts, cfg.d_ff_expert // tf),
        in_specs=[pl.BlockSpec((tt, D_MODEL), row),
                  pl.BlockSpec((tt, LANE), row),
                  pl.BlockSpec((1, D_MODEL, tf), lambda i, e, f: (e, 0, f)),
                  pl.BlockSpec((1, D_MODEL, tf), lambda i, e, f: (e, 0, f)),
                  pl.BlockSpec((1, tf, D_MODEL), lambda i, e, f: (e, f, 0)),
                  pl.BlockSpec((tt, D_MODEL), row),
                  pl.BlockSpec((1, D_MODEL), const),
                  pl.BlockSpec((1, D_MODEL), const)],
        out_specs=pl.BlockSpec((tt, D_MODEL), row),
        out_shape=jax.ShapeDtypeStruct((cfg.tokens, D_MODEL), F32),
        scratch_shapes=[pltpu.VMEM((tt, D_MODEL), F32)],
        compiler_params=_params(("parallel", "arbitrary", "arbitrary")),
        name="moe_ffn_ln",
    )(hb, gate, wg, wu, wd, h, g, b)


def _rope_tables(lp):
    half = HEAD_DIM // 2
    inv = 1.0 / (ROPE_THETA ** (jnp.arange(0, HEAD_DIM, 2, dtype=F32) / HEAD_DIM))
    pos = (jnp.arange(lp) - PAD).astype(F32)
    ang = pos[:, None] * inv[None, :]
    ang = jnp.tile(ang, (1, LANE // half))
    sign = jnp.where((jnp.arange(LANE) % HEAD_DIM) < half, -1.0, 1.0).astype(F32)
    return jnp.cos(ang), jnp.sin(ang) * sign[None, :]


def _forward(cfg, x, meta_tokens, ln_emb_g, ln_emb_b, w_in, conv_w, lambda_q1, lambda_k1,
             lambda_q2, lambda_k2, subln_g, w_out, ln_mix_g, ln_mix_b, ln_ffn_g, ln_ffn_b,
             w_gate_dense, w_up_dense, w_down_dense, w_router, w_gate_moe, w_up_moe, w_down_moe):
    alpha = (2 * cfg.depth) ** 0.25
    tokens = cfg.tokens
    vec = lambda a: a.reshape(1, -1)
    cos, sin = _rope_tables(cfg.lp)
    meta_block = jnp.pad(meta_tokens, ((cfg.t_embed - N_META, 0), (0, 0)))
    h, hb = _embed(x, meta_block, vec(ln_emb_g), vec(ln_emb_b), cfg)
    h = h.reshape(tokens, D_MODEL)
    for layer in range(cfg.depth):
        lambda_init = 0.8 - 0.6 * math.exp(-0.3 * layer)
        conv, qa, qb, k, v = _inproj(hb.reshape(cfg.batch, cfg.lp, D_MODEL), w_in[layer].astype(BF16),
                                     conv_w[layer], cos, sin, cfg)
        attn = _attention(qa, qb, k, v, vec(lambda_q1[layer]), vec(lambda_k1[layer]),
                          vec(lambda_q2[layer]), vec(lambda_k2[layer]), vec(subln_g[layer]),
                          lambda_init, cfg)
        h, hb = _outproj(conv.reshape(tokens, D_CONV), attn.reshape(tokens, D_MODEL - D_CONV),
                         w_out[layer].astype(BF16), h, vec(ln_mix_g[layer]), vec(ln_mix_b[layer]),
                         cfg, alpha)
        idx = layer // 2
        g, b = vec(ln_ffn_g[layer]), vec(ln_ffn_b[layer])
        if layer % 2 == 0:
            h, hb = _ffn(hb, w_gate_dense[idx].astype(BF16), w_up_dense[idx].astype(BF16),
                         w_down_dense[idx].astype(BF16), h, g, b, cfg, alpha)
        else:
            wr_pad = jnp.pad(w_router[idx], ((0, 0), (0, LANE - cfg.n_experts)))
            gate = _router(h, wr_pad, cfg)
            h = _moe(hb, gate, w_gate_moe[idx].astype(BF16), w_up_moe[idx].astype(BF16),
                     w_down_moe[idx].astype(BF16), h, g, b, cfg, alpha)
            hb = h.astype(BF16) if layer + 1 < cfg.depth else None
    return h.reshape(cfg.batch, cfg.lp, D_MODEL)[:, PAD + N_META:]


_CFG = Cfg(batch=8, seq=4096, depth=2, d_ff_dense=2816, d_ff_expert=3584, n_experts=8,
           t_embed=128, t_proj=1056, t_q=384, t_k=384, t_tok=1024, t_ff_dense=256, t_ff_expert=896)


def kernel(x, meta_tokens, ln_emb_g, ln_emb_b, w_in, conv_w, lambda_q1, lambda_k1, lambda_q2, lambda_k2, subln_g, w_out, ln_mix_g, ln_mix_b, ln_ffn_g, ln_ffn_b, w_gate_dense, w_up_dense, w_down_dense, w_router, w_gate_moe, w_up_moe, w_down_moe):
    return _forward(_CFG, x, meta_tokens, ln_emb_g, ln_emb_b, w_in, conv_w, lambda_q1, lambda_k1,
                    lambda_q2, lambda_k2, subln_g, w_out, ln_mix_g, ln_mix_b, ln_ffn_g, ln_ffn_b,
                    w_gate_dense, w_up_dense, w_down_dense, w_router, w_gate_moe, w_up_moe, w_down_moe)
```

```python
import dataclasses
import functools
import math

import jax
import jax.numpy as jnp
from jax import lax
from jax.experimental import pallas as pl
from jax.experimental.pallas import tpu as pltpu

F32 = jnp.float32
BF16 = jnp.bfloat16

LANE = 128
D_MODEL = 1024
N_META = 16
PAD = LANE - N_META
D_CONV = D_MODEL // 2
CONV_WIDTH = 3
HEAD_DIM = 64
N_HEADS = 4
V_DIM = 2 * HEAD_DIM
QK_W = N_HEADS * 2 * HEAD_DIM
GROUP_W = 512
ROPE_THETA = 10000.0
TOP_K = 2
LN_EPS = 1e-5
RMS_EPS = 1e-5
NEG_BIG = -1e30
VMEM_LIMIT = 56 * 1024 * 1024


@dataclasses.dataclass(frozen=True)
class Cfg:
    batch: int
    seq: int
    depth: int
    d_ff_dense: int
    d_ff_expert: int
    n_experts: int
    t_embed: int
    t_proj: int
    t_q: int
    t_k: int
    t_tok: int
    t_ff_dense: int
    t_ff_expert: int
    t_route: int
    t_group: int

    @property
    def lp(self):
        return PAD + N_META + self.seq

    @property
    def tokens(self):
        return self.batch * self.lp


def _params(semantics):
    return pltpu.CompilerParams(dimension_semantics=semantics, vmem_limit_bytes=VMEM_LIMIT)


def _layer_norm(y, g, b):
    mu = jnp.mean(y, axis=-1, keepdims=True)
    yc = y - mu
    var = jnp.mean(yc * yc, axis=-1, keepdims=True)
    return yc * lax.rsqrt(var + LN_EPS) * g + b


def _embed_kernel(x0_ref, x1_ref, x2_ref, meta_ref, g_ref, b_ref, h_ref, hb_ref):
    j = pl.program_id(1)
    first = jnp.where(j == 0, meta_ref[...], x0_ref[0])
    rows = jnp.concatenate([first, x1_ref[0], x2_ref[0]], axis=0)
    y = _layer_norm(rows, g_ref[...], b_ref[...])
    h_ref[0] = y
    hb_ref[0] = y.astype(BF16)


def _embed(x, meta_block, g, b, cfg):
    te = cfg.t_embed
    steps = cfg.lp // (3 * te)

    def x_spec(i):
        return pl.BlockSpec((1, te, D_MODEL), lambda bb, j: (bb, jnp.maximum(3 * j + i - 1, 0), 0))

    const = lambda bb, j: (0, 0)
    out_spec = pl.BlockSpec((1, 3 * te, D_MODEL), lambda bb, j: (bb, j, 0))
    return pl.pallas_call(
        _embed_kernel,
        grid=(cfg.batch, steps),
        in_specs=[x_spec(0), x_spec(1), x_spec(2),
                  pl.BlockSpec((te, D_MODEL), const),
                  pl.BlockSpec((1, D_MODEL), const),
                  pl.BlockSpec((1, D_MODEL), const)],
        out_specs=[out_spec, out_spec],
        out_shape=[jax.ShapeDtypeStruct((cfg.batch, cfg.lp, D_MODEL), F32),
                   jax.ShapeDtypeStruct((cfg.batch, cfg.lp, D_MODEL), BF16)],
        compiler_params=_params(("parallel", "arbitrary")),
        name="embed_ln",
    )(x, x, x, meta_block, g, b)


def _inproj_kernel(hb_ref, w_ref, cw_ref, cos_ref, sin_ref,
                   conv_ref, qa_ref, qb_ref, k_ref, v_ref, carry_ref, *, t_proj):
    j = pl.program_id(1)
    x = hb_ref[0]

    def proj(g):
        return jnp.dot(x, w_ref[:, g * GROUP_W:(g + 1) * GROUP_W], preferred_element_type=F32)

    rid = lax.broadcasted_iota(jnp.int32, (t_proj, 1), 0)
    gated = jnp.where(j * t_proj + rid >= PAD, proj(1) * proj(2), 0.0)

    @pl.when(j == 0)
    def _():
        carry_ref[...] = jnp.zeros_like(carry_ref)

    prev = carry_ref[...]
    back1 = jnp.where(rid == 0, prev[7:8], pltpu.roll(gated, 1, 0))
    back2 = jnp.where(rid == 0, prev[6:7], jnp.where(rid == 1, prev[7:8], pltpu.roll(gated, 2, 0)))
    carry_ref[...] = gated[t_proj - 8:]
    cw = cw_ref[...]
    conv = cw[0:1] * back2 + cw[1:2] * back1 + cw[2:3] * gated
    conv_ref[0] = (proj(0) * conv).astype(BF16)

    cos = cos_ref[...]
    sin = sin_ref[...]
    lane = lax.broadcasted_iota(jnp.int32, (1, LANE), 1)
    low_half = (lane % HEAD_DIM) < HEAD_DIM // 2
    first_map = lane < HEAD_DIM

    def rope(z):
        partner = jnp.where(low_half, pltpu.roll(z, LANE - HEAD_DIM // 2, 1),
                            pltpu.roll(z, HEAD_DIM // 2, 1))
        return z * cos + partner * sin

    zq = proj(3)
    zk = proj(4)
    scale = HEAD_DIM ** -0.5
    for h in range(N_HEADS):
        sl = slice(h * LANE, (h + 1) * LANE)
        q = rope(zq[:, sl]) * scale
        qa_ref[0, :, sl] = jnp.where(first_map, q, 0.0).astype(BF16)
        qb_ref[0, :, sl] = jnp.where(first_map, 0.0, q).astype(BF16)
        k_ref[0, :, sl] = rope(zk[:, sl]).astype(BF16)
    v_ref[0] = proj(5).astype(BF16)


def _inproj(hb, w_in, conv_w, cos, sin, cfg):
    tp = cfg.t_proj
    row = lambda bb, j: (bb, j, 0)
    const = lambda bb, j: (0, 0)
    out_spec = pl.BlockSpec((1, tp, GROUP_W), row)
    out_shape = jax.ShapeDtypeStruct((cfg.batch, cfg.lp, GROUP_W), BF16)
    return pl.pallas_call(
        functools.partial(_inproj_kernel, t_proj=tp),
        grid=(cfg.batch, cfg.lp // tp),
        in_specs=[pl.BlockSpec((1, tp, D_MODEL), row),
                  pl.BlockSpec((D_MODEL, 6 * GROUP_W), const),
                  pl.BlockSpec((CONV_WIDTH, D_CONV), const),
                  pl.BlockSpec((tp, LANE), lambda bb, j: (j, 0)),
                  pl.BlockSpec((tp, LANE), lambda bb, j: (j, 0))],
        out_specs=[out_spec] * 5,
        out_shape=[out_shape] * 5,
        scratch_shapes=[pltpu.VMEM((8, D_CONV), F32)],
        compiler_params=_params(("parallel", "arbitrary")),
        name="inproj_conv_rope",
    )(hb, w_in, conv_w, cos, sin)


def _attn_kernel(lq1_ref, lk1_ref, lq2_ref, lk2_ref, g_ref, qa_ref, qb_ref, k_ref, v_ref,
                 o_ref, m_ref, l_ref, acc_ref, *, t_q, t_k, lambda_init):
    j = pl.program_id(2)
    q2 = jnp.concatenate([qa_ref[0], qb_ref[0]], axis=0)
    m_ref[...] = jnp.full_like(m_ref, NEG_BIG)
    l_ref[...] = jnp.zeros_like(l_ref)
    acc_ref[...] = jnp.zeros_like(acc_ref)

    def step(c, masked):
        start = pl.multiple_of(c * t_k, t_k)
        k = k_ref[0, pl.ds(start, t_k), :]
        v = v_ref[0, pl.ds(start, t_k), :]
        s = lax.dot_general(q2, k, (((1,), (1,)), ((), ())), preferred_element_type=F32)
        if masked:
            qpos = j * t_q + lax.broadcasted_iota(jnp.int32, (2 * t_q, 1), 0) % t_q
            kpos = c * t_k + lax.broadcasted_iota(jnp.int32, (1, t_k), 1)
            s = jnp.where(kpos <= qpos, jnp.where(kpos >= PAD, s, NEG_BIG), NEG_BIG)
        m_prev = m_ref[...]
        m_new = jnp.maximum(m_prev, jnp.max(s, axis=-1, keepdims=True))
        alpha = jnp.exp(m_prev - m_new)
        p = jnp.exp(s - jnp.concatenate([m_new] * (t_k // LANE), axis=1))
        l_ref[...] = alpha * l_ref[...] + jnp.sum(p, axis=-1, keepdims=True)
        acc_ref[...] = alpha * acc_ref[...] + jnp.dot(p.astype(BF16), v, preferred_element_type=F32)
        m_ref[...] = m_new

    step(0, True)

    def body(c, carry):
        step(c, False)
        return carry

    lax.fori_loop(1, j, body, 0)

    @pl.when(j > 0)
    def _():
        step(j, True)

    lam = (jnp.exp(jnp.sum(lq1_ref[...] * lk1_ref[...], axis=-1, keepdims=True))
           - jnp.exp(jnp.sum(lq2_ref[...] * lk2_ref[...], axis=-1, keepdims=True)) + lambda_init)
    o = acc_ref[...] / l_ref[...]
    d = o[:t_q] - lam * o[t_q:]
    d = d * lax.rsqrt(jnp.mean(d * d, axis=-1, keepdims=True) + RMS_EPS) * g_ref[...]
    o_ref[0] = (d * (1.0 - lambda_init)).astype(BF16)


def _attention(qa, qb, k, v, lq1, lk1, lq2, lk2, subln_g, lambda_init, cfg):
    tq = cfg.t_q
    small = lambda bb, h, j: (0, 0)
    q_spec = pl.BlockSpec((1, tq, LANE), lambda bb, h, j: (bb, j, h))
    kv_spec = pl.BlockSpec((1, cfg.lp, LANE), lambda bb, h, j: (bb, 0, h))
    return pl.pallas_call(
        functools.partial(_attn_kernel, t_q=tq, t_k=cfg.t_k, lambda_init=lambda_init),
        grid=(cfg.batch, N_HEADS, cfg.lp // tq),
        in_specs=[pl.BlockSpec((1, HEAD_DIM), small)] * 4
        + [pl.BlockSpec((1, V_DIM), small), q_spec, q_spec, kv_spec, kv_spec],
        out_specs=q_spec,
        out_shape=jax.ShapeDtypeStruct((cfg.batch, cfg.lp, N_HEADS * V_DIM), BF16),
        scratch_shapes=[pltpu.VMEM((2 * tq, LANE), F32),
                        pltpu.VMEM((2 * tq, LANE), F32),
                        pltpu.VMEM((2 * tq, V_DIM), F32)],
        compiler_params=_params(("parallel", "parallel", "arbitrary")),
        name="diff_attention",
    )(lq1, lk1, lq2, lk2, subln_g, qa, qb, k, v)


def _outproj_kernel(conv_ref, attn_ref, w_ref, h_ref, g_ref, b_ref, ho_ref, hbo_ref, *, alpha):
    mix = (jnp.dot(conv_ref[...], w_ref[:D_CONV], preferred_element_type=F32)
           + jnp.dot(attn_ref[...], w_ref[D_CONV:], preferred_element_type=F32))
    y = _layer_norm(alpha * h_ref[...] + mix, g_ref[...], b_ref[...])
    ho_ref[...] = y
    hbo_ref[...] = y.astype(BF16)


def _outproj(conv, attn, w_out, h, g, b, cfg, alpha):
    tt = cfg.t_tok
    row = lambda i: (i, 0)
    const = lambda i: (0, 0)
    return pl.pallas_call(
        functools.partial(_outproj_kernel, alpha=alpha),
        grid=(cfg.tokens // tt,),
        in_specs=[pl.BlockSpec((tt, D_CONV), row),
                  pl.BlockSpec((tt, D_MODEL - D_CONV), row),
                  pl.BlockSpec((D_MODEL, D_MODEL), const),
                  pl.BlockSpec((tt, D_MODEL), row),
                  pl.BlockSpec((1, D_MODEL), const),
                  pl.BlockSpec((1, D_MODEL), const)],
        out_specs=[pl.BlockSpec((tt, D_MODEL), row)] * 2,
        out_shape=[jax.ShapeDtypeStruct((cfg.tokens, D_MODEL), F32),
                   jax.ShapeDtypeStruct((cfg.tokens, D_MODEL), BF16)],
        compiler_params=_params(("parallel",)),
        name="outproj_ln",
    )(conv, attn, w_out, h, g, b)


def _swiglu_contrib(x, wg_ref, wu_ref, wd_ref):
    gate = jnp.dot(x, wg_ref[...], preferred_element_type=F32)
    up = jnp.dot(x, wu_ref[...], preferred_element_type=F32)
    act = gate * (1.0 / (1.0 + jnp.exp(-gate))) * up
    return jnp.dot(act.astype(BF16), wd_ref[...], preferred_element_type=F32)


def _ffn_kernel(x_ref, wg_ref, wu_ref, wd_ref, h_ref, g_ref, b_ref, ho_ref, hbo_ref, *, alpha, t_ff):
    x = x_ref[...]
    acc = None
    for c in range(wg_ref.shape[1] // t_ff):
        cols = pl.ds(c * t_ff, t_ff)
        contrib = _swiglu_contrib(x, wg_ref.at[:, cols], wu_ref.at[:, cols], wd_ref.at[cols, :])
        acc = contrib if acc is None else acc + contrib
    y = _layer_norm(alpha * h_ref[...] + acc, g_ref[...], b_ref[...])
    ho_ref[...] = y
    hbo_ref[...] = y.astype(BF16)


def _ffn(hb, wg, wu, wd, h, g, b, cfg, alpha):
    tt = cfg.t_tok
    row = lambda i: (i, 0)
    const = lambda i: (0, 0)
    resident = lambda shape: pl.BlockSpec(shape, const, pipeline_mode=pl.Buffered(1))
    return pl.pallas_call(
        functools.partial(_ffn_kernel, alpha=alpha, t_ff=cfg.t_ff_dense),
        grid=(cfg.tokens // tt,),
        in_specs=[pl.BlockSpec((tt, D_MODEL), row),
                  resident((D_MODEL, cfg.d_ff_dense)),
                  resident((D_MODEL, cfg.d_ff_dense)),
                  resident((cfg.d_ff_dense, D_MODEL)),
                  pl.BlockSpec((tt, D_MODEL), row),
                  pl.BlockSpec((1, D_MODEL), const),
                  pl.BlockSpec((1, D_MODEL), const)],
        out_specs=[pl.BlockSpec((tt, D_MODEL), row)] * 2,
        out_shape=[jax.ShapeDtypeStruct((cfg.tokens, D_MODEL), F32),
                   jax.ShapeDtypeStruct((cfg.tokens, D_MODEL), BF16)],
        compiler_params=_params(("parallel",)),
        name="dense_ffn_ln",
    )(hb, wg, wu, wd, h, g, b)


INFO_E1, INFO_E2, INFO_W1, INFO_W2, INFO_R1, INFO_R2 = range(6)


def _lane_pick(values, lane, index):
    return jnp.sum(jnp.where(lane == index, values, 0.0), axis=-1, keepdims=True)


def _router_kernel(h_ref, wr_ref, info_ref, cnt_ref, seen_ref, *, n_experts):
    @pl.when(jnp.logical_and(pl.program_id(0) == 0, pl.program_id(1) == 0))
    def _():
        seen_ref[...] = jnp.zeros_like(seen_ref)

    logits = jnp.dot(h_ref[0], wr_ref[...], preferred_element_type=F32,
                     precision=lax.Precision.HIGHEST)
    t_route = logits.shape[0]
    lane = lax.broadcasted_iota(jnp.int32, logits.shape, 1).astype(F32)
    logits = jnp.where(lane < n_experts, logits, -jnp.inf)
    v1 = jnp.max(logits, axis=-1, keepdims=True)
    e1 = jnp.min(jnp.where(logits == v1, lane, float(LANE)), axis=-1, keepdims=True)
    rest = jnp.where(lane == e1, -jnp.inf, logits)
    v2 = jnp.max(rest, axis=-1, keepdims=True)
    e2 = jnp.min(jnp.where(rest == v2, lane, float(LANE)), axis=-1, keepdims=True)
    ex = jnp.exp(v2 - v1)
    w1 = 1.0 / (1.0 + ex)
    w2 = ex / (1.0 + ex)

    chosen = jnp.where(lane == e1, 1.0, 0.0) + jnp.where(lane == e2, 1.0, 0.0)
    r = lax.broadcasted_iota(jnp.int32, (t_route, t_route), 0)
    c = lax.broadcasted_iota(jnp.int32, (t_route, t_route), 1)
    earlier = jnp.where(c < r, 1.0, 0.0).astype(BF16)
    before = jnp.dot(earlier, chosen.astype(BF16), preferred_element_type=F32) + seen_ref[...]
    r1 = _lane_pick(before, lane, e1)
    r2 = _lane_pick(before, lane, e2)
    seen_ref[...] += jnp.sum(chosen, axis=0, keepdims=True)
    cnt_ref[...] = seen_ref[...]

    record = jnp.zeros_like(logits)
    for slot, val in ((INFO_E1, e1), (INFO_E2, e2), (INFO_W1, w1), (INFO_W2, w2),
                      (INFO_R1, r1), (INFO_R2, r2)):
        record = jnp.where(lane == slot, val, record)
    info_ref[0] = record


def _seq_rows_spec(t_rows, width):
    return pl.BlockSpec((pl.Element(1), pl.Element(t_rows), pl.Element(width)),
                        lambda bb, j: (bb, pl.multiple_of(PAD + N_META + j * t_rows, LANE), 0))


def _router(h3, wr_pad, cfg):
    tr = cfg.t_route
    return pl.pallas_call(
        functools.partial(_router_kernel, n_experts=cfg.n_experts),
        grid=(cfg.batch, cfg.seq // tr),
        in_specs=[_seq_rows_spec(tr, D_MODEL),
                  pl.BlockSpec((D_MODEL, LANE), lambda bb, j: (0, 0))],
        out_specs=[pl.BlockSpec((1, tr, LANE), lambda bb, j: (bb, j, 0)),
                   pl.BlockSpec((1, LANE), lambda bb, j: (0, 0))],
        out_shape=[jax.ShapeDtypeStruct((cfg.batch, cfg.seq, LANE), F32),
                   jax.ShapeDtypeStruct((1, LANE), F32)],
        scratch_shapes=[pltpu.VMEM((1, LANE), F32)],
        compiler_params=_params(("arbitrary", "arbitrary")),
        name="router_top2",
    )(h3, wr_pad)


ZERO_ROWS = 8

def _dispatch_kernel(pos_ref, fill_ref, h_ref, xs_ref, zero_ref, sem, *, t_route, n_experts):
    def zero_copy(p, n):
        start = pl.multiple_of(p * n, n)
        return pltpu.make_async_copy(zero_ref.at[pl.ds(0, n)], xs_ref.at[pl.ds(start, n)], sem.at[2])

    def zero_rows(lo, hi, n):
        def start(p, carry):
            zero_copy(p, n).start()
            return carry

        def wait(p, carry):
            zero_copy(p, n).wait()
            return carry

        lax.fori_loop(lo, hi, start, 0)
        lax.fori_loop(lo, hi, wait, 0)

    @pl.when(jnp.logical_and(pl.program_id(0) == 0, pl.program_id(1) == 0))
    def _():
        zero_ref[...] = jnp.zeros_like(zero_ref)
        for e in range(n_experts):
            zero_rows(fill_ref[0, e], fill_ref[1, e], 1)
        zero_rows(fill_ref[0, n_experts], fill_ref[1, n_experts], ZERO_ROWS)

    def issue(r, carry):
        for slot in range(TOP_K):
            pltpu.make_async_copy(h_ref.at[0, pl.ds(r, 1)],
                                  xs_ref.at[pl.ds(pos_ref[0, 0, TOP_K * r + slot], 1)],
                                  sem.at[slot]).start()
        return carry

    lax.fori_loop(0, t_route, issue, 0)
    for slot in range(TOP_K):
        pltpu.make_async_copy(h_ref.at[0], xs_ref.at[pl.ds(0, t_route)], sem.at[slot]).wait()


def _dispatch(pos, fill, h3, cfg, rows):
    tr = cfg.t_route
    steps = cfg.seq // tr
    return pl.pallas_call(
        functools.partial(_dispatch_kernel, t_route=tr, n_experts=cfg.n_experts),
        grid=(cfg.batch, steps),
        in_specs=[pl.BlockSpec((1, 1, TOP_K * tr), lambda bb, j: (bb * steps + j, 0, 0),
                               memory_space=pltpu.SMEM),
                  pl.BlockSpec(memory_space=pltpu.SMEM),
                  _seq_rows_spec(tr, D_MODEL)],
        out_specs=pl.BlockSpec(memory_space=pl.ANY),
        out_shape=jax.ShapeDtypeStruct((rows, D_MODEL), F32),
        scratch_shapes=[pltpu.VMEM((ZERO_ROWS, D_MODEL), F32), pltpu.SemaphoreType.DMA((3,))],
        compiler_params=_params(("arbitrary", "arbitrary")),
        name="moe_dispatch",
    )(pos, fill, h3)


def _grouped_kernel(te_ref, nv_ref, x_ref, wg_ref, wu_ref, wd_ref, y_ref, xb_ref, acc_ref):
    f = pl.program_id(1)

    @pl.when(pl.program_id(0) < nv_ref[0])
    def _():
        @pl.when(f == 0)
        def _():
            xb_ref[...] = x_ref[...].astype(BF16)

        contrib = _swiglu_contrib(xb_ref[...], wg_ref.at[0], wu_ref.at[0], wd_ref.at[0])

        @pl.when(f == 0)
        def _():
            acc_ref[...] = contrib

        @pl.when(f > 0)
        def _():
            acc_ref[...] += contrib

        @pl.when(f == pl.num_programs(1) - 1)
        def _():
            y_ref[...] = acc_ref[...]

    @pl.when(pl.program_id(0) >= nv_ref[0])
    def _():
        y_ref[...] = jnp.zeros_like(y_ref)


def _grouped_ffn(tile_expert, n_valid, xs, wg, wu, wd, cfg):
    tg, tf = cfg.t_group, cfg.t_ff_expert
    n_f = cfg.d_ff_expert // tf
    rows = xs.shape[0]

    def live(i, nv):
        return jnp.minimum(i, nv[0] - 1)

    def f_idx(i, f, nv):
        return jnp.where(i < nv[0], f, n_f - 1)

    grid_spec = pltpu.PrefetchScalarGridSpec(
        num_scalar_prefetch=2,
        grid=(rows // tg, n_f),
        in_specs=[pl.BlockSpec((tg, D_MODEL), lambda i, f, te, nv: (live(i, nv), 0)),
                  pl.BlockSpec((1, D_MODEL, tf), lambda i, f, te, nv: (te[i], 0, f_idx(i, f, nv))),
                  pl.BlockSpec((1, D_MODEL, tf), lambda i, f, te, nv: (te[i], 0, f_idx(i, f, nv))),
                  pl.BlockSpec((1, tf, D_MODEL), lambda i, f, te, nv: (te[i], f_idx(i, f, nv), 0))],
        out_specs=pl.BlockSpec((tg, D_MODEL), lambda i, f, te, nv: (i, 0)),
        scratch_shapes=[pltpu.VMEM((tg, D_MODEL), BF16), pltpu.VMEM((tg, D_MODEL), F32)],
    )
    return pl.pallas_call(
        _grouped_kernel,
        grid_spec=grid_spec,
        out_shape=jax.ShapeDtypeStruct((rows, D_MODEL), F32),
        compiler_params=_params(("arbitrary", "arbitrary")),
        name="moe_grouped_ffn",
    )(tile_expert, n_valid, xs, wg, wu, wd)


def _combine_kernel(pos_ref, info_ref, h_ref, g_ref, b_ref, ys_ref, out_ref, rows_ref, sem, *, alpha, t_route):
    def issue(r, carry):
        for slot in range(TOP_K):
            pltpu.make_async_copy(ys_ref.at[pl.ds(pos_ref[0, 0, TOP_K * r + slot], 1)],
                                  rows_ref.at[slot, pl.ds(r, 1)], sem.at[slot]).start()
        return carry

    lax.fori_loop(0, t_route, issue, 0)
    for slot in range(TOP_K):
        pltpu.make_async_copy(ys_ref.at[pl.ds(0, t_route)], rows_ref.at[slot], sem.at[slot]).wait()

    info = info_ref[0]
    lane = lax.broadcasted_iota(jnp.int32, info.shape, 1)
    ffn = (_lane_pick(info, lane, INFO_W1) * rows_ref[0] + _lane_pick(info, lane, INFO_W2) * rows_ref[1])
    out_ref[0] = _layer_norm(alpha * h_ref[0] + ffn, g_ref[...], b_ref[...])


def _combine(pos, info, h3, g, b, ys, cfg, alpha):
    tr = cfg.t_route
    steps = cfg.seq // tr
    const = lambda bb, j: (0, 0)
    return pl.pallas_call(
        functools.partial(_combine_kernel, alpha=alpha, t_route=tr),
        grid=(cfg.batch, steps),
        in_specs=[pl.BlockSpec((1, 1, TOP_K * tr), lambda bb, j: (bb * steps + j, 0, 0),
                               memory_space=pltpu.SMEM),
                  pl.BlockSpec((1, tr, LANE), lambda bb, j: (bb, j, 0)),
                  _seq_rows_spec(tr, D_MODEL),
                  pl.BlockSpec((1, D_MODEL), const),
                  pl.BlockSpec((1, D_MODEL), const),
                  pl.BlockSpec(memory_space=pl.ANY)],
        out_specs=pl.BlockSpec((1, tr, D_MODEL), lambda bb, j: (bb, j, 0)),
        out_shape=jax.ShapeDtypeStruct((cfg.batch, cfg.seq, D_MODEL), F32),
        scratch_shapes=[pltpu.VMEM((TOP_K, tr, D_MODEL), F32), pltpu.SemaphoreType.DMA((TOP_K,))],
        compiler_params=_params(("arbitrary", "arbitrary")),
        name="moe_combine_ln",
    )(pos, info, h3, g, b, ys)


def _routed_moe(h3, w_router, wg, wu, wd, g, b, cfg, alpha):
    n_e, tg, tr = cfg.n_experts, cfg.t_group, cfg.t_route
    n_tok = cfg.batch * cfg.seq
    rows = TOP_K * n_tok + n_e * tg
    wr_pad = jnp.pad(w_router, ((0, 0), (0, LANE - n_e)))
    info, seen = _router(h3, wr_pad, cfg)

    flat = info.reshape(n_tok, LANE)
    as_int = lambda col: flat[:, col].astype(jnp.int32)
    counts = seen[0, :n_e].astype(jnp.int32)
    padded = (counts + tg - 1) // tg * tg
    ends = jnp.cumsum(padded)
    starts = ends - padded
    pos = jnp.stack([starts[as_int(INFO_E1)] + as_int(INFO_R1),
                     starts[as_int(INFO_E2)] + as_int(INFO_R2)], axis=-1)
    pos = pos.reshape(n_tok // tr, 1, TOP_K * tr)
    fill = jnp.stack([jnp.append(starts + counts, ends[-1] // ZERO_ROWS),
                      jnp.append(ends, rows // ZERO_ROWS)])
    n_valid = (ends[-1] // tg).reshape(1)
    tile_start = jnp.arange(rows // tg, dtype=jnp.int32) * tg
    tile_expert = jnp.sum(tile_start[:, None] >= ends[None, :], axis=-1).astype(jnp.int32)
    tile_expert = tile_expert[jnp.minimum(jnp.arange(rows // tg), n_valid[0] - 1)]

    xs = _dispatch(pos, fill, h3, cfg, rows)
    ys = _grouped_ffn(tile_expert, n_valid, xs, wg, wu, wd, cfg)
    return _combine(pos, info, h3, g, b, ys, cfg, alpha)


def _rope_tables(lp):
    half = HEAD_DIM // 2
    inv = 1.0 / (ROPE_THETA ** (jnp.arange(0, HEAD_DIM, 2, dtype=F32) / HEAD_DIM))
    pos = (jnp.arange(lp) - PAD).astype(F32)
    ang = pos[:, None] * inv[None, :]
    ang = jnp.tile(ang, (1, LANE // half))
    sign = jnp.where((jnp.arange(LANE) % HEAD_DIM) < half, -1.0, 1.0).astype(F32)
    return jnp.cos(ang), jnp.sin(ang) * sign[None, :]


def _forward(cfg, x, meta_tokens, ln_emb_g, ln_emb_b, w_in, conv_w, lambda_q1, lambda_k1,
             lambda_q2, lambda_k2, subln_g, w_out, ln_mix_g, ln_mix_b, ln_ffn_g, ln_ffn_b,
             w_gate_dense, w_up_dense, w_down_dense, w_router, w_gate_moe, w_up_moe, w_down_moe):
    alpha = (2 * cfg.depth) ** 0.25
    tokens = cfg.tokens
    vec = lambda a: a.reshape(1, -1)
    cos, sin = _rope_tables(cfg.lp)
    meta_block = jnp.pad(meta_tokens, ((cfg.t_embed - N_META, 0), (0, 0)))
    h, hb = _embed(x, meta_block, vec(ln_emb_g), vec(ln_emb_b), cfg)
    h = h.reshape(tokens, D_MODEL)
    for layer in range(cfg.depth):
        lambda_init = 0.8 - 0.6 * math.exp(-0.3 * layer)
        conv, qa, qb, k, v = _inproj(hb.reshape(cfg.batch, cfg.lp, D_MODEL), w_in[layer].astype(BF16),
                                     conv_w[layer], cos, sin, cfg)
        attn = _attention(qa, qb, k, v, vec(lambda_q1[layer]), vec(lambda_k1[layer]),
                          vec(lambda_q2[layer]), vec(lambda_k2[layer]), vec(subln_g[layer]),
                          lambda_init, cfg)
        h, hb = _outproj(conv.reshape(tokens, D_CONV), attn.reshape(tokens, D_MODEL - D_CONV),
                         w_out[layer].astype(BF16), h, vec(ln_mix_g[layer]), vec(ln_mix_b[layer]),
                         cfg, alpha)
        idx = layer // 2
        g, b = vec(ln_ffn_g[layer]), vec(ln_ffn_b[layer])
        if layer % 2 == 0:
            h, hb = _ffn(hb, w_gate_dense[idx].astype(BF16), w_up_dense[idx].astype(BF16),
                         w_down_dense[idx].astype(BF16), h, g, b, cfg, alpha)
        else:
            assert layer == cfg.depth - 1
            return _routed_moe(h.reshape(cfg.batch, cfg.lp, D_MODEL), w_router[idx],
                               w_gate_moe[idx].astype(BF16), w_up_moe[idx].astype(BF16),
                               w_down_moe[idx].astype(BF16), g, b, cfg, alpha)
    return h.reshape(cfg.batch, cfg.lp, D_MODEL)[:, PAD + N_META:]


_CFG = Cfg(batch=8, seq=4096, depth=2, d_ff_dense=2816, d_ff_expert=3584, n_experts=8,
           t_embed=128, t_proj=1056, t_q=384, t_k=384, t_tok=1024, t_ff_dense=256, t_ff_expert=896,
           t_route=512, t_group=1024)


def kernel(x, meta_tokens, ln_emb_g, ln_emb_b, w_in, conv_w, lambda_q1, lambda_k1, lambda_q2, lambda_k2, subln_g, w_out, ln_mix_g, ln_mix_b, ln_ffn_g, ln_ffn_b, w_gate_dense, w_up_dense, w_down_dense, w_router, w_gate_moe, w_up_moe, w_down_moe):
    return _forward(_CFG, x, meta_tokens, ln_emb_g, ln_emb_b, w_in, conv_w, lambda_q1, lambda_k1,
                    lambda_q2, lambda_k2, subln_g, w_out, ln_mix_g, ln_mix_b, ln_ffn_g, ln_ffn_b,
                    w_gate_dense, w_up_dense, w_down_dense, w_router, w_gate_moe, w_up_moe, w_down_moe)
```

```python
import dataclasses
import functools
import math

import jax
import jax.numpy as jnp
from jax import lax
from jax.experimental import pallas as pl
from jax.experimental.pallas import tpu as pltpu

F32 = jnp.float32
BF16 = jnp.bfloat16

LANE = 128
D_MODEL = 1024
N_META = 16
PAD = LANE - N_META
D_CONV = D_MODEL // 2
CONV_WIDTH = 3
HEAD_DIM = 64
N_HEADS = 4
V_DIM = 2 * HEAD_DIM
QK_W = N_HEADS * 2 * HEAD_DIM
GROUP_W = 512
ROPE_THETA = 10000.0
TOP_K = 2
LN_EPS = 1e-5
RMS_EPS = 1e-5
NEG_BIG = -1e30
VMEM_LIMIT = 56 * 1024 * 1024


@dataclasses.dataclass(frozen=True)
class Cfg:
    batch: int
    seq: int
    depth: int
    d_ff_dense: int
    d_ff_expert: int
    n_experts: int
    t_embed: int
    t_proj: int
    t_q: int
    t_k: int
    t_tok: int
    t_ff_dense: int
    t_ff_expert: int
    t_route: int
    t_group: int

    @property
    def lp(self):
        return N_META + self.seq + PAD

    @property
    def tokens(self):
        return self.batch * self.lp


def _params(semantics):
    return pltpu.CompilerParams(dimension_semantics=semantics, vmem_limit_bytes=VMEM_LIMIT)


def _layer_norm(y, g, b):
    mu = jnp.mean(y, axis=-1, keepdims=True)
    yc = y - mu
    var = jnp.mean(yc * yc, axis=-1, keepdims=True)
    return yc * lax.rsqrt(var + LN_EPS) * g + b


def _embed_kernel(x0_ref, x1_ref, x2_ref, x3_ref, meta_ref, g_ref, b_ref, h_ref, hb_ref, *, seq):
    j = pl.program_id(1)
    te = x1_ref.shape[1]
    head = jnp.where(j == 0, meta_ref[...], x0_ref[0, te - N_META:])
    rows = jnp.concatenate([head, x1_ref[0], x2_ref[0], x3_ref[0, :te - N_META]], axis=0)
    rid = j * (3 * te) + lax.broadcasted_iota(jnp.int32, (3 * te, 1), 0)
    rows = jnp.where(rid < N_META + seq, rows, 0.0)
    y = _layer_norm(rows, g_ref[...], b_ref[...])
    h_ref[0] = y
    hb_ref[0] = y.astype(BF16)


def _embed(x, meta_tokens, g, b, cfg):
    te = cfg.t_embed
    steps = cfg.lp // (3 * te)
    last = cfg.seq // te - 1

    def x_spec(i):
        return pl.BlockSpec((1, te, D_MODEL), lambda bb, j: (bb, jnp.clip(3 * j + i - 1, 0, last), 0))

    const = lambda bb, j: (0, 0)
    out_spec = pl.BlockSpec((1, 3 * te, D_MODEL), lambda bb, j: (bb, j, 0))
    return pl.pallas_call(
        functools.partial(_embed_kernel, seq=cfg.seq),
        grid=(cfg.batch, steps),
        in_specs=[x_spec(0), x_spec(1), x_spec(2), x_spec(3),
                  pl.BlockSpec((N_META, D_MODEL), const),
                  pl.BlockSpec((1, D_MODEL), const),
                  pl.BlockSpec((1, D_MODEL), const)],
        out_specs=[out_spec, out_spec],
        out_shape=[jax.ShapeDtypeStruct((cfg.batch, cfg.lp, D_MODEL), F32),
                   jax.ShapeDtypeStruct((cfg.batch, cfg.lp, D_MODEL), BF16)],
        compiler_params=_params(("parallel", "arbitrary")),
        name="embed_ln",
    )(x, x, x, x, meta_tokens, g, b)


def _inproj_kernel(hb_ref, w_ref, cw_ref, cos_ref, sin_ref,
                   conv_ref, qa_ref, qb_ref, k_ref, v_ref, carry_ref, *, t_proj):
    j = pl.program_id(1)
    x = hb_ref[0]

    def proj(g):
        return jnp.dot(x, w_ref[:, g * GROUP_W:(g + 1) * GROUP_W], preferred_element_type=F32)

    rid = lax.broadcasted_iota(jnp.int32, (t_proj, 1), 0)
    gated = proj(1) * proj(2)

    @pl.when(j == 0)
    def _():
        carry_ref[...] = jnp.zeros_like(carry_ref)

    prev = carry_ref[...]
    back1 = jnp.where(rid == 0, prev[7:8], pltpu.roll(gated, 1, 0))
    back2 = jnp.where(rid == 0, prev[6:7], jnp.where(rid == 1, prev[7:8], pltpu.roll(gated, 2, 0)))
    carry_ref[...] = gated[t_proj - 8:]
    cw = cw_ref[...]
    conv = cw[0:1] * back2 + cw[1:2] * back1 + cw[2:3] * gated
    conv_ref[0] = (proj(0) * conv).astype(BF16)

    cos = cos_ref[...]
    sin = sin_ref[...]
    lane = lax.broadcasted_iota(jnp.int32, (1, LANE), 1)
    low_half = (lane % HEAD_DIM) < HEAD_DIM // 2
    first_map = lane < HEAD_DIM

    def rope(z):
        partner = jnp.where(low_half, pltpu.roll(z, LANE - HEAD_DIM // 2, 1),
                            pltpu.roll(z, HEAD_DIM // 2, 1))
        return z * cos + partner * sin

    zq = proj(3)
    zk = proj(4)
    scale = HEAD_DIM ** -0.5 * math.log2(math.e)
    for h in range(N_HEADS):
        sl = slice(h * LANE, (h + 1) * LANE)
        q = rope(zq[:, sl]) * scale
        qa_ref[0, :, sl] = jnp.where(first_map, q, 0.0).astype(BF16)
        qb_ref[0, :, sl] = jnp.where(first_map, 0.0, q).astype(BF16)
        k_ref[0, :, sl] = rope(zk[:, sl]).astype(BF16)
    v_ref[0] = proj(5).astype(BF16)


def _inproj(hb, w_in, conv_w, cos, sin, cfg):
    tp = cfg.t_proj
    row = lambda bb, j: (bb, j, 0)
    const = lambda bb, j: (0, 0)
    out_spec = pl.BlockSpec((1, tp, GROUP_W), row)
    out_shape = jax.ShapeDtypeStruct((cfg.batch, cfg.lp, GROUP_W), BF16)
    return pl.pallas_call(
        functools.partial(_inproj_kernel, t_proj=tp),
        grid=(cfg.batch, cfg.lp // tp),
        in_specs=[pl.BlockSpec((1, tp, D_MODEL), row),
                  pl.BlockSpec((D_MODEL, 6 * GROUP_W), const),
                  pl.BlockSpec((CONV_WIDTH, D_CONV), const),
                  pl.BlockSpec((tp, LANE), lambda bb, j: (j, 0)),
                  pl.BlockSpec((tp, LANE), lambda bb, j: (j, 0))],
        out_specs=[out_spec] * 5,
        out_shape=[out_shape] * 5,
        scratch_shapes=[pltpu.VMEM((8, D_CONV), F32)],
        compiler_params=_params(("parallel", "arbitrary")),
        name="inproj_conv_rope",
    )(hb, w_in, conv_w, cos, sin)


def _attn_kernel(lq1_ref, lk1_ref, lq2_ref, lk2_ref, g_ref, qa_ref, qb_ref, k_ref, v_ref,
                 o_ref, m_ref, l_ref, acc_ref, *, t_q, t_k, lambda_init):
    j = pl.program_id(1)
    head = lambda h: slice(h * LANE, (h + 1) * LANE)
    q2 = [jnp.concatenate([qa_ref[0, :, head(h)], qb_ref[0, :, head(h)]], axis=0) for h in range(N_HEADS)]
    m_ref[...] = jnp.full_like(m_ref, NEG_BIG)
    l_ref[...] = jnp.zeros_like(l_ref)
    acc_ref[...] = jnp.zeros_like(acc_ref)

    def step(c, causal):
        start = pl.multiple_of(c * t_k, t_k)
        for h in range(N_HEADS):
            k = k_ref[0, pl.ds(start, t_k), head(h)]
            v = v_ref[0, pl.ds(start, t_k), head(h)]
            s = lax.dot_general(q2[h], k, (((1,), (1,)), ((), ())), preferred_element_type=F32)
            if causal:
                qpos = lax.broadcasted_iota(jnp.int32, (2 * t_q, 1), 0) % t_q
                kpos = lax.broadcasted_iota(jnp.int32, (1, t_k), 1)
                s = jnp.where(kpos <= qpos, s, NEG_BIG)
            m_prev = m_ref[h]
            m_new = jnp.maximum(m_prev, jnp.max(s, axis=-1, keepdims=True))
            alpha = jnp.exp2(m_prev - m_new)
            p = jnp.exp2(s - jnp.concatenate([m_new] * (t_k // LANE), axis=1))
            l_ref[h] = alpha * l_ref[h] + jnp.sum(p, axis=-1, keepdims=True)
            acc_ref[h] = alpha * acc_ref[h] + jnp.dot(p.astype(BF16), v, preferred_element_type=F32)
            m_ref[h] = m_new

    def body(c, carry):
        step(c, causal=False)
        return carry

    lax.fori_loop(0, j, body, 0)
    step(j, causal=True)

    lam = (jnp.exp(jnp.sum(lq1_ref[...] * lk1_ref[...], axis=-1, keepdims=True))
           - jnp.exp(jnp.sum(lq2_ref[...] * lk2_ref[...], axis=-1, keepdims=True)) + lambda_init)
    for h in range(N_HEADS):
        o = acc_ref[h] / l_ref[h]
        d = o[:t_q] - lam * o[t_q:]
        d = d * lax.rsqrt(jnp.mean(d * d, axis=-1, keepdims=True) + RMS_EPS) * g_ref[...]
        o_ref[0, :, head(h)] = (d * (1.0 - lambda_init)).astype(BF16)


def _attention(qa, qb, k, v, lq1, lk1, lq2, lk2, subln_g, lambda_init, cfg):
    tq = cfg.t_q
    width = N_HEADS * V_DIM
    small = lambda bb, j: (0, 0)
    q_spec = pl.BlockSpec((1, tq, width), lambda bb, j: (bb, j, 0))
    kv_spec = pl.BlockSpec((1, cfg.lp, width), lambda bb, j: (bb, 0, 0))
    return pl.pallas_call(
        functools.partial(_attn_kernel, t_q=tq, t_k=cfg.t_k, lambda_init=lambda_init),
        grid=(cfg.batch, cfg.lp // tq),
        in_specs=[pl.BlockSpec((1, HEAD_DIM), small)] * 4
        + [pl.BlockSpec((1, V_DIM), small), q_spec, q_spec, kv_spec, kv_spec],
        out_specs=q_spec,
        out_shape=jax.ShapeDtypeStruct((cfg.batch, cfg.lp, width), BF16),
        scratch_shapes=[pltpu.VMEM((N_HEADS, 2 * tq, LANE), F32),
                        pltpu.VMEM((N_HEADS, 2 * tq, LANE), F32),
                        pltpu.VMEM((N_HEADS, 2 * tq, V_DIM), F32)],
        compiler_params=_params(("parallel", "arbitrary")),
        name="diff_attention",
    )(lq1, lk1, lq2, lk2, subln_g, qa, qb, k, v)


def _outproj_kernel(conv_ref, attn_ref, w_ref, h_ref, g_ref, b_ref, ho_ref, hbo_ref, *, alpha):
    mix = (jnp.dot(conv_ref[...], w_ref[:D_CONV], preferred_element_type=F32)
           + jnp.dot(attn_ref[...], w_ref[D_CONV:], preferred_element_type=F32))
    y = _layer_norm(alpha * h_ref[...] + mix, g_ref[...], b_ref[...])
    ho_ref[...] = y
    hbo_ref[...] = y.astype(BF16)


def _outproj(conv, attn, w_out, h, g, b, cfg, alpha):
    tt = cfg.t_tok
    row = lambda i: (i, 0)
    const = lambda i: (0, 0)
    return pl.pallas_call(
        functools.partial(_outproj_kernel, alpha=alpha),
        grid=(cfg.tokens // tt,),
        in_specs=[pl.BlockSpec((tt, D_CONV), row),
                  pl.BlockSpec((tt, D_MODEL - D_CONV), row),
                  pl.BlockSpec((D_MODEL, D_MODEL), const),
                  pl.BlockSpec((tt, D_MODEL), row),
                  pl.BlockSpec((1, D_MODEL), const),
                  pl.BlockSpec((1, D_MODEL), const)],
        out_specs=[pl.BlockSpec((tt, D_MODEL), row)] * 2,
        out_shape=[jax.ShapeDtypeStruct((cfg.tokens, D_MODEL), F32),
                   jax.ShapeDtypeStruct((cfg.tokens, D_MODEL), BF16)],
        compiler_params=_params(("parallel",)),
        name="outproj_ln",
    )(conv, attn, w_out, h, g, b)


def _swiglu_contrib(x, wg_ref, wu_ref, wd_ref):
    gate = jnp.dot(x, wg_ref[...], preferred_element_type=F32)
    up = jnp.dot(x, wu_ref[...], preferred_element_type=F32)
    act = gate * (1.0 / (1.0 + jnp.exp(-gate))) * up
    return jnp.dot(act.astype(BF16), wd_ref[...], preferred_element_type=F32)


def _ffn_kernel(x_ref, wg_ref, wu_ref, wd_ref, h_ref, g_ref, b_ref, ho_ref, hbo_ref, *, alpha, t_ff):
    x = x_ref[...]
    acc = None
    for c in range(wg_ref.shape[1] // t_ff):
        cols = pl.ds(c * t_ff, t_ff)
        contrib = _swiglu_contrib(x, wg_ref.at[:, cols], wu_ref.at[:, cols], wd_ref.at[cols, :])
        acc = contrib if acc is None else acc + contrib
    y = _layer_norm(alpha * h_ref[...] + acc, g_ref[...], b_ref[...])
    ho_ref[...] = y
    hbo_ref[...] = y.astype(BF16)


def _ffn(hb, wg, wu, wd, h, g, b, cfg, alpha):
    tt = cfg.t_tok
    row = lambda i: (i, 0)
    const = lambda i: (0, 0)
    resident = lambda shape: pl.BlockSpec(shape, const, pipeline_mode=pl.Buffered(1))
    return pl.pallas_call(
        functools.partial(_ffn_kernel, alpha=alpha, t_ff=cfg.t_ff_dense),
        grid=(cfg.tokens // tt,),
        in_specs=[pl.BlockSpec((tt, D_MODEL), row),
                  resident((D_MODEL, cfg.d_ff_dense)),
                  resident((D_MODEL, cfg.d_ff_dense)),
                  resident((cfg.d_ff_dense, D_MODEL)),
                  pl.BlockSpec((tt, D_MODEL), row),
                  pl.BlockSpec((1, D_MODEL), const),
                  pl.BlockSpec((1, D_MODEL), const)],
        out_specs=[pl.BlockSpec((tt, D_MODEL), row)] * 2,
        out_shape=[jax.ShapeDtypeStruct((cfg.tokens, D_MODEL), F32),
                   jax.ShapeDtypeStruct((cfg.tokens, D_MODEL), BF16)],
        compiler_params=_params(("parallel",)),
        name="dense_ffn_ln",
    )(hb, wg, wu, wd, h, g, b)


INFO_E1, INFO_E2, INFO_W1, INFO_W2, INFO_R1, INFO_R2 = range(6)


def _lane_pick(values, lane, index):
    return jnp.sum(jnp.where(lane == index, values, 0.0), axis=-1, keepdims=True)


def _router_kernel(h_ref, wr_ref, info_ref, cnt_ref, seen_ref, *, n_experts):
    @pl.when(jnp.logical_and(pl.program_id(0) == 0, pl.program_id(1) == 0))
    def _():
        seen_ref[...] = jnp.zeros_like(seen_ref)

    logits = jnp.dot(h_ref[0], wr_ref[...], preferred_element_type=F32,
                     precision=lax.Precision.HIGHEST)
    t_route = logits.shape[0]
    lane = lax.broadcasted_iota(jnp.int32, logits.shape, 1).astype(F32)
    logits = jnp.where(lane < n_experts, logits, -jnp.inf)
    v1 = jnp.max(logits, axis=-1, keepdims=True)
    e1 = jnp.min(jnp.where(logits == v1, lane, float(LANE)), axis=-1, keepdims=True)
    rest = jnp.where(lane == e1, -jnp.inf, logits)
    v2 = jnp.max(rest, axis=-1, keepdims=True)
    e2 = jnp.min(jnp.where(rest == v2, lane, float(LANE)), axis=-1, keepdims=True)
    ex = jnp.exp(v2 - v1)
    w1 = 1.0 / (1.0 + ex)
    w2 = ex / (1.0 + ex)

    chosen = jnp.where(lane == e1, 1.0, 0.0) + jnp.where(lane == e2, 1.0, 0.0)
    r = lax.broadcasted_iota(jnp.int32, (t_route, t_route), 0)
    c = lax.broadcasted_iota(jnp.int32, (t_route, t_route), 1)
    earlier = jnp.where(c < r, 1.0, 0.0).astype(BF16)
    before = jnp.dot(earlier, chosen.astype(BF16), preferred_element_type=F32) + seen_ref[...]
    r1 = _lane_pick(before, lane, e1)
    r2 = _lane_pick(before, lane, e2)
    seen_ref[...] += jnp.sum(chosen, axis=0, keepdims=True)
    cnt_ref[...] = seen_ref[...]

    record = jnp.zeros_like(logits)
    for slot, val in ((INFO_E1, e1), (INFO_E2, e2), (INFO_W1, w1), (INFO_W2, w2),
                      (INFO_R1, r1), (INFO_R2, r2)):
        record = jnp.where(lane == slot, val, record)
    info_ref[0] = record


def _seq_rows_spec(t_rows, width):
    return pl.BlockSpec((pl.Element(1), pl.Element(t_rows), pl.Element(width)),
                        lambda bb, j: (bb, pl.multiple_of(N_META + j * t_rows, N_META), 0))


def _router(h3, wr_pad, cfg):
    tr = cfg.t_route
    return pl.pallas_call(
        functools.partial(_router_kernel, n_experts=cfg.n_experts),
        grid=(cfg.batch, cfg.seq // tr),
        in_specs=[_seq_rows_spec(tr, D_MODEL),
                  pl.BlockSpec((D_MODEL, LANE), lambda bb, j: (0, 0))],
        out_specs=[pl.BlockSpec((1, tr, LANE), lambda bb, j: (bb, j, 0)),
                   pl.BlockSpec((1, LANE), lambda bb, j: (0, 0))],
        out_shape=[jax.ShapeDtypeStruct((cfg.batch, cfg.seq, LANE), F32),
                   jax.ShapeDtypeStruct((1, LANE), F32)],
        scratch_shapes=[pltpu.VMEM((1, LANE), F32)],
        compiler_params=_params(("arbitrary", "arbitrary")),
        name="router_top2",
    )(h3, wr_pad)


ZERO_ROWS = 8
ISSUE_UNROLL = 8


def _dispatch_kernel(pos_ref, fill_ref, h_ref, xs_ref, zero_ref, sem, *, t_route, n_experts):
    def zero_copy(p, n):
        start = pl.multiple_of(p * n, n)
        return pltpu.make_async_copy(zero_ref.at[pl.ds(0, n)], xs_ref.at[pl.ds(start, n)], sem.at[2])

    def zero_rows(lo, hi, n):
        def start(p, carry):
            zero_copy(p, n).start()
            return carry

        def wait(p, carry):
            zero_copy(p, n).wait()
            return carry

        lax.fori_loop(lo, hi, start, 0)
        lax.fori_loop(lo, hi, wait, 0)

    @pl.when(jnp.logical_and(pl.program_id(0) == 0, pl.program_id(1) == 0))
    def _():
        zero_ref[...] = jnp.zeros_like(zero_ref)
        for e in range(n_experts):
            zero_rows(fill_ref[0, e], fill_ref[1, e], 1)
        zero_rows(fill_ref[0, n_experts], fill_ref[1, n_experts], ZERO_ROWS)

    def issue(r, carry):
        for slot in range(TOP_K):
            pltpu.make_async_copy(h_ref.at[0, pl.ds(r, 1)],
                                  xs_ref.at[pl.ds(pos_ref[0, 0, TOP_K * r + slot], 1)],
                                  sem.at[slot]).start()
        return carry

    lax.fori_loop(0, t_route, issue, 0, unroll=ISSUE_UNROLL)
    for slot in range(TOP_K):
        pltpu.make_async_copy(h_ref.at[0], xs_ref.at[pl.ds(0, t_route)], sem.at[slot]).wait()


def _dispatch(pos, fill, h3, cfg, rows):
    tr = cfg.t_route
    steps = cfg.seq // tr
    return pl.pallas_call(
        functools.partial(_dispatch_kernel, t_route=tr, n_experts=cfg.n_experts),
        grid=(cfg.batch, steps),
        in_specs=[pl.BlockSpec((1, 1, TOP_K * tr), lambda bb, j: (bb * steps + j, 0, 0),
                               memory_space=pltpu.SMEM),
                  pl.BlockSpec(memory_space=pltpu.SMEM),
                  _seq_rows_spec(tr, D_MODEL)],
        out_specs=pl.BlockSpec(memory_space=pl.ANY),
        out_shape=jax.ShapeDtypeStruct((rows, D_MODEL), F32),
        scratch_shapes=[pltpu.VMEM((ZERO_ROWS, D_MODEL), F32), pltpu.SemaphoreType.DMA((3,))],
        compiler_params=_params(("arbitrary", "arbitrary")),
        name="moe_dispatch",
    )(pos, fill, h3)


def _grouped_kernel(te_ref, nv_ref, x_ref, wg_ref, wu_ref, wd_ref, y_ref, xb_ref, *, t_chunk):
    f = pl.program_id(1)

    @pl.when(pl.program_id(0) < nv_ref[0])
    def _():
        @pl.when(f == 0)
        def _():
            xb_ref[...] = x_ref[...].astype(BF16)

        x = xb_ref[...]
        acc = None
        for c in range(wg_ref.shape[2] // t_chunk):
            cols = pl.ds(c * t_chunk, t_chunk)
            contrib = _swiglu_contrib(x, wg_ref.at[0, :, cols], wu_ref.at[0, :, cols], wd_ref.at[0, cols, :])
            acc = contrib if acc is None else acc + contrib

        @pl.when(f == 0)
        def _():
            y_ref[...] = acc

        @pl.when(f > 0)
        def _():
            y_ref[...] += acc

    @pl.when(pl.program_id(0) >= nv_ref[0])
    def _():
        y_ref[...] = jnp.zeros_like(y_ref)


def _grouped_ffn(tile_expert, n_valid, xs, wg, wu, wd, cfg):
    tg, tf = cfg.t_group, cfg.t_ff_expert
    n_f = cfg.d_ff_expert // tf
    rows = xs.shape[0]

    def live(i, nv):
        return jnp.minimum(i, nv[0] - 1)

    def f_idx(i, f, nv):
        return jnp.where(i < nv[0], f, n_f - 1)

    grid_spec = pltpu.PrefetchScalarGridSpec(
        num_scalar_prefetch=2,
        grid=(rows // tg, n_f),
        in_specs=[pl.BlockSpec((tg, D_MODEL), lambda i, f, te, nv: (live(i, nv), 0)),
                  pl.BlockSpec((1, D_MODEL, tf), lambda i, f, te, nv: (te[i], 0, f_idx(i, f, nv))),
                  pl.BlockSpec((1, D_MODEL, tf), lambda i, f, te, nv: (te[i], 0, f_idx(i, f, nv))),
                  pl.BlockSpec((1, tf, D_MODEL), lambda i, f, te, nv: (te[i], f_idx(i, f, nv), 0))],
        out_specs=pl.BlockSpec((tg, D_MODEL), lambda i, f, te, nv: (i, 0)),
        scratch_shapes=[pltpu.VMEM((tg, D_MODEL), BF16)],
    )
    return pl.pallas_call(
        functools.partial(_grouped_kernel, t_chunk=cfg.t_ff_dense),
        grid_spec=grid_spec,
        out_shape=jax.ShapeDtypeStruct((rows, D_MODEL), F32),
        compiler_params=_params(("arbitrary", "arbitrary")),
        name="moe_grouped_ffn",
    )(tile_expert, n_valid, xs, wg, wu, wd)


def _combine_kernel(pos_ref, info_ref, h_ref, g_ref, b_ref, ys_ref, out_ref, rows_ref, sem, *, alpha, t_route):
    def issue(r, carry):
        for slot in range(TOP_K):
            pltpu.make_async_copy(ys_ref.at[pl.ds(pos_ref[0, 0, TOP_K * r + slot], 1)],
                                  rows_ref.at[slot, pl.ds(r, 1)], sem.at[slot]).start()
        return carry

    lax.fori_loop(0, t_route, issue, 0, unroll=ISSUE_UNROLL)
    for slot in range(TOP_K):
        pltpu.make_async_copy(ys_ref.at[pl.ds(0, t_route)], rows_ref.at[slot], sem.at[slot]).wait()

    info = info_ref[0]
    lane = lax.broadcasted_iota(jnp.int32, info.shape, 1)
    ffn = (_lane_pick(info, lane, INFO_W1) * rows_ref[0] + _lane_pick(info, lane, INFO_W2) * rows_ref[1])
    out_ref[0] = _layer_norm(alpha * h_ref[0] + ffn, g_ref[...], b_ref[...])


def _combine(pos, info, h3, g, b, ys, cfg, alpha):
    tr = cfg.t_route
    steps = cfg.seq // tr
    const = lambda bb, j: (0, 0)
    return pl.pallas_call(
        functools.partial(_combine_kernel, alpha=alpha, t_route=tr),
        grid=(cfg.batch, steps),
        in_specs=[pl.BlockSpec((1, 1, TOP_K * tr), lambda bb, j: (bb * steps + j, 0, 0),
                               memory_space=pltpu.SMEM),
                  pl.BlockSpec((1, tr, LANE), lambda bb, j: (bb, j, 0)),
                  _seq_rows_spec(tr, D_MODEL),
                  pl.BlockSpec((1, D_MODEL), const),
                  pl.BlockSpec((1, D_MODEL), const),
                  pl.BlockSpec(memory_space=pl.ANY)],
        out_specs=pl.BlockSpec((1, tr, D_MODEL), lambda bb, j: (bb, j, 0)),
        out_shape=jax.ShapeDtypeStruct((cfg.batch, cfg.seq, D_MODEL), F32),
        scratch_shapes=[pltpu.VMEM((TOP_K, tr, D_MODEL), F32), pltpu.SemaphoreType.DMA((TOP_K,))],
        compiler_params=_params(("arbitrary", "arbitrary")),
        name="moe_combine_ln",
    )(pos, info, h3, g, b, ys)


def _routed_moe(h3, w_router, wg, wu, wd, g, b, cfg, alpha):
    n_e, tg, tr = cfg.n_experts, cfg.t_group, cfg.t_route
    n_tok = cfg.batch * cfg.seq
    rows = TOP_K * n_tok + n_e * tg
    wr_pad = jnp.pad(w_router, ((0, 0), (0, LANE - n_e)))
    info, seen = _router(h3, wr_pad, cfg)

    flat = info.reshape(n_tok, LANE)
    as_int = lambda col: flat[:, col].astype(jnp.int32)
    counts = seen[0, :n_e].astype(jnp.int32)
    padded = (counts + tg - 1) // tg * tg
    ends = jnp.cumsum(padded)
    starts = ends - padded
    pos = jnp.stack([starts[as_int(INFO_E1)] + as_int(INFO_R1),
                     starts[as_int(INFO_E2)] + as_int(INFO_R2)], axis=-1)
    pos = pos.reshape(n_tok // tr, 1, TOP_K * tr)
    fill = jnp.stack([jnp.append(starts + counts, ends[-1] // ZERO_ROWS),
                      jnp.append(ends, rows // ZERO_ROWS)])
    n_valid = (ends[-1] // tg).reshape(1)
    tile_start = jnp.arange(rows // tg, dtype=jnp.int32) * tg
    tile_expert = jnp.sum(tile_start[:, None] >= ends[None, :], axis=-1).astype(jnp.int32)
    tile_expert = tile_expert[jnp.minimum(jnp.arange(rows // tg), n_valid[0] - 1)]

    xs = _dispatch(pos, fill, h3, cfg, rows)
    ys = _grouped_ffn(tile_expert, n_valid, xs, wg, wu, wd, cfg)
    return _combine(pos, info, h3, g, b, ys, cfg, alpha)


def _rope_tables(lp):
    half = HEAD_DIM // 2
    inv = 1.0 / (ROPE_THETA ** (jnp.arange(0, HEAD_DIM, 2, dtype=F32) / HEAD_DIM))
    pos = jnp.arange(lp, dtype=F32)
    ang = pos[:, None] * inv[None, :]
    ang = jnp.tile(ang, (1, LANE // half))
    sign = jnp.where((jnp.arange(LANE) % HEAD_DIM) < half, -1.0, 1.0).astype(F32)
    return jnp.cos(ang), jnp.sin(ang) * sign[None, :]


def _forward(cfg, x, meta_tokens, ln_emb_g, ln_emb_b, w_in, conv_w, lambda_q1, lambda_k1,
             lambda_q2, lambda_k2, subln_g, w_out, ln_mix_g, ln_mix_b, ln_ffn_g, ln_ffn_b,
             w_gate_dense, w_up_dense, w_down_dense, w_router, w_gate_moe, w_up_moe, w_down_moe):
    alpha = (2 * cfg.depth) ** 0.25
    tokens = cfg.tokens
    vec = lambda a: a.reshape(1, -1)
    cos, sin = _rope_tables(cfg.lp)
    h, hb = _embed(x, meta_tokens, vec(ln_emb_g), vec(ln_emb_b), cfg)
    h = h.reshape(tokens, D_MODEL)
    for layer in range(cfg.depth):
        lambda_init = 0.8 - 0.6 * math.exp(-0.3 * layer)
        conv, qa, qb, k, v = _inproj(hb.reshape(cfg.batch, cfg.lp, D_MODEL), w_in[layer].astype(BF16),
                                     conv_w[layer], cos, sin, cfg)
        attn = _attention(qa, qb, k, v, vec(lambda_q1[layer]), vec(lambda_k1[layer]),
                          vec(lambda_q2[layer]), vec(lambda_k2[layer]), vec(subln_g[layer]),
                          lambda_init, cfg)
        h, hb = _outproj(conv.reshape(tokens, D_CONV), attn.reshape(tokens, D_MODEL - D_CONV),
                         w_out[layer].astype(BF16), h, vec(ln_mix_g[layer]), vec(ln_mix_b[layer]),
                         cfg, alpha)
        idx = layer // 2
        g, b = vec(ln_ffn_g[layer]), vec(ln_ffn_b[layer])
        if layer % 2 == 0:
            h, hb = _ffn(hb, w_gate_dense[idx].astype(BF16), w_up_dense[idx].astype(BF16),
                         w_down_dense[idx].astype(BF16), h, g, b, cfg, alpha)
        else:
            assert layer == cfg.depth - 1
            return _routed_moe(h.reshape(cfg.batch, cfg.lp, D_MODEL), w_router[idx],
                               w_gate_moe[idx].astype(BF16), w_up_moe[idx].astype(BF16),
                               w_down_moe[idx].astype(BF16), g, b, cfg, alpha)
    return h.reshape(cfg.batch, cfg.lp, D_MODEL)[:, N_META:N_META + cfg.seq]


_CFG = Cfg(batch=8, seq=4096, depth=2, d_ff_dense=2816, d_ff_expert=3584, n_experts=8,
           t_embed=128, t_proj=1056, t_q=384, t_k=384, t_tok=1024, t_ff_dense=256, t_ff_expert=1792,
           t_route=1024, t_group=1024)


def kernel(x, meta_tokens, ln_emb_g, ln_emb_b, w_in, conv_w, lambda_q1, lambda_k1, lambda_q2, lambda_k2, subln_g, w_out, ln_mix_g, ln_mix_b, ln_ffn_g, ln_ffn_b, w_gate_dense, w_up_dense, w_down_dense, w_router, w_gate_moe, w_up_moe, w_down_moe):
    return _forward(_CFG, x, meta_tokens, ln_emb_g, ln_emb_b, w_in, conv_w, lambda_q1, lambda_k1,
                    lambda_q2, lambda_k2, subln_g, w_out, ln_mix_g, ln_mix_b, ln_ffn_g, ln_ffn_b,
                    w_gate_dense, w_up_dense, w_down_dense, w_router, w_gate_moe, w_up_moe, w_down_moe)
```

```python
import dataclasses
import functools
import math

import jax
import jax.numpy as jnp
from jax import lax
from jax.experimental import pallas as pl
from jax.experimental.pallas import tpu as pltpu

F32 = jnp.float32
BF16 = jnp.bfloat16

LANE = 128
D_MODEL = 1024
N_META = 16
PAD = LANE - N_META
D_CONV = D_MODEL // 2
CONV_WIDTH = 3
HEAD_DIM = 64
N_HEADS = 4
V_DIM = 2 * HEAD_DIM
QK_W = N_HEADS * 2 * HEAD_DIM
GROUP_W = 512
ROPE_THETA = 10000.0
TOP_K = 2
LN_EPS = 1e-5
RMS_EPS = 1e-5
NEG_BIG = -1e30
VMEM_LIMIT = 56 * 1024 * 1024


@dataclasses.dataclass(frozen=True)
class Cfg:
    batch: int
    seq: int
    depth: int
    d_ff_dense: int
    d_ff_expert: int
    n_experts: int
    t_embed: int
    t_proj: int
    t_q: int
    t_k: int
    t_tok: int
    t_ff_dense: int
    t_ff_expert: int
    t_route: int
    t_group: int

    @property
    def lp(self):
        return N_META + self.seq + PAD

    @property
    def tokens(self):
        return self.batch * self.lp


def _params(semantics):
    return pltpu.CompilerParams(dimension_semantics=semantics, vmem_limit_bytes=VMEM_LIMIT)


def _layer_norm(y, g, b):
    mu = jnp.mean(y, axis=-1, keepdims=True)
    yc = y - mu
    var = jnp.mean(yc * yc, axis=-1, keepdims=True)
    return yc * lax.rsqrt(var + LN_EPS) * g + b


def _embed_kernel(x0_ref, x1_ref, x2_ref, x3_ref, meta_ref, g_ref, b_ref, h_ref, hb_ref, *, seq):
    j = pl.program_id(1)
    te = x1_ref.shape[1]
    head = jnp.where(j == 0, meta_ref[...], x0_ref[0, te - N_META:])
    rows = jnp.concatenate([head, x1_ref[0], x2_ref[0], x3_ref[0, :te - N_META]], axis=0)
    rid = j * (3 * te) + lax.broadcasted_iota(jnp.int32, (3 * te, 1), 0)
    rows = jnp.where(rid < N_META + seq, rows, 0.0)
    y = _layer_norm(rows, g_ref[...], b_ref[...])
    h_ref[0] = y
    hb_ref[0] = y.astype(BF16)


def _embed(x, meta_tokens, g, b, cfg):
    te = cfg.t_embed
    steps = cfg.lp // (3 * te)
    last = cfg.seq // te - 1

    def x_spec(i):
        return pl.BlockSpec((1, te, D_MODEL), lambda bb, j: (bb, jnp.clip(3 * j + i - 1, 0, last), 0))

    const = lambda bb, j: (0, 0)
    out_spec = pl.BlockSpec((1, 3 * te, D_MODEL), lambda bb, j: (bb, j, 0))
    return pl.pallas_call(
        functools.partial(_embed_kernel, seq=cfg.seq),
        grid=(cfg.batch, steps),
        in_specs=[x_spec(0), x_spec(1), x_spec(2), x_spec(3),
                  pl.BlockSpec((N_META, D_MODEL), const),
                  pl.BlockSpec((1, D_MODEL), const),
                  pl.BlockSpec((1, D_MODEL), const)],
        out_specs=[out_spec, out_spec],
        out_shape=[jax.ShapeDtypeStruct((cfg.batch, cfg.lp, D_MODEL), F32),
                   jax.ShapeDtypeStruct((cfg.batch, cfg.lp, D_MODEL), BF16)],
        compiler_params=_params(("parallel", "arbitrary")),
        name="embed_ln",
    )(x, x, x, x, meta_tokens, g, b)


def _inproj_kernel(hb_ref, w_ref, cw_ref, cos_ref, sin_ref,
                   conv_ref, qa_ref, qb_ref, k_ref, v_ref, carry_ref, *, t_proj):
    j = pl.program_id(1)
    x = hb_ref[0]

    def proj(g):
        return jnp.dot(x, w_ref[:, g * GROUP_W:(g + 1) * GROUP_W], preferred_element_type=F32)

    rid = lax.broadcasted_iota(jnp.int32, (t_proj, 1), 0)
    gated = proj(1) * proj(2)

    @pl.when(j == 0)
    def _():
        carry_ref[...] = jnp.zeros_like(carry_ref)

    prev = carry_ref[...]
    back1 = jnp.where(rid == 0, prev[7:8], pltpu.roll(gated, 1, 0))
    back2 = jnp.where(rid == 0, prev[6:7], jnp.where(rid == 1, prev[7:8], pltpu.roll(gated, 2, 0)))
    carry_ref[...] = gated[t_proj - 8:]
    cw = cw_ref[...]
    conv = cw[0:1] * back2 + cw[1:2] * back1 + cw[2:3] * gated
    conv_ref[0] = (proj(0) * conv).astype(BF16)

    cos = cos_ref[...]
    sin = sin_ref[...]
    lane = lax.broadcasted_iota(jnp.int32, (1, LANE), 1)
    low_half = (lane % HEAD_DIM) < HEAD_DIM // 2
    first_map = lane < HEAD_DIM

    def rope(z):
        partner = jnp.where(low_half, pltpu.roll(z, LANE - HEAD_DIM // 2, 1),
                            pltpu.roll(z, HEAD_DIM // 2, 1))
        return z * cos + partner * sin

    zq = proj(3)
    zk = proj(4)
    scale = HEAD_DIM ** -0.5 * math.log2(math.e)
    for h in range(N_HEADS):
        sl = slice(h * LANE, (h + 1) * LANE)
        q = rope(zq[:, sl]) * scale
        qa_ref[0, :, sl] = jnp.where(first_map, q, 0.0).astype(BF16)
        qb_ref[0, :, sl] = jnp.where(first_map, 0.0, q).astype(BF16)
        k_ref[0, :, sl] = rope(zk[:, sl]).astype(BF16)
    v_ref[0] = proj(5).astype(BF16)


def _inproj(hb, w_in, conv_w, cos, sin, cfg):
    tp = cfg.t_proj
    row = lambda bb, j: (bb, j, 0)
    const = lambda bb, j: (0, 0)
    out_spec = pl.BlockSpec((1, tp, GROUP_W), row)
    out_shape = jax.ShapeDtypeStruct((cfg.batch, cfg.lp, GROUP_W), BF16)
    return pl.pallas_call(
        functools.partial(_inproj_kernel, t_proj=tp),
        grid=(cfg.batch, cfg.lp // tp),
        in_specs=[pl.BlockSpec((1, tp, D_MODEL), row),
                  pl.BlockSpec((D_MODEL, 6 * GROUP_W), const),
                  pl.BlockSpec((CONV_WIDTH, D_CONV), const),
                  pl.BlockSpec((tp, LANE), lambda bb, j: (j, 0)),
                  pl.BlockSpec((tp, LANE), lambda bb, j: (j, 0))],
        out_specs=[out_spec] * 5,
        out_shape=[out_shape] * 5,
        scratch_shapes=[pltpu.VMEM((8, D_CONV), F32)],
        compiler_params=_params(("parallel", "arbitrary")),
        name="inproj_conv_rope",
    )(hb, w_in, conv_w, cos, sin)


def _attn_kernel(lq1_ref, lk1_ref, lq2_ref, lk2_ref, g_ref, qa_ref, qb_ref, k_ref, v_ref,
                 o_ref, m_ref, l_ref, acc_ref, *, t_q, t_k, lambda_init):
    j = pl.program_id(1)
    head = lambda h: slice(h * LANE, (h + 1) * LANE)
    q2 = [jnp.concatenate([qa_ref[0, :, head(h)], qb_ref[0, :, head(h)]], axis=0) for h in range(N_HEADS)]
    m_ref[...] = jnp.full_like(m_ref, NEG_BIG)
    l_ref[...] = jnp.zeros_like(l_ref)
    acc_ref[...] = jnp.zeros_like(acc_ref)

    def step(c, width, causal):
        start = pl.multiple_of(c * t_k, t_k)
        for h in range(N_HEADS):
            k = k_ref[0, pl.ds(start, width), head(h)]
            v = v_ref[0, pl.ds(start, width), head(h)]
            s = lax.dot_general(q2[h], k, (((1,), (1,)), ((), ())), preferred_element_type=F32)
            if causal:
                qpos = lax.broadcasted_iota(jnp.int32, (2 * t_q, 1), 0) % t_q
                kpos = lax.broadcasted_iota(jnp.int32, (1, width), 1)
                s = jnp.where(kpos <= qpos, s, NEG_BIG)
            m_prev = m_ref[h]
            m_new = jnp.maximum(m_prev, jnp.max(s, axis=-1, keepdims=True))
            alpha = jnp.exp2(m_prev - m_new)
            p = jnp.exp2((s - jnp.concatenate([m_new] * (width // LANE), axis=1)).astype(BF16))
            l_ref[h] = alpha * l_ref[h] + jnp.sum(p.astype(F32), axis=-1, keepdims=True)
            acc_ref[h] = alpha * acc_ref[h] + jnp.dot(p, v, preferred_element_type=F32)
            m_ref[h] = m_new

    def pair(i, carry):
        step(2 * i, 2 * t_k, causal=False)
        return carry

    lax.fori_loop(0, j // 2, pair, 0)

    @pl.when(j % 2 == 1)
    def _():
        step(j - 1, t_k, causal=False)

    step(j, t_k, causal=True)

    lam = (jnp.exp(jnp.sum(lq1_ref[...] * lk1_ref[...], axis=-1, keepdims=True))
           - jnp.exp(jnp.sum(lq2_ref[...] * lk2_ref[...], axis=-1, keepdims=True)) + lambda_init)
    for h in range(N_HEADS):
        o = acc_ref[h] / l_ref[h]
        d = o[:t_q] - lam * o[t_q:]
        d = d * lax.rsqrt(jnp.mean(d * d, axis=-1, keepdims=True) + RMS_EPS) * g_ref[...]
        o_ref[0, :, head(h)] = (d * (1.0 - lambda_init)).astype(BF16)


def _attention(qa, qb, k, v, lq1, lk1, lq2, lk2, subln_g, lambda_init, cfg):
    tq = cfg.t_q
    width = N_HEADS * V_DIM
    small = lambda bb, j: (0, 0)
    q_spec = pl.BlockSpec((1, tq, width), lambda bb, j: (bb, j, 0))
    kv_spec = pl.BlockSpec((1, cfg.lp, width), lambda bb, j: (bb, 0, 0))
    return pl.pallas_call(
        functools.partial(_attn_kernel, t_q=tq, t_k=cfg.t_k, lambda_init=lambda_init),
        grid=(cfg.batch, cfg.lp // tq),
        in_specs=[pl.BlockSpec((1, HEAD_DIM), small)] * 4
        + [pl.BlockSpec((1, V_DIM), small), q_spec, q_spec, kv_spec, kv_spec],
        out_specs=q_spec,
        out_shape=jax.ShapeDtypeStruct((cfg.batch, cfg.lp, width), BF16),
        scratch_shapes=[pltpu.VMEM((N_HEADS, 2 * tq, LANE), F32),
                        pltpu.VMEM((N_HEADS, 2 * tq, LANE), F32),
                        pltpu.VMEM((N_HEADS, 2 * tq, V_DIM), F32)],
        compiler_params=_params(("parallel", "arbitrary")),
        name="diff_attention",
    )(lq1, lk1, lq2, lk2, subln_g, qa, qb, k, v)


def _outproj_kernel(conv_ref, attn_ref, w_ref, h_ref, g_ref, b_ref, ho_ref, hbo_ref, *, alpha):
    mix = (jnp.dot(conv_ref[...], w_ref[:D_CONV], preferred_element_type=F32)
           + jnp.dot(attn_ref[...], w_ref[D_CONV:], preferred_element_type=F32))
    y = _layer_norm(alpha * h_ref[...] + mix, g_ref[...], b_ref[...])
    ho_ref[...] = y
    hbo_ref[...] = y.astype(BF16)


def _outproj(conv, attn, w_out, h, g, b, cfg, alpha):
    tt = cfg.t_tok
    row = lambda i: (i, 0)
    const = lambda i: (0, 0)
    return pl.pallas_call(
        functools.partial(_outproj_kernel, alpha=alpha),
        grid=(cfg.tokens // tt,),
        in_specs=[pl.BlockSpec((tt, D_CONV), row),
                  pl.BlockSpec((tt, D_MODEL - D_CONV), row),
                  pl.BlockSpec((D_MODEL, D_MODEL), const),
                  pl.BlockSpec((tt, D_MODEL), row),
                  pl.BlockSpec((1, D_MODEL), const),
                  pl.BlockSpec((1, D_MODEL), const)],
        out_specs=[pl.BlockSpec((tt, D_MODEL), row)] * 2,
        out_shape=[jax.ShapeDtypeStruct((cfg.tokens, D_MODEL), F32),
                   jax.ShapeDtypeStruct((cfg.tokens, D_MODEL), BF16)],
        compiler_params=_params(("parallel",)),
        name="outproj_ln",
    )(conv, attn, w_out, h, g, b)


def _swiglu_contrib(x, wg_ref, wu_ref, wd_ref):
    gate = jnp.dot(x, wg_ref[...], preferred_element_type=F32)
    up = jnp.dot(x, wu_ref[...], preferred_element_type=F32)
    act = gate * (1.0 / (1.0 + jnp.exp(-gate))) * up
    return jnp.dot(act.astype(BF16), wd_ref[...], preferred_element_type=F32)


def _ffn_kernel(x_ref, wg_ref, wu_ref, wd_ref, h_ref, g_ref, b_ref, ho_ref, hbo_ref, *, alpha, t_ff):
    x = x_ref[...]
    acc = None
    for c in range(wg_ref.shape[1] // t_ff):
        cols = pl.ds(c * t_ff, t_ff)
        contrib = _swiglu_contrib(x, wg_ref.at[:, cols], wu_ref.at[:, cols], wd_ref.at[cols, :])
        acc = contrib if acc is None else acc + contrib
    y = _layer_norm(alpha * h_ref[...] + acc, g_ref[...], b_ref[...])
    ho_ref[...] = y
    hbo_ref[...] = y.astype(BF16)


def _ffn(hb, wg, wu, wd, h, g, b, cfg, alpha):
    tt = cfg.t_tok
    row = lambda i: (i, 0)
    const = lambda i: (0, 0)
    resident = lambda shape: pl.BlockSpec(shape, const, pipeline_mode=pl.Buffered(1))
    return pl.pallas_call(
        functools.partial(_ffn_kernel, alpha=alpha, t_ff=cfg.t_ff_dense),
        grid=(cfg.tokens // tt,),
        in_specs=[pl.BlockSpec((tt, D_MODEL), row),
                  resident((D_MODEL, cfg.d_ff_dense)),
                  resident((D_MODEL, cfg.d_ff_dense)),
                  resident((cfg.d_ff_dense, D_MODEL)),
                  pl.BlockSpec((tt, D_MODEL), row),
                  pl.BlockSpec((1, D_MODEL), const),
                  pl.BlockSpec((1, D_MODEL), const)],
        out_specs=[pl.BlockSpec((tt, D_MODEL), row)] * 2,
        out_shape=[jax.ShapeDtypeStruct((cfg.tokens, D_MODEL), F32),
                   jax.ShapeDtypeStruct((cfg.tokens, D_MODEL), BF16)],
        compiler_params=_params(("parallel",)),
        name="dense_ffn_ln",
    )(hb, wg, wu, wd, h, g, b)


INFO_E1, INFO_E2, INFO_W1, INFO_W2, INFO_R1, INFO_R2 = range(6)


def _lane_pick(values, lane, index):
    return jnp.sum(jnp.where(lane == index, values, 0.0), axis=-1, keepdims=True)


def _router_kernel(h_ref, wr_ref, info_ref, cnt_ref, seen_ref, *, n_experts):
    @pl.when(jnp.logical_and(pl.program_id(0) == 0, pl.program_id(1) == 0))
    def _():
        seen_ref[...] = jnp.zeros_like(seen_ref)

    h = h_ref[0]
    w = wr_ref[...]
    h_hi = h.astype(BF16)
    h_lo = (h - h_hi.astype(F32)).astype(BF16)
    w_hi = w.astype(BF16)
    w_lo = (w - w_hi.astype(F32)).astype(BF16)
    logits = (jnp.dot(h_hi, w_hi, preferred_element_type=F32)
              + jnp.dot(h_lo, w_hi, preferred_element_type=F32)
              + jnp.dot(h_hi, w_lo, preferred_element_type=F32))
    t_route = logits.shape[0]
    lane = lax.broadcasted_iota(jnp.int32, logits.shape, 1).astype(F32)
    logits = jnp.where(lane < n_experts, logits, -jnp.inf)
    v1 = jnp.max(logits, axis=-1, keepdims=True)
    e1 = jnp.min(jnp.where(logits == v1, lane, float(LANE)), axis=-1, keepdims=True)
    rest = jnp.where(lane == e1, -jnp.inf, logits)
    v2 = jnp.max(rest, axis=-1, keepdims=True)
    e2 = jnp.min(jnp.where(rest == v2, lane, float(LANE)), axis=-1, keepdims=True)
    ex = jnp.exp(v2 - v1)
    w1 = 1.0 / (1.0 + ex)
    w2 = ex / (1.0 + ex)

    chosen = jnp.where(lane == e1, 1.0, 0.0) + jnp.where(lane == e2, 1.0, 0.0)
    r = lax.broadcasted_iota(jnp.int32, (t_route, t_route), 0)
    c = lax.broadcasted_iota(jnp.int32, (t_route, t_route), 1)
    earlier = jnp.where(c < r, 1.0, 0.0).astype(BF16)
    before = jnp.dot(earlier, chosen.astype(BF16), preferred_element_type=F32) + seen_ref[...]
    r1 = _lane_pick(before, lane, e1)
    r2 = _lane_pick(before, lane, e2)
    seen_ref[...] += jnp.sum(chosen, axis=0, keepdims=True)
    cnt_ref[...] = seen_ref[...]

    record = jnp.zeros_like(logits)
    for slot, val in ((INFO_E1, e1), (INFO_E2, e2), (INFO_W1, w1), (INFO_W2, w2),
                      (INFO_R1, r1), (INFO_R2, r2)):
        record = jnp.where(lane == slot, val, record)
    info_ref[0] = record


def _seq_rows_spec(t_rows, width):
    return pl.BlockSpec((pl.Element(1), pl.Element(t_rows), pl.Element(width)),
                        lambda bb, j: (bb, pl.multiple_of(N_META + j * t_rows, N_META), 0))


def _router(h3, wr_pad, cfg):
    tr = cfg.t_route
    return pl.pallas_call(
        functools.partial(_router_kernel, n_experts=cfg.n_experts),
        grid=(cfg.batch, cfg.seq // tr),
        in_specs=[_seq_rows_spec(tr, D_MODEL),
                  pl.BlockSpec((D_MODEL, LANE), lambda bb, j: (0, 0))],
        out_specs=[pl.BlockSpec((1, tr, LANE), lambda bb, j: (bb, j, 0)),
                   pl.BlockSpec((1, LANE), lambda bb, j: (0, 0))],
        out_shape=[jax.ShapeDtypeStruct((cfg.batch, cfg.seq, LANE), F32),
                   jax.ShapeDtypeStruct((1, LANE), F32)],
        scratch_shapes=[pltpu.VMEM((1, LANE), F32)],
        compiler_params=_params(("arbitrary", "arbitrary")),
        name="router_top2",
    )(h3, wr_pad)


ZERO_ROWS = 8
ISSUE_UNROLL = 8


def _dispatch_kernel(pos_ref, fill_ref, h_ref, xs_ref, zero_ref, sem, *, t_route, n_experts):
    def zero_copy(p, n):
        start = pl.multiple_of(p * n, n)
        return pltpu.make_async_copy(zero_ref.at[pl.ds(0, n)], xs_ref.at[pl.ds(start, n)], sem.at[2])

    def zero_rows(lo, hi, n):
        def start(p, carry):
            zero_copy(p, n).start()
            return carry

        def wait(p, carry):
            zero_copy(p, n).wait()
            return carry

        lax.fori_loop(lo, hi, start, 0)
        lax.fori_loop(lo, hi, wait, 0)

    @pl.when(jnp.logical_and(pl.program_id(0) == 0, pl.program_id(1) == 0))
    def _():
        zero_ref[...] = jnp.zeros_like(zero_ref)
        for e in range(n_experts):
            zero_rows(fill_ref[0, e], fill_ref[1, e], 1)
        zero_rows(fill_ref[0, n_experts], fill_ref[1, n_experts], ZERO_ROWS)

    bb = pl.program_id(0)
    first_row = N_META + pl.program_id(1) * t_route

    def issue(r, carry):
        for slot in range(TOP_K):
            pltpu.make_async_copy(h_ref.at[bb, pl.ds(first_row + r, 1)],
                                  xs_ref.at[pl.ds(pos_ref[0, 0, TOP_K * r + slot], 1)],
                                  sem.at[slot]).start()
        return carry

    def wait_step():
        for slot in range(TOP_K):
            pltpu.make_async_copy(h_ref.at[0, pl.ds(0, t_route)], xs_ref.at[pl.ds(0, t_route)],
                                  sem.at[slot]).wait()

    step = bb * pl.num_programs(1) + pl.program_id(1)
    lax.fori_loop(0, t_route, issue, 0, unroll=ISSUE_UNROLL)

    @pl.when(step > 0)
    def _():
        wait_step()

    @pl.when(step == pl.num_programs(0) * pl.num_programs(1) - 1)
    def _():
        wait_step()


def _dispatch(pos, fill, h3, cfg, rows):
    tr = cfg.t_route
    steps = cfg.seq // tr
    return pl.pallas_call(
        functools.partial(_dispatch_kernel, t_route=tr, n_experts=cfg.n_experts),
        grid=(cfg.batch, steps),
        in_specs=[pl.BlockSpec((1, 1, TOP_K * tr), lambda bb, j: (bb * steps + j, 0, 0),
                               memory_space=pltpu.SMEM),
                  pl.BlockSpec(memory_space=pltpu.SMEM),
                  pl.BlockSpec(memory_space=pl.ANY)],
        out_specs=pl.BlockSpec(memory_space=pl.ANY),
        out_shape=jax.ShapeDtypeStruct((rows, D_MODEL), F32),
        scratch_shapes=[pltpu.VMEM((ZERO_ROWS, D_MODEL), F32), pltpu.SemaphoreType.DMA((3,))],
        compiler_params=_params(("arbitrary", "arbitrary")),
        name="moe_dispatch",
    )(pos, fill, h3)


def _grouped_kernel(te_ref, nv_ref, x_ref, wg_ref, wu_ref, wd_ref, y_ref, xb_ref, *, t_chunk):
    f = pl.program_id(1)

    @pl.when(pl.program_id(0) < nv_ref[0])
    def _():
        @pl.when(f == 0)
        def _():
            xb_ref[...] = x_ref[...].astype(BF16)

        x = xb_ref[...]
        acc = None
        for c in range(wg_ref.shape[2] // t_chunk):
            cols = pl.ds(c * t_chunk, t_chunk)
            contrib = _swiglu_contrib(x, wg_ref.at[0, :, cols], wu_ref.at[0, :, cols], wd_ref.at[0, cols, :])
            acc = contrib if acc is None else acc + contrib

        @pl.when(f == 0)
        def _():
            y_ref[...] = acc

        @pl.when(f > 0)
        def _():
            y_ref[...] += acc

    @pl.when(pl.program_id(0) >= nv_ref[0])
    def _():
        y_ref[...] = jnp.zeros_like(y_ref)


def _grouped_ffn(tile_expert, n_valid, xs, wg, wu, wd, cfg):
    tg, tf = cfg.t_group, cfg.t_ff_expert
    n_f = cfg.d_ff_expert // tf
    rows = xs.shape[0]

    def live(i, nv):
        return jnp.minimum(i, nv[0] - 1)

    def f_idx(i, f, nv):
        return jnp.where(i < nv[0], f, n_f - 1)

    grid_spec = pltpu.PrefetchScalarGridSpec(
        num_scalar_prefetch=2,
        grid=(rows // tg, n_f),
        in_specs=[pl.BlockSpec((tg, D_MODEL), lambda i, f, te, nv: (live(i, nv), 0)),
                  pl.BlockSpec((1, D_MODEL, tf), lambda i, f, te, nv: (te[i], 0, f_idx(i, f, nv))),
                  pl.BlockSpec((1, D_MODEL, tf), lambda i, f, te, nv: (te[i], 0, f_idx(i, f, nv))),
                  pl.BlockSpec((1, tf, D_MODEL), lambda i, f, te, nv: (te[i], f_idx(i, f, nv), 0))],
        out_specs=pl.BlockSpec((tg, D_MODEL), lambda i, f, te, nv: (i, 0)),
        scratch_shapes=[pltpu.VMEM((tg, D_MODEL), BF16)],
    )
    return pl.pallas_call(
        functools.partial(_grouped_kernel, t_chunk=cfg.t_ff_dense),
        grid_spec=grid_spec,
        out_shape=jax.ShapeDtypeStruct((rows, D_MODEL), F32),
        compiler_params=_params(("arbitrary", "arbitrary")),
        name="moe_grouped_ffn",
    )(tile_expert, n_valid, xs, wg, wu, wd)


def _combine_kernel(pos_ref, next_pos_ref, info_ref, h_ref, g_ref, b_ref, ys_ref, out_ref, rows_ref, sem,
                    *, alpha, t_route):
    step = pl.program_id(0) * pl.num_programs(1) + pl.program_id(1)
    last = pl.num_programs(0) * pl.num_programs(1) - 1
    buf = step % 2

    def gather(tile_pos_ref, into):
        def issue(r, carry):
            for slot in range(TOP_K):
                pltpu.make_async_copy(ys_ref.at[pl.ds(tile_pos_ref[0, 0, TOP_K * r + slot], 1)],
                                      rows_ref.at[into, slot, pl.ds(r, 1)], sem.at[into, slot]).start()
            return carry

        lax.fori_loop(0, t_route, issue, 0, unroll=ISSUE_UNROLL)

    @pl.when(step == 0)
    def _():
        gather(pos_ref, 0)

    @pl.when(step < last)
    def _():
        gather(next_pos_ref, 1 - buf)

    for slot in range(TOP_K):
        pltpu.make_async_copy(ys_ref.at[pl.ds(0, t_route)], rows_ref.at[buf, slot], sem.at[buf, slot]).wait()

    info = info_ref[0]
    lane = lax.broadcasted_iota(jnp.int32, info.shape, 1)
    ffn = (_lane_pick(info, lane, INFO_W1) * rows_ref[buf, 0]
           + _lane_pick(info, lane, INFO_W2) * rows_ref[buf, 1])
    out_ref[0] = _layer_norm(alpha * h_ref[0] + ffn, g_ref[...], b_ref[...])


def _combine(pos, info, h3, g, b, ys, cfg, alpha):
    tr = cfg.t_route
    steps = cfg.seq // tr
    n_tiles = cfg.batch * steps
    const = lambda bb, j: (0, 0)

    def pos_spec(ahead):
        return pl.BlockSpec((1, 1, TOP_K * tr),
                            lambda bb, j: (jnp.minimum(bb * steps + j + ahead, n_tiles - 1), 0, 0),
                            memory_space=pltpu.SMEM)

    return pl.pallas_call(
        functools.partial(_combine_kernel, alpha=alpha, t_route=tr),
        grid=(cfg.batch, steps),
        in_specs=[pos_spec(0), pos_spec(1),
                  pl.BlockSpec((1, tr, LANE), lambda bb, j: (bb, j, 0)),
                  _seq_rows_spec(tr, D_MODEL),
                  pl.BlockSpec((1, D_MODEL), const),
                  pl.BlockSpec((1, D_MODEL), const),
                  pl.BlockSpec(memory_space=pl.ANY)],
        out_specs=pl.BlockSpec((1, tr, D_MODEL), lambda bb, j: (bb, j, 0)),
        out_shape=jax.ShapeDtypeStruct((cfg.batch, cfg.seq, D_MODEL), F32),
        scratch_shapes=[pltpu.VMEM((2, TOP_K, tr, D_MODEL), F32), pltpu.SemaphoreType.DMA((2, TOP_K))],
        compiler_params=_params(("arbitrary", "arbitrary")),
        name="moe_combine_ln",
    )(pos, pos, info, h3, g, b, ys)


def _routed_moe(h3, w_router, wg, wu, wd, g, b, cfg, alpha):
    n_e, tg, tr = cfg.n_experts, cfg.t_group, cfg.t_route
    n_tok = cfg.batch * cfg.seq
    rows = TOP_K * n_tok + n_e * tg
    wr_pad = jnp.pad(w_router, ((0, 0), (0, LANE - n_e)))
    info, seen = _router(h3, wr_pad, cfg)

    flat = info.reshape(n_tok, LANE)
    as_int = lambda col: flat[:, col].astype(jnp.int32)
    counts = seen[0, :n_e].astype(jnp.int32)
    padded = (counts + tg - 1) // tg * tg
    ends = jnp.cumsum(padded)
    starts = ends - padded
    pos = jnp.stack([starts[as_int(INFO_E1)] + as_int(INFO_R1),
                     starts[as_int(INFO_E2)] + as_int(INFO_R2)], axis=-1)
    pos = pos.reshape(n_tok // tr, 1, TOP_K * tr)
    fill = jnp.stack([jnp.append(starts + counts, ends[-1] // ZERO_ROWS),
                      jnp.append(ends, rows // ZERO_ROWS)])
    n_valid = (ends[-1] // tg).reshape(1)
    tile_start = jnp.arange(rows // tg, dtype=jnp.int32) * tg
    tile_expert = jnp.sum(tile_start[:, None] >= ends[None, :], axis=-1).astype(jnp.int32)
    tile_expert = tile_expert[jnp.minimum(jnp.arange(rows // tg), n_valid[0] - 1)]

    xs = _dispatch(pos, fill, h3, cfg, rows)
    ys = _grouped_ffn(tile_expert, n_valid, xs, wg, wu, wd, cfg)
    return _combine(pos, info, h3, g, b, ys, cfg, alpha)


def _rope_tables(lp):
    half = HEAD_DIM // 2
    inv = 1.0 / (ROPE_THETA ** (jnp.arange(0, HEAD_DIM, 2, dtype=F32) / HEAD_DIM))
    pos = jnp.arange(lp, dtype=F32)
    ang = pos[:, None] * inv[None, :]
    ang = jnp.tile(ang, (1, LANE // half))
    sign = jnp.where((jnp.arange(LANE) % HEAD_DIM) < half, -1.0, 1.0).astype(F32)
    return jnp.cos(ang), jnp.sin(ang) * sign[None, :]


def _forward(cfg, x, meta_tokens, ln_emb_g, ln_emb_b, w_in, conv_w, lambda_q1, lambda_k1,
             lambda_q2, lambda_k2, subln_g, w_out, ln_mix_g, ln_mix_b, ln_ffn_g, ln_ffn_b,
             w_gate_dense, w_up_dense, w_down_dense, w_router, w_gate_moe, w_up_moe, w_down_moe):
    alpha = (2 * cfg.depth) ** 0.25
    tokens = cfg.tokens
    vec = lambda a: a.reshape(1, -1)
    cos, sin = _rope_tables(cfg.lp)
    h, hb = _embed(x, meta_tokens, vec(ln_emb_g), vec(ln_emb_b), cfg)
    h = h.reshape(tokens, D_MODEL)
    for layer in range(cfg.depth):
        lambda_init = 0.8 - 0.6 * math.exp(-0.3 * layer)
        conv, qa, qb, k, v = _inproj(hb.reshape(cfg.batch, cfg.lp, D_MODEL), w_in[layer].astype(BF16),
                                     conv_w[layer], cos, sin, cfg)
        attn = _attention(qa, qb, k, v, vec(lambda_q1[layer]), vec(lambda_k1[layer]),
                          vec(lambda_q2[layer]), vec(lambda_k2[layer]), vec(subln_g[layer]),
                          lambda_init, cfg)
        h, hb = _outproj(conv.reshape(tokens, D_CONV), attn.reshape(tokens, D_MODEL - D_CONV),
                         w_out[layer].astype(BF16), h, vec(ln_mix_g[layer]), vec(ln_mix_b[layer]),
                         cfg, alpha)
        idx = layer // 2
        g, b = vec(ln_ffn_g[layer]), vec(ln_ffn_b[layer])
        if layer % 2 == 0:
            h, hb = _ffn(hb, w_gate_dense[idx].astype(BF16), w_up_dense[idx].astype(BF16),
                         w_down_dense[idx].astype(BF16), h, g, b, cfg, alpha)
        else:
            assert layer == cfg.depth - 1
            return _routed_moe(h.reshape(cfg.batch, cfg.lp, D_MODEL), w_router[idx],
                               w_gate_moe[idx].astype(BF16), w_up_moe[idx].astype(BF16),
                               w_down_moe[idx].astype(BF16), g, b, cfg, alpha)
    return h.reshape(cfg.batch, cfg.lp, D_MODEL)[:, N_META:N_META + cfg.seq]


_CFG = Cfg(batch=8, seq=4096, depth=2, d_ff_dense=2816, d_ff_expert=3584, n_experts=8,
           t_embed=128, t_proj=1056, t_q=384, t_k=384, t_tok=1024, t_ff_dense=256, t_ff_expert=1792,
           t_route=1024, t_group=1024)


def kernel(x, meta_tokens, ln_emb_g, ln_emb_b, w_in, conv_w, lambda_q1, lambda_k1, lambda_q2, lambda_k2, subln_g, w_out, ln_mix_g, ln_mix_b, ln_ffn_g, ln_ffn_b, w_gate_dense, w_up_dense, w_down_dense, w_router, w_gate_moe, w_up_moe, w_down_moe):
    return _forward(_CFG, x, meta_tokens, ln_emb_g, ln_emb_b, w_in, conv_w, lambda_q1, lambda_k1,
                    lambda_q2, lambda_k2, subln_g, w_out, ln_mix_g, ln_mix_b, ln_ffn_g, ln_ffn_b,
                    w_gate_dense, w_up_dense, w_down_dense, w_router, w_gate_moe, w_up_moe, w_down_moe)
```

```python
import dataclasses
import functools
import math

import jax
import jax.numpy as jnp
from jax import lax
from jax.experimental import pallas as pl
from jax.experimental.pallas import tpu as pltpu

F32 = jnp.float32
BF16 = jnp.bfloat16

LANE = 128
D_MODEL = 1024
N_META = 16
PAD = LANE - N_META
D_CONV = D_MODEL // 2
CONV_WIDTH = 3
HEAD_DIM = 64
N_HEADS = 4
V_DIM = 2 * HEAD_DIM
QK_W = N_HEADS * 2 * HEAD_DIM
GROUP_W = 512
ROPE_THETA = 10000.0
TOP_K = 2
LN_EPS = 1e-5
RMS_EPS = 1e-5
NEG_BIG = -1e30
VMEM_LIMIT = 56 * 1024 * 1024


@dataclasses.dataclass(frozen=True)
class Cfg:
    batch: int
    seq: int
    depth: int
    d_ff_dense: int
    d_ff_expert: int
    n_experts: int
    t_embed: int
    t_proj: int
    t_q: int
    t_k: int
    t_tok: int
    t_ff_dense: int
    t_ff_expert: int
    t_route: int
    t_group: int

    @property
    def lp(self):
        return N_META + self.seq + PAD

    @property
    def tokens(self):
        return self.batch * self.lp


def _params(semantics):
    return pltpu.CompilerParams(dimension_semantics=semantics, vmem_limit_bytes=VMEM_LIMIT)


def _layer_norm(y, g, b):
    mu = jnp.mean(y, axis=-1, keepdims=True)
    yc = y - mu
    var = jnp.mean(yc * yc, axis=-1, keepdims=True)
    return yc * lax.rsqrt(var + LN_EPS) * g + b


def _embed_kernel(x0_ref, x1_ref, x2_ref, x3_ref, meta_ref, g_ref, b_ref, h_ref, hb_ref, *, seq):
    j = pl.program_id(1)
    te = x1_ref.shape[1]
    head = jnp.where(j == 0, meta_ref[...], x0_ref[0, te - N_META:])
    rows = jnp.concatenate([head, x1_ref[0], x2_ref[0], x3_ref[0, :te - N_META]], axis=0)
    rid = j * (3 * te) + lax.broadcasted_iota(jnp.int32, (3 * te, 1), 0)
    rows = jnp.where(rid < N_META + seq, rows, 0.0)
    y = _layer_norm(rows, g_ref[...], b_ref[...])
    h_ref[0] = y
    hb_ref[0] = y.astype(BF16)


def _embed(x, meta_tokens, g, b, cfg):
    te = cfg.t_embed
    steps = cfg.lp // (3 * te)
    last = cfg.seq // te - 1

    def x_spec(i):
        return pl.BlockSpec((1, te, D_MODEL), lambda bb, j: (bb, jnp.clip(3 * j + i - 1, 0, last), 0))

    const = lambda bb, j: (0, 0)
    out_spec = pl.BlockSpec((1, 3 * te, D_MODEL), lambda bb, j: (bb, j, 0))
    return pl.pallas_call(
        functools.partial(_embed_kernel, seq=cfg.seq),
        grid=(cfg.batch, steps),
        in_specs=[x_spec(0), x_spec(1), x_spec(2), x_spec(3),
                  pl.BlockSpec((N_META, D_MODEL), const),
                  pl.BlockSpec((1, D_MODEL), const),
                  pl.BlockSpec((1, D_MODEL), const)],
        out_specs=[out_spec, out_spec],
        out_shape=[jax.ShapeDtypeStruct((cfg.batch, cfg.lp, D_MODEL), F32),
                   jax.ShapeDtypeStruct((cfg.batch, cfg.lp, D_MODEL), BF16)],
        compiler_params=_params(("parallel", "arbitrary")),
        name="embed_ln",
    )(x, x, x, x, meta_tokens, g, b)


def _inproj_kernel(hb_ref, w_ref, cw_ref, cos_ref, sin_ref,
                   conv_ref, qa_ref, qb_ref, k_ref, v_ref, carry_ref, *, t_proj):
    j = pl.program_id(1)
    x = hb_ref[0]

    def proj(g):
        return jnp.dot(x, w_ref[:, g * GROUP_W:(g + 1) * GROUP_W], preferred_element_type=F32)

    rid = lax.broadcasted_iota(jnp.int32, (t_proj, 1), 0)
    gated = proj(1) * proj(2)

    @pl.when(j == 0)
    def _():
        carry_ref[...] = jnp.zeros_like(carry_ref)

    prev = carry_ref[...]
    back1 = jnp.where(rid == 0, prev[7:8], pltpu.roll(gated, 1, 0))
    back2 = jnp.where(rid == 0, prev[6:7], jnp.where(rid == 1, prev[7:8], pltpu.roll(gated, 2, 0)))
    carry_ref[...] = gated[t_proj - 8:]
    cw = cw_ref[...]
    conv = cw[0:1] * back2 + cw[1:2] * back1 + cw[2:3] * gated
    conv_ref[0] = (proj(0) * conv).astype(BF16)

    cos = cos_ref[...]
    sin = sin_ref[...]
    lane = lax.broadcasted_iota(jnp.int32, (1, LANE), 1)
    low_half = (lane % HEAD_DIM) < HEAD_DIM // 2
    first_map = lane < HEAD_DIM

    def rope(z):
        partner = jnp.where(low_half, pltpu.roll(z, LANE - HEAD_DIM // 2, 1),
                            pltpu.roll(z, HEAD_DIM // 2, 1))
        return z * cos + partner * sin

    zq = proj(3)
    zk = proj(4)
    scale = HEAD_DIM ** -0.5 * math.log2(math.e)
    for h in range(N_HEADS):
        sl = slice(h * LANE, (h + 1) * LANE)
        q = rope(zq[:, sl]) * scale
        qa_ref[0, :, sl] = jnp.where(first_map, q, 0.0).astype(BF16)
        qb_ref[0, :, sl] = jnp.where(first_map, 0.0, q).astype(BF16)
        k_ref[0, :, sl] = rope(zk[:, sl]).astype(BF16)
    v_ref[0] = proj(5).astype(BF16)


def _inproj(hb, w_in, conv_w, cos, sin, cfg):
    tp = cfg.t_proj
    row = lambda bb, j: (bb, j, 0)
    const = lambda bb, j: (0, 0)
    out_spec = pl.BlockSpec((1, tp, GROUP_W), row)
    out_shape = jax.ShapeDtypeStruct((cfg.batch, cfg.lp, GROUP_W), BF16)
    return pl.pallas_call(
        functools.partial(_inproj_kernel, t_proj=tp),
        grid=(cfg.batch, cfg.lp // tp),
        in_specs=[pl.BlockSpec((1, tp, D_MODEL), row),
                  pl.BlockSpec((D_MODEL, 6 * GROUP_W), const),
                  pl.BlockSpec((CONV_WIDTH, D_CONV), const),
                  pl.BlockSpec((tp, LANE), lambda bb, j: (j, 0)),
                  pl.BlockSpec((tp, LANE), lambda bb, j: (j, 0))],
        out_specs=[out_spec] * 5,
        out_shape=[out_shape] * 5,
        scratch_shapes=[pltpu.VMEM((8, D_CONV), F32)],
        compiler_params=_params(("parallel", "arbitrary")),
        name="inproj_conv_rope",
    )(hb, w_in, conv_w, cos, sin)


def _attn_kernel(lq1_ref, lk1_ref, lq2_ref, lk2_ref, g_ref, qa_ref, qb_ref, k_ref, v_ref,
                 o_ref, m_ref, l_ref, acc_ref, *, t_q, t_k, lambda_init):
    j = pl.program_id(1)
    head = lambda h: slice(h * LANE, (h + 1) * LANE)
    q2 = [jnp.concatenate([qa_ref[0, :, head(h)], qb_ref[0, :, head(h)]], axis=0) for h in range(N_HEADS)]
    m_ref[...] = jnp.full_like(m_ref, NEG_BIG)
    l_ref[...] = jnp.zeros_like(l_ref)
    acc_ref[...] = jnp.zeros_like(acc_ref)

    def step(c, width, causal):
        start = pl.multiple_of(c * t_k, t_k)
        for h in range(N_HEADS):
            k = k_ref[0, pl.ds(start, width), head(h)]
            v = v_ref[0, pl.ds(start, width), head(h)]
            s = lax.dot_general(q2[h], k, (((1,), (1,)), ((), ())), preferred_element_type=F32)
            if causal:
                qpos = lax.broadcasted_iota(jnp.int32, (2 * t_q, 1), 0) % t_q
                kpos = lax.broadcasted_iota(jnp.int32, (1, width), 1)
                s = jnp.where(kpos <= qpos, s, NEG_BIG)
            m_prev = m_ref[h]
            m_new = jnp.maximum(m_prev, jnp.max(s, axis=-1, keepdims=True))
            alpha = jnp.exp2(m_prev - m_new)
            p = jnp.exp2((s - jnp.concatenate([m_new] * (width // LANE), axis=1)).astype(BF16))
            l_ref[h] = alpha * l_ref[h] + jnp.sum(p.astype(F32), axis=-1, keepdims=True)
            acc_ref[h] = alpha * acc_ref[h] + jnp.dot(p, v, preferred_element_type=F32)
            m_ref[h] = m_new

    def pair(i, carry):
        step(2 * i, 2 * t_k, causal=False)
        return carry

    lax.fori_loop(0, j // 2, pair, 0)

    @pl.when(j % 2 == 1)
    def _():
        step(j - 1, t_k, causal=False)

    step(j, t_k, causal=True)

    lam = (jnp.exp(jnp.sum(lq1_ref[...] * lk1_ref[...], axis=-1, keepdims=True))
           - jnp.exp(jnp.sum(lq2_ref[...] * lk2_ref[...], axis=-1, keepdims=True)) + lambda_init)
    for h in range(N_HEADS):
        o = acc_ref[h] / l_ref[h]
        d = o[:t_q] - lam * o[t_q:]
        d = d * lax.rsqrt(jnp.mean(d * d, axis=-1, keepdims=True) + RMS_EPS) * g_ref[...]
        o_ref[0, :, head(h)] = (d * (1.0 - lambda_init)).astype(BF16)


def _attention(qa, qb, k, v, lq1, lk1, lq2, lk2, subln_g, lambda_init, cfg):
    tq = cfg.t_q
    width = N_HEADS * V_DIM
    small = lambda bb, j: (0, 0)
    q_spec = pl.BlockSpec((1, tq, width), lambda bb, j: (bb, j, 0))
    kv_spec = pl.BlockSpec((1, cfg.lp, width), lambda bb, j: (bb, 0, 0))
    return pl.pallas_call(
        functools.partial(_attn_kernel, t_q=tq, t_k=cfg.t_k, lambda_init=lambda_init),
        grid=(cfg.batch, cfg.lp // tq),
        in_specs=[pl.BlockSpec((1, HEAD_DIM), small)] * 4
        + [pl.BlockSpec((1, V_DIM), small), q_spec, q_spec, kv_spec, kv_spec],
        out_specs=q_spec,
        out_shape=jax.ShapeDtypeStruct((cfg.batch, cfg.lp, width), BF16),
        scratch_shapes=[pltpu.VMEM((N_HEADS, 2 * tq, LANE), F32),
                        pltpu.VMEM((N_HEADS, 2 * tq, LANE), F32),
                        pltpu.VMEM((N_HEADS, 2 * tq, V_DIM), F32)],
        compiler_params=_params(("parallel", "arbitrary")),
        name="diff_attention",
    )(lq1, lk1, lq2, lk2, subln_g, qa, qb, k, v)


def _outproj_kernel(conv_ref, attn_ref, w_ref, h_ref, g_ref, b_ref, ho_ref, *maybe_hbo_ref, alpha):
    mix = (jnp.dot(conv_ref[...], w_ref[:D_CONV], preferred_element_type=F32)
           + jnp.dot(attn_ref[...], w_ref[D_CONV:], preferred_element_type=F32))
    y = _layer_norm(alpha * h_ref[...] + mix, g_ref[...], b_ref[...])
    ho_ref[...] = y
    for hbo_ref in maybe_hbo_ref:
        hbo_ref[...] = y.astype(BF16)


def _outproj(conv, attn, w_out, h, g, b, cfg, alpha, with_bf16):
    tt = cfg.t_tok
    n_out = 2 if with_bf16 else 1
    row = lambda i: (i, 0)
    const = lambda i: (0, 0)
    return pl.pallas_call(
        functools.partial(_outproj_kernel, alpha=alpha),
        grid=(cfg.tokens // tt,),
        in_specs=[pl.BlockSpec((tt, D_CONV), row),
                  pl.BlockSpec((tt, D_MODEL - D_CONV), row),
                  pl.BlockSpec((D_MODEL, D_MODEL), const),
                  pl.BlockSpec((tt, D_MODEL), row),
                  pl.BlockSpec((1, D_MODEL), const),
                  pl.BlockSpec((1, D_MODEL), const)],
        out_specs=[pl.BlockSpec((tt, D_MODEL), row)] * n_out,
        out_shape=[jax.ShapeDtypeStruct((cfg.tokens, D_MODEL), F32),
                   jax.ShapeDtypeStruct((cfg.tokens, D_MODEL), BF16)][:n_out],
        compiler_params=_params(("parallel",)),
        name="outproj_ln",
    )(conv, attn, w_out, h, g, b)


def _swiglu_contrib(x, wg_ref, wu_ref, wd_ref):
    gate = jnp.dot(x, wg_ref[...], preferred_element_type=F32)
    up = jnp.dot(x, wu_ref[...], preferred_element_type=F32)
    act = gate * (1.0 / (1.0 + jnp.exp(-gate))) * up
    return jnp.dot(act.astype(BF16), wd_ref[...], preferred_element_type=F32)


def _ffn_kernel(x_ref, wg_ref, wu_ref, wd_ref, h_ref, g_ref, b_ref, ho_ref, hbo_ref, *, alpha, t_ff):
    x = x_ref[...]
    acc = None
    for c in range(wg_ref.shape[1] // t_ff):
        cols = pl.ds(c * t_ff, t_ff)
        contrib = _swiglu_contrib(x, wg_ref.at[:, cols], wu_ref.at[:, cols], wd_ref.at[cols, :])
        acc = contrib if acc is None else acc + contrib
    y = _layer_norm(alpha * h_ref[...] + acc, g_ref[...], b_ref[...])
    ho_ref[...] = y
    hbo_ref[...] = y.astype(BF16)


def _ffn(hb, wg, wu, wd, h, g, b, cfg, alpha):
    tt = cfg.t_tok
    row = lambda i: (i, 0)
    const = lambda i: (0, 0)
    resident = lambda shape: pl.BlockSpec(shape, const, pipeline_mode=pl.Buffered(1))
    return pl.pallas_call(
        functools.partial(_ffn_kernel, alpha=alpha, t_ff=cfg.t_ff_dense),
        grid=(cfg.tokens // tt,),
        in_specs=[pl.BlockSpec((tt, D_MODEL), row),
                  resident((D_MODEL, cfg.d_ff_dense)),
                  resident((D_MODEL, cfg.d_ff_dense)),
                  resident((cfg.d_ff_dense, D_MODEL)),
                  pl.BlockSpec((tt, D_MODEL), row),
                  pl.BlockSpec((1, D_MODEL), const),
                  pl.BlockSpec((1, D_MODEL), const)],
        out_specs=[pl.BlockSpec((tt, D_MODEL), row)] * 2,
        out_shape=[jax.ShapeDtypeStruct((cfg.tokens, D_MODEL), F32),
                   jax.ShapeDtypeStruct((cfg.tokens, D_MODEL), BF16)],
        compiler_params=_params(("parallel",)),
        name="dense_ffn_ln",
    )(hb, wg, wu, wd, h, g, b)


INFO_E1, INFO_E2, INFO_W1, INFO_W2, INFO_R1, INFO_R2 = range(6)
ROUTE_ROWS = 8


def _lane_pick(values, lane, index):
    return jnp.sum(jnp.where(lane == index, values, 0.0), axis=-1, keepdims=True)


def _router_kernel(h_ref, wr_ref, info_ref, route_ref, cnt_ref, seen_ref, *, n_experts):
    @pl.when(jnp.logical_and(pl.program_id(0) == 0, pl.program_id(1) == 0))
    def _():
        seen_ref[...] = jnp.zeros_like(seen_ref)

    h = h_ref[0]
    w = wr_ref[...]
    h_hi = h.astype(BF16)
    h_lo = (h - h_hi.astype(F32)).astype(BF16)
    w_hi = w.astype(BF16)
    w_lo = (w - w_hi.astype(F32)).astype(BF16)
    logits = (jnp.dot(h_hi, w_hi, preferred_element_type=F32)
              + jnp.dot(h_lo, w_hi, preferred_element_type=F32)
              + jnp.dot(h_hi, w_lo, preferred_element_type=F32))
    t_route = logits.shape[0]
    lane = lax.broadcasted_iota(jnp.int32, logits.shape, 1).astype(F32)
    logits = jnp.where(lane < n_experts, logits, -jnp.inf)
    v1 = jnp.max(logits, axis=-1, keepdims=True)
    e1 = jnp.min(jnp.where(logits == v1, lane, float(LANE)), axis=-1, keepdims=True)
    rest = jnp.where(lane == e1, -jnp.inf, logits)
    v2 = jnp.max(rest, axis=-1, keepdims=True)
    e2 = jnp.min(jnp.where(rest == v2, lane, float(LANE)), axis=-1, keepdims=True)
    ex = jnp.exp(v2 - v1)
    w1 = 1.0 / (1.0 + ex)
    w2 = ex / (1.0 + ex)

    chosen = jnp.where(lane == e1, 1.0, 0.0) + jnp.where(lane == e2, 1.0, 0.0)
    r = lax.broadcasted_iota(jnp.int32, (t_route, t_route), 0)
    c = lax.broadcasted_iota(jnp.int32, (t_route, t_route), 1)
    earlier = jnp.where(c < r, 1.0, 0.0).astype(BF16)
    before = jnp.dot(earlier, chosen.astype(BF16), preferred_element_type=F32) + seen_ref[...]
    r1 = _lane_pick(before, lane, e1)
    r2 = _lane_pick(before, lane, e2)
    seen_ref[...] += jnp.sum(chosen, axis=0, keepdims=True)
    cnt_ref[...] = seen_ref[...]

    record = jnp.zeros_like(logits)
    for slot, val in ((INFO_E1, e1), (INFO_E2, e2), (INFO_W1, w1), (INFO_W2, w2),
                      (INFO_R1, r1), (INFO_R2, r2)):
        record = jnp.where(lane == slot, val, record)
    info_ref[0] = record
    route_ref[0] = record.T[:ROUTE_ROWS]


def _seq_rows_spec(t_rows, width):
    return pl.BlockSpec((pl.Element(1), pl.Element(t_rows), pl.Element(width)),
                        lambda bb, j: (bb, pl.multiple_of(N_META + j * t_rows, N_META), 0))


def _router(h3, wr_pad, cfg):
    tr = cfg.t_route
    steps = cfg.seq // tr
    return pl.pallas_call(
        functools.partial(_router_kernel, n_experts=cfg.n_experts),
        grid=(cfg.batch, cfg.seq // tr),
        in_specs=[_seq_rows_spec(tr, D_MODEL),
                  pl.BlockSpec((D_MODEL, LANE), lambda bb, j: (0, 0))],
        out_specs=[pl.BlockSpec((1, tr, LANE), lambda bb, j: (bb, j, 0)),
                   pl.BlockSpec((1, ROUTE_ROWS, tr), lambda bb, j: (bb * steps + j, 0, 0)),
                   pl.BlockSpec((1, LANE), lambda bb, j: (0, 0))],
        out_shape=[jax.ShapeDtypeStruct((cfg.batch, cfg.seq, LANE), F32),
                   jax.ShapeDtypeStruct((cfg.batch * steps, ROUTE_ROWS, tr), F32),
                   jax.ShapeDtypeStruct((1, LANE), F32)],
        scratch_shapes=[pltpu.VMEM((1, LANE), F32)],
        compiler_params=_params(("arbitrary", "arbitrary")),
        name="router_top2",
    )(h3, wr_pad)


ZERO_ROWS = 8
ISSUE_UNROLL = 8


def _dispatch_kernel(pos_ref, fill_ref, h_ref, xs_ref, zero_ref, sem, *, t_route, n_experts):
    def zero_copy(p, n):
        start = pl.multiple_of(p * n, n)
        return pltpu.make_async_copy(zero_ref.at[pl.ds(0, n)], xs_ref.at[pl.ds(start, n)], sem.at[2])

    def zero_rows(lo, hi, n):
        def start(p, carry):
            zero_copy(p, n).start()
            return carry

        def wait(p, carry):
            zero_copy(p, n).wait()
            return carry

        lax.fori_loop(lo, hi, start, 0)
        lax.fori_loop(lo, hi, wait, 0)

    @pl.when(jnp.logical_and(pl.program_id(0) == 0, pl.program_id(1) == 0))
    def _():
        zero_ref[...] = jnp.zeros_like(zero_ref)
        for e in range(n_experts):
            zero_rows(fill_ref[0, e], fill_ref[1, e], 1)
        zero_rows(fill_ref[0, n_experts], fill_ref[1, n_experts], ZERO_ROWS)

    def issue(r, carry):
        for slot in range(TOP_K):
            pltpu.make_async_copy(h_ref.at[0, pl.ds(r, 1)],
                                  xs_ref.at[pl.ds(pos_ref[0, slot, r], 1)],
                                  sem.at[slot]).start()
        return carry

    lax.fori_loop(0, t_route, issue, 0, unroll=ISSUE_UNROLL)
    for slot in range(TOP_K):
        pltpu.make_async_copy(h_ref.at[0], xs_ref.at[pl.ds(0, t_route)], sem.at[slot]).wait()


def _dispatch(pos, fill, h3, cfg, rows):
    tr = cfg.t_route
    steps = cfg.seq // tr
    return pl.pallas_call(
        functools.partial(_dispatch_kernel, t_route=tr, n_experts=cfg.n_experts),
        grid=(cfg.batch, steps),
        in_specs=[pl.BlockSpec((1, TOP_K, tr), lambda bb, j: (bb * steps + j, 0, 0),
                               memory_space=pltpu.SMEM),
                  pl.BlockSpec(memory_space=pltpu.SMEM),
                  _seq_rows_spec(tr, D_MODEL)],
        out_specs=pl.BlockSpec(memory_space=pl.ANY),
        out_shape=jax.ShapeDtypeStruct((rows, D_MODEL), F32),
        scratch_shapes=[pltpu.VMEM((ZERO_ROWS, D_MODEL), F32), pltpu.SemaphoreType.DMA((3,))],
        compiler_params=_params(("arbitrary", "arbitrary")),
        name="moe_dispatch",
    )(pos, fill, h3)


def _grouped_kernel(te_ref, nv_ref, x_ref, wg_ref, wu_ref, wd_ref, y_ref, xb_ref, *, t_chunk):
    f = pl.program_id(1)

    @pl.when(pl.program_id(0) < nv_ref[0])
    def _():
        @pl.when(f == 0)
        def _():
            xb_ref[...] = x_ref[...].astype(BF16)

        x = xb_ref[...]
        acc = None
        for c in range(wg_ref.shape[2] // t_chunk):
            cols = pl.ds(c * t_chunk, t_chunk)
            contrib = _swiglu_contrib(x, wg_ref.at[0, :, cols], wu_ref.at[0, :, cols], wd_ref.at[0, cols, :])
            acc = contrib if acc is None else acc + contrib

        @pl.when(f == 0)
        def _():
            y_ref[...] = acc

        @pl.when(f > 0)
        def _():
            y_ref[...] += acc

    @pl.when(pl.program_id(0) >= nv_ref[0])
    def _():
        y_ref[...] = jnp.zeros_like(y_ref)


def _grouped_ffn(tile_expert, n_valid, xs, wg, wu, wd, cfg):
    tg, tf = cfg.t_group, cfg.t_ff_expert
    n_f = cfg.d_ff_expert // tf
    rows = xs.shape[0]

    def live(i, nv):
        return jnp.minimum(i, nv[0] - 1)

    def f_idx(i, f, nv):
        return jnp.where(i < nv[0], f, n_f - 1)

    grid_spec = pltpu.PrefetchScalarGridSpec(
        num_scalar_prefetch=2,
        grid=(rows // tg, n_f),
        in_specs=[pl.BlockSpec((tg, D_MODEL), lambda i, f, te, nv: (live(i, nv), 0)),
                  pl.BlockSpec((1, D_MODEL, tf), lambda i, f, te, nv: (te[i], 0, f_idx(i, f, nv))),
                  pl.BlockSpec((1, D_MODEL, tf), lambda i, f, te, nv: (te[i], 0, f_idx(i, f, nv))),
                  pl.BlockSpec((1, tf, D_MODEL), lambda i, f, te, nv: (te[i], f_idx(i, f, nv), 0))],
        out_specs=pl.BlockSpec((tg, D_MODEL), lambda i, f, te, nv: (i, 0)),
        scratch_shapes=[pltpu.VMEM((tg, D_MODEL), BF16)],
    )
    return pl.pallas_call(
        functools.partial(_grouped_kernel, t_chunk=cfg.t_ff_dense),
        grid_spec=grid_spec,
        out_shape=jax.ShapeDtypeStruct((rows, D_MODEL), F32),
        compiler_params=_params(("arbitrary", "arbitrary")),
        name="moe_grouped_ffn",
    )(tile_expert, n_valid, xs, wg, wu, wd)


def _combine_kernel(pos_ref, next_pos_ref, info_ref, h_ref, g_ref, b_ref, ys_ref, out_ref, rows_ref, sem,
                    *, alpha, t_route):
    step = pl.program_id(0) * pl.num_programs(1) + pl.program_id(1)
    last = pl.num_programs(0) * pl.num_programs(1) - 1
    buf = step % 2

    def gather(tile_pos_ref, into):
        def issue(r, carry):
            for slot in range(TOP_K):
                pltpu.make_async_copy(ys_ref.at[pl.ds(tile_pos_ref[0, slot, r], 1)],
                                      rows_ref.at[into, slot, pl.ds(r, 1)], sem.at[into, slot]).start()
            return carry

        lax.fori_loop(0, t_route, issue, 0, unroll=ISSUE_UNROLL)

    @pl.when(step == 0)
    def _():
        gather(pos_ref, 0)

    @pl.when(step < last)
    def _():
        gather(next_pos_ref, 1 - buf)

    for slot in range(TOP_K):
        pltpu.make_async_copy(ys_ref.at[pl.ds(0, t_route)], rows_ref.at[buf, slot], sem.at[buf, slot]).wait()

    info = info_ref[0]
    lane = lax.broadcasted_iota(jnp.int32, info.shape, 1)
    ffn = (_lane_pick(info, lane, INFO_W1) * rows_ref[buf, 0]
           + _lane_pick(info, lane, INFO_W2) * rows_ref[buf, 1])
    out_ref[0] = _layer_norm(alpha * h_ref[0] + ffn, g_ref[...], b_ref[...])


def _combine(pos, info, h3, g, b, ys, cfg, alpha):
    tr = cfg.t_route
    steps = cfg.seq // tr
    n_tiles = cfg.batch * steps
    const = lambda bb, j: (0, 0)

    def pos_spec(ahead):
        return pl.BlockSpec((1, TOP_K, tr),
                            lambda bb, j: (jnp.minimum(bb * steps + j + ahead, n_tiles - 1), 0, 0),
                            memory_space=pltpu.SMEM)

    return pl.pallas_call(
        functools.partial(_combine_kernel, alpha=alpha, t_route=tr),
        grid=(cfg.batch, steps),
        in_specs=[pos_spec(0), pos_spec(1),
                  pl.BlockSpec((1, tr, LANE), lambda bb, j: (bb, j, 0)),
                  _seq_rows_spec(tr, D_MODEL),
                  pl.BlockSpec((1, D_MODEL), const),
                  pl.BlockSpec((1, D_MODEL), const),
                  pl.BlockSpec(memory_space=pl.ANY)],
        out_specs=pl.BlockSpec((1, tr, D_MODEL), lambda bb, j: (bb, j, 0)),
        out_shape=jax.ShapeDtypeStruct((cfg.batch, cfg.seq, D_MODEL), F32),
        scratch_shapes=[pltpu.VMEM((2, TOP_K, tr, D_MODEL), F32), pltpu.SemaphoreType.DMA((2, TOP_K))],
        compiler_params=_params(("arbitrary", "arbitrary")),
        name="moe_combine_ln",
    )(pos, pos, info, h3, g, b, ys)


def _routed_moe(h3, w_router, wg, wu, wd, g, b, cfg, alpha):
    n_e, tg, tr = cfg.n_experts, cfg.t_group, cfg.t_route
    n_tok = cfg.batch * cfg.seq
    rows = TOP_K * n_tok + n_e * tg
    wr_pad = jnp.pad(w_router, ((0, 0), (0, LANE - n_e)))
    info, route, seen = _router(h3, wr_pad, cfg)

    as_int = lambda row: route[:, row, :].astype(jnp.int32)
    counts = seen[0, :n_e].astype(jnp.int32)
    padded = (counts + tg - 1) // tg * tg
    ends = jnp.cumsum(padded)
    starts = ends - padded
    start_of = lambda e: jnp.sum(jnp.where(e[..., None] == jnp.arange(n_e), starts, 0), axis=-1)
    pos = jnp.stack([start_of(as_int(INFO_E1)) + as_int(INFO_R1),
                     start_of(as_int(INFO_E2)) + as_int(INFO_R2)], axis=1)
    fill = jnp.stack([jnp.append(starts + counts, ends[-1] // ZERO_ROWS),
                      jnp.append(ends, rows // ZERO_ROWS)])
    n_valid = (ends[-1] // tg).reshape(1)
    tile_start = jnp.arange(rows // tg, dtype=jnp.int32) * tg
    tile_expert = jnp.sum(tile_start[:, None] >= ends[None, :], axis=-1).astype(jnp.int32)
    tile_expert = tile_expert[jnp.minimum(jnp.arange(rows // tg), n_valid[0] - 1)]

    xs = _dispatch(pos, fill, h3, cfg, rows)
    ys = _grouped_ffn(tile_expert, n_valid, xs, wg, wu, wd, cfg)
    return _combine(pos, info, h3, g, b, ys, cfg, alpha)


def _rope_tables(lp):
    half = HEAD_DIM // 2
    inv = 1.0 / (ROPE_THETA ** (jnp.arange(0, HEAD_DIM, 2, dtype=F32) / HEAD_DIM))
    pos = jnp.arange(lp, dtype=F32)
    ang = pos[:, None] * inv[None, :]
    ang = jnp.tile(ang, (1, LANE // half))
    sign = jnp.where((jnp.arange(LANE) % HEAD_DIM) < half, -1.0, 1.0).astype(F32)
    return jnp.cos(ang), jnp.sin(ang) * sign[None, :]


def _forward(cfg, x, meta_tokens, ln_emb_g, ln_emb_b, w_in, conv_w, lambda_q1, lambda_k1,
             lambda_q2, lambda_k2, subln_g, w_out, ln_mix_g, ln_mix_b, ln_ffn_g, ln_ffn_b,
             w_gate_dense, w_up_dense, w_down_dense, w_router, w_gate_moe, w_up_moe, w_down_moe):
    alpha = (2 * cfg.depth) ** 0.25
    tokens = cfg.tokens
    vec = lambda a: a.reshape(1, -1)
    cos, sin = _rope_tables(cfg.lp)
    h, hb = _embed(x, meta_tokens, vec(ln_emb_g), vec(ln_emb_b), cfg)
    h = h.reshape(tokens, D_MODEL)
    for layer in range(cfg.depth):
        lambda_init = 0.8 - 0.6 * math.exp(-0.3 * layer)
        conv, qa, qb, k, v = _inproj(hb.reshape(cfg.batch, cfg.lp, D_MODEL), w_in[layer].astype(BF16),
                                     conv_w[layer], cos, sin, cfg)
        attn = _attention(qa, qb, k, v, vec(lambda_q1[layer]), vec(lambda_k1[layer]),
                          vec(lambda_q2[layer]), vec(lambda_k2[layer]), vec(subln_g[layer]),
                          lambda_init, cfg)
        dense = layer % 2 == 0
        h, *hb = _outproj(conv.reshape(tokens, D_CONV), attn.reshape(tokens, D_MODEL - D_CONV),
                          w_out[layer].astype(BF16), h, vec(ln_mix_g[layer]), vec(ln_mix_b[layer]),
                          cfg, alpha, with_bf16=dense)
        idx = layer // 2
        g, b = vec(ln_ffn_g[layer]), vec(ln_ffn_b[layer])
        if dense:
            hb, = hb
            h, hb = _ffn(hb, w_gate_dense[idx].astype(BF16), w_up_dense[idx].astype(BF16),
                         w_down_dense[idx].astype(BF16), h, g, b, cfg, alpha)
        else:
            assert layer == cfg.depth - 1
            return _routed_moe(h.reshape(cfg.batch, cfg.lp, D_MODEL), w_router[idx],
                               w_gate_moe[idx].astype(BF16), w_up_moe[idx].astype(BF16),
                               w_down_moe[idx].astype(BF16), g, b, cfg, alpha)
    return h.reshape(cfg.batch, cfg.lp, D_MODEL)[:, N_META:N_META + cfg.seq]


_CFG = Cfg(batch=8, seq=4096, depth=2, d_ff_dense=2816, d_ff_expert=3584, n_experts=8,
           t_embed=128, t_proj=1056, t_q=384, t_k=384, t_tok=1024, t_ff_dense=256, t_ff_expert=1792,
           t_route=1024, t_group=1024)


def kernel(x, meta_tokens, ln_emb_g, ln_emb_b, w_in, conv_w, lambda_q1, lambda_k1, lambda_q2, lambda_k2, subln_g, w_out, ln_mix_g, ln_mix_b, ln_ffn_g, ln_ffn_b, w_gate_dense, w_up_dense, w_down_dense, w_router, w_gate_moe, w_up_moe, w_down_moe):
    return _forward(_CFG, x, meta_tokens, ln_emb_g, ln_emb_b, w_in, conv_w, lambda_q1, lambda_k1,
                    lambda_q2, lambda_k2, subln_g, w_out, ln_mix_g, ln_mix_b, ln_ffn_g, ln_ffn_b,
                    w_gate_dense, w_up_dense, w_down_dense, w_router, w_gate_moe, w_up_moe, w_down_moe)
```

```python
import dataclasses
import functools
import math

import jax
import jax.numpy as jnp
from jax import lax
from jax.experimental import pallas as pl
from jax.experimental.pallas import tpu as pltpu

F32 = jnp.float32
BF16 = jnp.bfloat16

LANE = 128
D_MODEL = 1024
N_META = 16
PAD = LANE - N_META
D_CONV = D_MODEL // 2
CONV_WIDTH = 3
HEAD_DIM = 64
N_HEADS = 4
V_DIM = 2 * HEAD_DIM
QK_W = N_HEADS * 2 * HEAD_DIM
GROUP_W = 512
ROPE_THETA = 10000.0
TOP_K = 2
LN_EPS = 1e-5
RMS_EPS = 1e-5
NEG_BIG = -1e30
VMEM_LIMIT = 56 * 1024 * 1024


@dataclasses.dataclass(frozen=True)
class Cfg:
    batch: int
    seq: int
    depth: int
    d_ff_dense: int
    d_ff_expert: int
    n_experts: int
    t_embed: int
    t_proj: int
    t_q: int
    t_k: int
    t_tok: int
    t_ff_dense: int
    t_ff_expert: int
    t_route: int
    t_group: int

    @property
    def lp(self):
        return N_META + self.seq + PAD

    @property
    def tokens(self):
        return self.batch * self.lp


def _params(semantics):
    return pltpu.CompilerParams(dimension_semantics=semantics, vmem_limit_bytes=VMEM_LIMIT)


def _layer_norm(y, g, b):
    mu = jnp.mean(y, axis=-1, keepdims=True)
    yc = y - mu
    var = jnp.mean(yc * yc, axis=-1, keepdims=True)
    return yc * lax.rsqrt(var + LN_EPS) * g + b


def _embed_kernel(x0_ref, x1_ref, x2_ref, x3_ref, meta_ref, g_ref, b_ref, h_ref, hb_ref, *, seq):
    j = pl.program_id(1)
    te = x1_ref.shape[1]
    head = jnp.where(j == 0, meta_ref[...], x0_ref[0, te - N_META:])
    rows = jnp.concatenate([head, x1_ref[0], x2_ref[0], x3_ref[0, :te - N_META]], axis=0)
    rid = j * (3 * te) + lax.broadcasted_iota(jnp.int32, (3 * te, 1), 0)
    rows = jnp.where(rid < N_META + seq, rows, 0.0)
    y = _layer_norm(rows, g_ref[...], b_ref[...])
    h_ref[0] = y
    hb_ref[0] = y.astype(BF16)


def _embed(x, meta_tokens, g, b, cfg):
    te = cfg.t_embed
    steps = cfg.lp // (3 * te)
    last = cfg.seq // te - 1

    def x_spec(i):
        return pl.BlockSpec((1, te, D_MODEL), lambda bb, j: (bb, jnp.clip(3 * j + i - 1, 0, last), 0))

    const = lambda bb, j: (0, 0)
    out_spec = pl.BlockSpec((1, 3 * te, D_MODEL), lambda bb, j: (bb, j, 0))
    return pl.pallas_call(
        functools.partial(_embed_kernel, seq=cfg.seq),
        grid=(cfg.batch, steps),
        in_specs=[x_spec(0), x_spec(1), x_spec(2), x_spec(3),
                  pl.BlockSpec((N_META, D_MODEL), const),
                  pl.BlockSpec((1, D_MODEL), const),
                  pl.BlockSpec((1, D_MODEL), const)],
        out_specs=[out_spec, out_spec],
        out_shape=[jax.ShapeDtypeStruct((cfg.batch, cfg.lp, D_MODEL), F32),
                   jax.ShapeDtypeStruct((cfg.batch, cfg.lp, D_MODEL), BF16)],
        compiler_params=_params(("parallel", "arbitrary")),
        name="embed_ln",
    )(x, x, x, x, meta_tokens, g, b)


def _inproj_kernel(hb_ref, w_ref, cw_ref, cos_ref, sin_ref,
                   conv_ref, qa_ref, qb_ref, k_ref, v_ref, carry_ref, *, t_proj):
    j = pl.program_id(1)
    x = hb_ref[0]

    def proj(g):
        return jnp.dot(x, w_ref[:, g * GROUP_W:(g + 1) * GROUP_W], preferred_element_type=F32)

    rid = lax.broadcasted_iota(jnp.int32, (t_proj, 1), 0)
    gated = proj(1) * proj(2)

    @pl.when(j == 0)
    def _():
        carry_ref[...] = jnp.zeros_like(carry_ref)

    prev = carry_ref[...]
    back1 = jnp.where(rid == 0, prev[7:8], pltpu.roll(gated, 1, 0))
    back2 = jnp.where(rid == 0, prev[6:7], jnp.where(rid == 1, prev[7:8], pltpu.roll(gated, 2, 0)))
    carry_ref[...] = gated[t_proj - 8:]
    cw = cw_ref[...]
    conv = cw[0:1] * back2 + cw[1:2] * back1 + cw[2:3] * gated
    conv_ref[0] = (proj(0) * conv).astype(BF16)

    cos = cos_ref[...]
    sin = sin_ref[...]
    lane = lax.broadcasted_iota(jnp.int32, (1, LANE), 1)
    low_half = (lane % HEAD_DIM) < HEAD_DIM // 2
    first_map = lane < HEAD_DIM

    def rope(z):
        partner = jnp.where(low_half, pltpu.roll(z, LANE - HEAD_DIM // 2, 1),
                            pltpu.roll(z, HEAD_DIM // 2, 1))
        return z * cos + partner * sin

    zq = proj(3)
    zk = proj(4)
    scale = HEAD_DIM ** -0.5 * math.log2(math.e)
    for h in range(N_HEADS):
        sl = slice(h * LANE, (h + 1) * LANE)
        q = rope(zq[:, sl]) * scale
        qa_ref[0, :, sl] = jnp.where(first_map, q, 0.0).astype(BF16)
        qb_ref[0, :, sl] = jnp.where(first_map, 0.0, q).astype(BF16)
        k_ref[0, :, sl] = rope(zk[:, sl]).astype(BF16)
    v_ref[0] = proj(5).astype(BF16)


def _inproj(hb, w_in, conv_w, cos, sin, cfg):
    tp = cfg.t_proj
    row = lambda bb, j: (bb, j, 0)
    const = lambda bb, j: (0, 0)
    out_spec = pl.BlockSpec((1, tp, GROUP_W), row)
    out_shape = jax.ShapeDtypeStruct((cfg.batch, cfg.lp, GROUP_W), BF16)
    return pl.pallas_call(
        functools.partial(_inproj_kernel, t_proj=tp),
        grid=(cfg.batch, cfg.lp // tp),
        in_specs=[pl.BlockSpec((1, tp, D_MODEL), row),
                  pl.BlockSpec((D_MODEL, 6 * GROUP_W), const),
                  pl.BlockSpec((CONV_WIDTH, D_CONV), const),
                  pl.BlockSpec((tp, LANE), lambda bb, j: (j, 0)),
                  pl.BlockSpec((tp, LANE), lambda bb, j: (j, 0))],
        out_specs=[out_spec] * 5,
        out_shape=[out_shape] * 5,
        scratch_shapes=[pltpu.VMEM((8, D_CONV), F32)],
        compiler_params=_params(("parallel", "arbitrary")),
        name="inproj_conv_rope",
    )(hb, w_in, conv_w, cos, sin)


def _attn_kernel(lq1_ref, lk1_ref, lq2_ref, lk2_ref, g_ref, qa_ref, qb_ref, k_ref, v_ref,
                 o_ref, m_ref, l_ref, acc_ref, *, t_q, t_k, lambda_init):
    j = pl.program_id(1)
    head = lambda h: slice(h * LANE, (h + 1) * LANE)
    q2 = [jnp.concatenate([qa_ref[0, :, head(h)], qb_ref[0, :, head(h)]], axis=0) for h in range(N_HEADS)]
    m_ref[...] = jnp.full_like(m_ref, NEG_BIG)
    l_ref[...] = jnp.zeros_like(l_ref)
    acc_ref[...] = jnp.zeros_like(acc_ref)

    def step(c, width, causal):
        start = pl.multiple_of(c * t_k, t_k)
        for h in range(N_HEADS):
            k = k_ref[0, pl.ds(start, width), head(h)]
            v = v_ref[0, pl.ds(start, width), head(h)]
            s = lax.dot_general(q2[h], k, (((1,), (1,)), ((), ())), preferred_element_type=F32)
            if causal:
                qpos = lax.broadcasted_iota(jnp.int32, (2 * t_q, 1), 0) % t_q
                kpos = lax.broadcasted_iota(jnp.int32, (1, width), 1) - (width - t_k)
                s = jnp.where(kpos <= qpos, s, NEG_BIG)
            m_prev = m_ref[h]
            m_new = jnp.maximum(m_prev, jnp.max(s, axis=-1, keepdims=True))
            alpha = jnp.exp2(m_prev - m_new)
            p = jnp.exp2((s - jnp.concatenate([m_new] * (width // LANE), axis=1)).astype(BF16))
            l_ref[h] = alpha * l_ref[h] + jnp.sum(p.astype(F32), axis=-1, keepdims=True)
            acc_ref[h] = alpha * acc_ref[h] + jnp.dot(p, v, preferred_element_type=F32)
            m_ref[h] = m_new

    def pair(i, carry):
        step(2 * i, 2 * t_k, causal=False)
        return carry

    lax.fori_loop(0, j // 2, pair, 0)

    @pl.when(j % 2 == 1)
    def _():
        step(j - 1, 2 * t_k, causal=True)

    @pl.when(j % 2 == 0)
    def _():
        step(j, t_k, causal=True)

    lam = (jnp.exp(jnp.sum(lq1_ref[...] * lk1_ref[...], axis=-1, keepdims=True))
           - jnp.exp(jnp.sum(lq2_ref[...] * lk2_ref[...], axis=-1, keepdims=True)) + lambda_init)
    for h in range(N_HEADS):
        o = acc_ref[h] / l_ref[h]
        d = o[:t_q] - lam * o[t_q:]
        d = d * lax.rsqrt(jnp.mean(d * d, axis=-1, keepdims=True) + RMS_EPS) * g_ref[...]
        o_ref[0, :, head(h)] = (d * (1.0 - lambda_init)).astype(BF16)


def _attention(qa, qb, k, v, lq1, lk1, lq2, lk2, subln_g, lambda_init, cfg):
    tq = cfg.t_q
    width = N_HEADS * V_DIM
    small = lambda bb, j: (0, 0)
    q_spec = pl.BlockSpec((1, tq, width), lambda bb, j: (bb, j, 0))
    kv_spec = pl.BlockSpec((1, cfg.lp, width), lambda bb, j: (bb, 0, 0))
    return pl.pallas_call(
        functools.partial(_attn_kernel, t_q=tq, t_k=cfg.t_k, lambda_init=lambda_init),
        grid=(cfg.batch, cfg.lp // tq),
        in_specs=[pl.BlockSpec((1, HEAD_DIM), small)] * 4
        + [pl.BlockSpec((1, V_DIM), small), q_spec, q_spec, kv_spec, kv_spec],
        out_specs=q_spec,
        out_shape=jax.ShapeDtypeStruct((cfg.batch, cfg.lp, width), BF16),
        scratch_shapes=[pltpu.VMEM((N_HEADS, 2 * tq, LANE), F32),
                        pltpu.VMEM((N_HEADS, 2 * tq, LANE), F32),
                        pltpu.VMEM((N_HEADS, 2 * tq, V_DIM), F32)],
        compiler_params=_params(("parallel", "arbitrary")),
        name="diff_attention",
    )(lq1, lk1, lq2, lk2, subln_g, qa, qb, k, v)


def _outproj_kernel(conv_ref, attn_ref, w_ref, h_ref, g_ref, b_ref, ho_ref, *maybe_hbo_ref, alpha):
    mix = (jnp.dot(conv_ref[...], w_ref[:D_CONV], preferred_element_type=F32)
           + jnp.dot(attn_ref[...], w_ref[D_CONV:], preferred_element_type=F32))
    y = _layer_norm(alpha * h_ref[...] + mix, g_ref[...], b_ref[...])
    ho_ref[...] = y
    for hbo_ref in maybe_hbo_ref:
        hbo_ref[...] = y.astype(BF16)


def _outproj(conv, attn, w_out, h, g, b, cfg, alpha, with_bf16):
    tt = cfg.t_tok
    n_out = 2 if with_bf16 else 1
    row = lambda i: (i, 0)
    const = lambda i: (0, 0)
    return pl.pallas_call(
        functools.partial(_outproj_kernel, alpha=alpha),
        grid=(cfg.tokens // tt,),
        in_specs=[pl.BlockSpec((tt, D_CONV), row),
                  pl.BlockSpec((tt, D_MODEL - D_CONV), row),
                  pl.BlockSpec((D_MODEL, D_MODEL), const),
                  pl.BlockSpec((tt, D_MODEL), row),
                  pl.BlockSpec((1, D_MODEL), const),
                  pl.BlockSpec((1, D_MODEL), const)],
        out_specs=[pl.BlockSpec((tt, D_MODEL), row)] * n_out,
        out_shape=[jax.ShapeDtypeStruct((cfg.tokens, D_MODEL), F32),
                   jax.ShapeDtypeStruct((cfg.tokens, D_MODEL), BF16)][:n_out],
        compiler_params=_params(("parallel",)),
        name="outproj_ln",
    )(conv, attn, w_out, h, g, b)


def _swiglu_contrib(x, wg_ref, wu_ref, wd_ref):
    gate = jnp.dot(x, wg_ref[...], preferred_element_type=F32)
    up = jnp.dot(x, wu_ref[...], preferred_element_type=F32)
    act = gate * (1.0 / (1.0 + jnp.exp(-gate))) * up
    return jnp.dot(act.astype(BF16), wd_ref[...], preferred_element_type=F32)


def _ffn_kernel(x_ref, wg_ref, wu_ref, wd_ref, h_ref, g_ref, b_ref, ho_ref, hbo_ref, *, alpha, t_ff):
    x = x_ref[...]
    acc = None
    for c in range(wg_ref.shape[1] // t_ff):
        cols = pl.ds(c * t_ff, t_ff)
        contrib = _swiglu_contrib(x, wg_ref.at[:, cols], wu_ref.at[:, cols], wd_ref.at[cols, :])
        acc = contrib if acc is None else acc + contrib
    y = _layer_norm(alpha * h_ref[...] + acc, g_ref[...], b_ref[...])
    ho_ref[...] = y
    hbo_ref[...] = y.astype(BF16)


def _ffn(hb, wg, wu, wd, h, g, b, cfg, alpha):
    tt = cfg.t_tok
    row = lambda i: (i, 0)
    const = lambda i: (0, 0)
    resident = lambda shape: pl.BlockSpec(shape, const, pipeline_mode=pl.Buffered(1))
    return pl.pallas_call(
        functools.partial(_ffn_kernel, alpha=alpha, t_ff=cfg.t_ff_dense),
        grid=(cfg.tokens // tt,),
        in_specs=[pl.BlockSpec((tt, D_MODEL), row),
                  resident((D_MODEL, cfg.d_ff_dense)),
                  resident((D_MODEL, cfg.d_ff_dense)),
                  resident((cfg.d_ff_dense, D_MODEL)),
                  pl.BlockSpec((tt, D_MODEL), row),
                  pl.BlockSpec((1, D_MODEL), const),
                  pl.BlockSpec((1, D_MODEL), const)],
        out_specs=[pl.BlockSpec((tt, D_MODEL), row)] * 2,
        out_shape=[jax.ShapeDtypeStruct((cfg.tokens, D_MODEL), F32),
                   jax.ShapeDtypeStruct((cfg.tokens, D_MODEL), BF16)],
        compiler_params=_params(("parallel",)),
        name="dense_ffn_ln",
    )(hb, wg, wu, wd, h, g, b)


INFO_E1, INFO_E2, INFO_W1, INFO_W2, INFO_R1, INFO_R2 = range(6)
ROUTE_ROWS = 8


def _lane_pick(values, lane, index):
    return jnp.sum(jnp.where(lane == index, values, 0.0), axis=-1, keepdims=True)


def _router_kernel(h_ref, wr_ref, info_ref, route_ref, cnt_ref, seen_ref, *, n_experts):
    @pl.when(jnp.logical_and(pl.program_id(0) == 0, pl.program_id(1) == 0))
    def _():
        seen_ref[...] = jnp.zeros_like(seen_ref)

    h = h_ref[0]
    w = wr_ref[...]
    h_hi = h.astype(BF16)
    h_lo = (h - h_hi.astype(F32)).astype(BF16)
    w_hi = w.astype(BF16)
    w_lo = (w - w_hi.astype(F32)).astype(BF16)
    logits = (jnp.dot(h_hi, w_hi, preferred_element_type=F32)
              + jnp.dot(h_lo, w_hi, preferred_element_type=F32)
              + jnp.dot(h_hi, w_lo, preferred_element_type=F32))
    t_route = logits.shape[0]
    lane = lax.broadcasted_iota(jnp.int32, logits.shape, 1).astype(F32)
    logits = jnp.where(lane < n_experts, logits, -jnp.inf)
    v1 = jnp.max(logits, axis=-1, keepdims=True)
    e1 = jnp.min(jnp.where(logits == v1, lane, float(LANE)), axis=-1, keepdims=True)
    rest = jnp.where(lane == e1, -jnp.inf, logits)
    v2 = jnp.max(rest, axis=-1, keepdims=True)
    e2 = jnp.min(jnp.where(rest == v2, lane, float(LANE)), axis=-1, keepdims=True)
    ex = jnp.exp(v2 - v1)
    w1 = 1.0 / (1.0 + ex)
    w2 = ex / (1.0 + ex)

    chosen = jnp.where(lane == e1, 1.0, 0.0) + jnp.where(lane == e2, 1.0, 0.0)
    r = lax.broadcasted_iota(jnp.int32, (t_route, t_route), 0)
    c = lax.broadcasted_iota(jnp.int32, (t_route, t_route), 1)
    earlier = jnp.where(c < r, 1.0, 0.0).astype(BF16)
    before = jnp.dot(earlier, chosen.astype(BF16), preferred_element_type=F32) + seen_ref[...]
    r1 = _lane_pick(before, lane, e1)
    r2 = _lane_pick(before, lane, e2)
    seen_ref[...] += jnp.sum(chosen, axis=0, keepdims=True)
    cnt_ref[...] = seen_ref[...]

    record = jnp.zeros_like(logits)
    for slot, val in ((INFO_E1, e1), (INFO_E2, e2), (INFO_W1, w1), (INFO_W2, w2),
                      (INFO_R1, r1), (INFO_R2, r2)):
        record = jnp.where(lane == slot, val, record)
    info_ref[0] = record
    route_ref[0] = record.T[:ROUTE_ROWS]


def _seq_rows_spec(t_rows, width):
    return pl.BlockSpec((pl.Element(1), pl.Element(t_rows), pl.Element(width)),
                        lambda bb, j: (bb, pl.multiple_of(N_META + j * t_rows, N_META), 0))


def _router(h3, wr_pad, cfg):
    tr = cfg.t_route
    steps = cfg.seq // tr
    return pl.pallas_call(
        functools.partial(_router_kernel, n_experts=cfg.n_experts),
        grid=(cfg.batch, cfg.seq // tr),
        in_specs=[_seq_rows_spec(tr, D_MODEL),
                  pl.BlockSpec((D_MODEL, LANE), lambda bb, j: (0, 0))],
        out_specs=[pl.BlockSpec((1, tr, LANE), lambda bb, j: (bb, j, 0)),
                   pl.BlockSpec((1, ROUTE_ROWS, tr), lambda bb, j: (bb * steps + j, 0, 0)),
                   pl.BlockSpec((1, LANE), lambda bb, j: (0, 0))],
        out_shape=[jax.ShapeDtypeStruct((cfg.batch, cfg.seq, LANE), F32),
                   jax.ShapeDtypeStruct((cfg.batch * steps, ROUTE_ROWS, tr), F32),
                   jax.ShapeDtypeStruct((1, LANE), F32)],
        scratch_shapes=[pltpu.VMEM((1, LANE), F32)],
        compiler_params=_params(("arbitrary", "arbitrary")),
        name="router_top2",
    )(h3, wr_pad)


ZERO_ROWS = 8
ISSUE_UNROLL = 8


def _dispatch_kernel(pos_ref, fill_ref, h_ref, xs_ref, zero_ref, sem, *, t_route, n_experts):
    def zero_copy(p, n):
        start = pl.multiple_of(p * n, n)
        return pltpu.make_async_copy(zero_ref.at[pl.ds(0, n)], xs_ref.at[pl.ds(start, n)], sem.at[2])

    def zero_rows(lo, hi, n):
        def start(p, carry):
            zero_copy(p, n).start()
            return carry

        def wait(p, carry):
            zero_copy(p, n).wait()
            return carry

        lax.fori_loop(lo, hi, start, 0)
        lax.fori_loop(lo, hi, wait, 0)

    @pl.when(jnp.logical_and(pl.program_id(0) == 0, pl.program_id(1) == 0))
    def _():
        zero_ref[...] = jnp.zeros_like(zero_ref)
        for e in range(n_experts):
            zero_rows(fill_ref[0, e], fill_ref[1, e], 1)
        zero_rows(fill_ref[0, n_experts], fill_ref[1, n_experts], ZERO_ROWS)

    def issue(r, carry):
        for slot in range(TOP_K):
            pltpu.make_async_copy(h_ref.at[0, pl.ds(r, 1)],
                                  xs_ref.at[pl.ds(pos_ref[0, 0, slot * t_route + r], 1)],
                                  sem.at[slot]).start(priority=slot)
        return carry

    lax.fori_loop(0, t_route, issue, 0, unroll=ISSUE_UNROLL)
    for slot in range(TOP_K):
        pltpu.make_async_copy(h_ref.at[0], xs_ref.at[pl.ds(0, t_route)], sem.at[slot]).wait()


def _dispatch(pos, fill, h3, cfg, rows):
    tr = cfg.t_route
    steps = cfg.seq // tr
    return pl.pallas_call(
        functools.partial(_dispatch_kernel, t_route=tr, n_experts=cfg.n_experts),
        grid=(cfg.batch, steps),
        in_specs=[pl.BlockSpec((1, 1, TOP_K * tr), lambda bb, j: (bb * steps + j, 0, 0),
                               memory_space=pltpu.SMEM),
                  pl.BlockSpec(memory_space=pltpu.SMEM),
                  _seq_rows_spec(tr, D_MODEL)],
        out_specs=pl.BlockSpec(memory_space=pl.ANY),
        out_shape=jax.ShapeDtypeStruct((rows, D_MODEL), F32),
        scratch_shapes=[pltpu.VMEM((ZERO_ROWS, D_MODEL), F32), pltpu.SemaphoreType.DMA((3,))],
        compiler_params=_params(("arbitrary", "arbitrary")),
        name="moe_dispatch",
    )(pos, fill, h3)


def _grouped_kernel(te_ref, nv_ref, x_ref, wg_ref, wu_ref, wd_ref, y_ref, xb_ref, *, t_chunk):
    f = pl.program_id(1)

    @pl.when(pl.program_id(0) < nv_ref[0])
    def _():
        @pl.when(f == 0)
        def _():
            xb_ref[...] = x_ref[...].astype(BF16)

        x = xb_ref[...]
        acc = None
        for c in range(wg_ref.shape[2] // t_chunk):
            cols = pl.ds(c * t_chunk, t_chunk)
            contrib = _swiglu_contrib(x, wg_ref.at[0, :, cols], wu_ref.at[0, :, cols], wd_ref.at[0, cols, :])
            acc = contrib if acc is None else acc + contrib

        @pl.when(f == 0)
        def _():
            y_ref[...] = acc

        @pl.when(f > 0)
        def _():
            y_ref[...] += acc

    @pl.when(pl.program_id(0) >= nv_ref[0])
    def _():
        y_ref[...] = jnp.zeros_like(y_ref)


def _grouped_ffn(tile_expert, n_valid, xs, wg, wu, wd, cfg):
    tg, tf = cfg.t_group, cfg.t_ff_expert
    n_f = cfg.d_ff_expert // tf
    rows = xs.shape[0]

    def live(i, nv):
        return jnp.minimum(i, nv[0] - 1)

    def f_idx(i, f, nv):
        return jnp.where(i < nv[0], f, n_f - 1)

    grid_spec = pltpu.PrefetchScalarGridSpec(
        num_scalar_prefetch=2,
        grid=(rows // tg, n_f),
        in_specs=[pl.BlockSpec((tg, D_MODEL), lambda i, f, te, nv: (live(i, nv), 0)),
                  pl.BlockSpec((1, D_MODEL, tf), lambda i, f, te, nv: (te[i], 0, f_idx(i, f, nv))),
                  pl.BlockSpec((1, D_MODEL, tf), lambda i, f, te, nv: (te[i], 0, f_idx(i, f, nv))),
                  pl.BlockSpec((1, tf, D_MODEL), lambda i, f, te, nv: (te[i], f_idx(i, f, nv), 0))],
        out_specs=pl.BlockSpec((tg, D_MODEL), lambda i, f, te, nv: (i, 0)),
        scratch_shapes=[pltpu.VMEM((tg, D_MODEL), BF16)],
    )
    return pl.pallas_call(
        functools.partial(_grouped_kernel, t_chunk=cfg.t_ff_dense),
        grid_spec=grid_spec,
        out_shape=jax.ShapeDtypeStruct((rows, D_MODEL), F32),
        compiler_params=_params(("arbitrary", "arbitrary")),
        name="moe_grouped_ffn",
    )(tile_expert, n_valid, xs, wg, wu, wd)


def _combine_kernel(pos_ref, next_pos_ref, info_ref, h_ref, g_ref, b_ref, ys_ref, out_ref, rows_ref, sem,
                    *, alpha, t_route):
    step = pl.program_id(0) * pl.num_programs(1) + pl.program_id(1)
    last = pl.num_programs(0) * pl.num_programs(1) - 1
    buf = step % 2

    def gather(tile_pos_ref, into):
        def issue(r, carry):
            for slot in range(TOP_K):
                pltpu.make_async_copy(ys_ref.at[pl.ds(tile_pos_ref[0, 0, slot * t_route + r], 1)],
                                      rows_ref.at[into, slot, pl.ds(r, 1)],
                                      sem.at[into, slot]).start(priority=slot)
            return carry

        lax.fori_loop(0, t_route, issue, 0, unroll=ISSUE_UNROLL)

    @pl.when(step == 0)
    def _():
        gather(pos_ref, 0)

    @pl.when(step < last)
    def _():
        gather(next_pos_ref, 1 - buf)

    for slot in range(TOP_K):
        pltpu.make_async_copy(ys_ref.at[pl.ds(0, t_route)], rows_ref.at[buf, slot], sem.at[buf, slot]).wait()

    info = info_ref[0]
    lane = lax.broadcasted_iota(jnp.int32, info.shape, 1)
    ffn = (_lane_pick(info, lane, INFO_W1) * rows_ref[buf, 0]
           + _lane_pick(info, lane, INFO_W2) * rows_ref[buf, 1])
    out_ref[0] = _layer_norm(alpha * h_ref[0] + ffn, g_ref[...], b_ref[...])


def _combine(pos, info, h3, g, b, ys, cfg, alpha):
    tr = cfg.t_route
    steps = cfg.seq // tr
    n_tiles = cfg.batch * steps
    const = lambda bb, j: (0, 0)

    def pos_spec(ahead):
        return pl.BlockSpec((1, 1, TOP_K * tr),
                            lambda bb, j: (jnp.minimum(bb * steps + j + ahead, n_tiles - 1), 0, 0),
                            memory_space=pltpu.SMEM)

    return pl.pallas_call(
        functools.partial(_combine_kernel, alpha=alpha, t_route=tr),
        grid=(cfg.batch, steps),
        in_specs=[pos_spec(0), pos_spec(1),
                  pl.BlockSpec((1, tr, LANE), lambda bb, j: (bb, j, 0)),
                  _seq_rows_spec(tr, D_MODEL),
                  pl.BlockSpec((1, D_MODEL), const),
                  pl.BlockSpec((1, D_MODEL), const),
                  pl.BlockSpec(memory_space=pl.ANY)],
        out_specs=pl.BlockSpec((1, tr, D_MODEL), lambda bb, j: (bb, j, 0)),
        out_shape=jax.ShapeDtypeStruct((cfg.batch, cfg.seq, D_MODEL), F32),
        scratch_shapes=[pltpu.VMEM((2, TOP_K, tr, D_MODEL), F32), pltpu.SemaphoreType.DMA((2, TOP_K))],
        compiler_params=_params(("arbitrary", "arbitrary")),
        name="moe_combine_ln",
    )(pos, pos, info, h3, g, b, ys)


def _routed_moe(h3, w_router, wg, wu, wd, g, b, cfg, alpha):
    n_e, tg, tr = cfg.n_experts, cfg.t_group, cfg.t_route
    n_tok = cfg.batch * cfg.seq
    rows = TOP_K * n_tok + n_e * tg
    wr_pad = jnp.pad(w_router, ((0, 0), (0, LANE - n_e)))
    info, route, seen = _router(h3, wr_pad, cfg)

    as_int = lambda row: route[:, row, :].astype(jnp.int32)
    counts = seen[0, :n_e].astype(jnp.int32)
    padded = (counts + tg - 1) // tg * tg
    ends = jnp.cumsum(padded)
    starts = ends - padded
    start_of = lambda e: jnp.sum(jnp.where(e[..., None] == jnp.arange(n_e), starts, 0), axis=-1)
    pos = jnp.stack([start_of(as_int(INFO_E1)) + as_int(INFO_R1),
                     start_of(as_int(INFO_E2)) + as_int(INFO_R2)], axis=1)
    pos = pos.reshape(n_tok // tr, 1, TOP_K * tr)
    fill = jnp.stack([jnp.append(starts + counts, ends[-1] // ZERO_ROWS),
                      jnp.append(ends, rows // ZERO_ROWS)])
    n_valid = (ends[-1] // tg).reshape(1)
    tile_start = jnp.arange(rows // tg, dtype=jnp.int32) * tg
    tile_expert = jnp.sum(tile_start[:, None] >= ends[None, :], axis=-1).astype(jnp.int32)
    tile_expert = tile_expert[jnp.minimum(jnp.arange(rows // tg), n_valid[0] - 1)]

    xs = _dispatch(pos, fill, h3, cfg, rows)
    ys = _grouped_ffn(tile_expert, n_valid, xs, wg, wu, wd, cfg)
    return _combine(pos, info, h3, g, b, ys, cfg, alpha)


def _rope_tables(lp):
    half = HEAD_DIM // 2
    inv = 1.0 / (ROPE_THETA ** (jnp.arange(0, HEAD_DIM, 2, dtype=F32) / HEAD_DIM))
    pos = jnp.arange(lp, dtype=F32)
    ang = pos[:, None] * inv[None, :]
    ang = jnp.tile(ang, (1, LANE // half))
    sign = jnp.where((jnp.arange(LANE) % HEAD_DIM) < half, -1.0, 1.0).astype(F32)
    return jnp.cos(ang), jnp.sin(ang) * sign[None, :]


def _forward(cfg, x, meta_tokens, ln_emb_g, ln_emb_b, w_in, conv_w, lambda_q1, lambda_k1,
             lambda_q2, lambda_k2, subln_g, w_out, ln_mix_g, ln_mix_b, ln_ffn_g, ln_ffn_b,
             w_gate_dense, w_up_dense, w_down_dense, w_router, w_gate_moe, w_up_moe, w_down_moe):
    alpha = (2 * cfg.depth) ** 0.25
    tokens = cfg.tokens
    vec = lambda a: a.reshape(1, -1)
    cos, sin = _rope_tables(cfg.lp)
    h, hb = _embed(x, meta_tokens, vec(ln_emb_g), vec(ln_emb_b), cfg)
    h = h.reshape(tokens, D_MODEL)
    for layer in range(cfg.depth):
        lambda_init = 0.8 - 0.6 * math.exp(-0.3 * layer)
        conv, qa, qb, k, v = _inproj(hb.reshape(cfg.batch, cfg.lp, D_MODEL), w_in[layer].astype(BF16),
                                     conv_w[layer], cos, sin, cfg)
        attn = _attention(qa, qb, k, v, vec(lambda_q1[layer]), vec(lambda_k1[layer]),
                          vec(lambda_q2[layer]), vec(lambda_k2[layer]), vec(subln_g[layer]),
                          lambda_init, cfg)
        dense = layer % 2 == 0
        h, *hb = _outproj(conv.reshape(tokens, D_CONV), attn.reshape(tokens, D_MODEL - D_CONV),
                          w_out[layer].astype(BF16), h, vec(ln_mix_g[layer]), vec(ln_mix_b[layer]),
                          cfg, alpha, with_bf16=dense)
        idx = layer // 2
        g, b = vec(ln_ffn_g[layer]), vec(ln_ffn_b[layer])
        if dense:
            hb, = hb
            h, hb = _ffn(hb, w_gate_dense[idx].astype(BF16), w_up_dense[idx].astype(BF16),
                         w_down_dense[idx].astype(BF16), h, g, b, cfg, alpha)
        else:
            assert layer == cfg.depth - 1
            return _routed_moe(h.reshape(cfg.batch, cfg.lp, D_MODEL), w_router[idx],
                               w_gate_moe[idx].astype(BF16), w_up_moe[idx].astype(BF16),
                               w_down_moe[idx].astype(BF16), g, b, cfg, alpha)
    return h.reshape(cfg.batch, cfg.lp, D_MODEL)[:, N_META:N_META + cfg.seq]


_CFG = Cfg(batch=8, seq=4096, depth=2, d_ff_dense=2816, d_ff_expert=3584, n_experts=8,
           t_embed=128, t_proj=1056, t_q=384, t_k=384, t_tok=1024, t_ff_dense=256, t_ff_expert=1792,
           t_route=1024, t_group=1024)


def kernel(x, meta_tokens, ln_emb_g, ln_emb_b, w_in, conv_w, lambda_q1, lambda_k1, lambda_q2, lambda_k2, subln_g, w_out, ln_mix_g, ln_mix_b, ln_ffn_g, ln_ffn_b, w_gate_dense, w_up_dense, w_down_dense, w_router, w_gate_moe, w_up_moe, w_down_moe):
    return _forward(_CFG, x, meta_tokens, ln_emb_g, ln_emb_b, w_in, conv_w, lambda_q1, lambda_k1,
                    lambda_q2, lambda_k2, subln_g, w_out, ln_mix_g, ln_mix_b, ln_ffn_g, ln_ffn_b,
                    w_gate_dense, w_up_dense, w_down_dense, w_router, w_gate_moe, w_up_moe, w_down_moe)
```

```python
import dataclasses
import functools
import math

import jax
import jax.numpy as jnp
from jax import lax
from jax.experimental import pallas as pl
from jax.experimental.pallas import tpu as pltpu

F32 = jnp.float32
BF16 = jnp.bfloat16

LANE = 128
D_MODEL = 1024
N_META = 16
PAD = LANE - N_META
D_CONV = D_MODEL // 2
CONV_WIDTH = 3
HEAD_DIM = 64
N_HEADS = 4
V_DIM = 2 * HEAD_DIM
QK_W = N_HEADS * 2 * HEAD_DIM
GROUP_W = 512
ROPE_THETA = 10000.0
TOP_K = 2
LN_EPS = 1e-5
RMS_EPS = 1e-5
NEG_BIG = -1e30
VMEM_LIMIT = 56 * 1024 * 1024


@dataclasses.dataclass(frozen=True)
class Cfg:
    batch: int
    seq: int
    depth: int
    d_ff_dense: int
    d_ff_expert: int
    n_experts: int
    t_embed: int
    t_proj: int
    t_q: int
    t_k: int
    t_tok: int
    t_ff_dense: int
    t_ff_expert: int
    t_route: int
    t_group: int

    @property
    def lp(self):
        return N_META + self.seq + PAD

    @property
    def tokens(self):
        return self.batch * self.lp


def _params(semantics):
    return pltpu.CompilerParams(dimension_semantics=semantics, vmem_limit_bytes=VMEM_LIMIT)


def _layer_norm(y, g, b):
    mu = jnp.mean(y, axis=-1, keepdims=True)
    yc = y - mu
    var = jnp.mean(yc * yc, axis=-1, keepdims=True)
    return yc * lax.rsqrt(var + LN_EPS) * g + b


def _embed_kernel(x0_ref, x1_ref, x2_ref, x3_ref, meta_ref, g_ref, b_ref, h_ref, hb_ref, *, seq):
    j = pl.program_id(1)
    te = x1_ref.shape[1]
    head = jnp.where(j == 0, meta_ref[...], x0_ref[0, te - N_META:])
    rows = jnp.concatenate([head, x1_ref[0], x2_ref[0], x3_ref[0, :te - N_META]], axis=0)
    rid = j * (3 * te) + lax.broadcasted_iota(jnp.int32, (3 * te, 1), 0)
    rows = jnp.where(rid < N_META + seq, rows, 0.0)
    y = _layer_norm(rows, g_ref[...], b_ref[...])
    h_ref[0] = y
    hb_ref[0] = y.astype(BF16)


def _embed(x, meta_tokens, g, b, cfg):
    te = cfg.t_embed
    steps = cfg.lp // (3 * te)
    last = cfg.seq // te - 1

    def x_spec(i):
        return pl.BlockSpec((1, te, D_MODEL), lambda bb, j: (bb, jnp.clip(3 * j + i - 1, 0, last), 0))

    const = lambda bb, j: (0, 0)
    out_spec = pl.BlockSpec((1, 3 * te, D_MODEL), lambda bb, j: (bb, j, 0))
    return pl.pallas_call(
        functools.partial(_embed_kernel, seq=cfg.seq),
        grid=(cfg.batch, steps),
        in_specs=[x_spec(0), x_spec(1), x_spec(2), x_spec(3),
                  pl.BlockSpec((N_META, D_MODEL), const),
                  pl.BlockSpec((1, D_MODEL), const),
                  pl.BlockSpec((1, D_MODEL), const)],
        out_specs=[out_spec, out_spec],
        out_shape=[jax.ShapeDtypeStruct((cfg.batch, cfg.lp, D_MODEL), F32),
                   jax.ShapeDtypeStruct((cfg.batch, cfg.lp, D_MODEL), BF16)],
        compiler_params=_params(("parallel", "arbitrary")),
        name="embed_ln",
    )(x, x, x, x, meta_tokens, g, b)


def _inproj_kernel(hb_ref, w_ref, cw_ref, cos_ref, sin_ref,
                   conv_ref, q_ref, k_ref, v_ref, carry_ref, *, t_proj):
    j = pl.program_id(1)
    x = hb_ref[0]

    def proj(g):
        return jnp.dot(x, w_ref[:, g * GROUP_W:(g + 1) * GROUP_W], preferred_element_type=F32)

    rid = lax.broadcasted_iota(jnp.int32, (t_proj, 1), 0)
    gated = proj(1) * proj(2)

    @pl.when(j == 0)
    def _():
        carry_ref[...] = jnp.zeros_like(carry_ref)

    prev = carry_ref[...]
    back1 = jnp.where(rid == 0, prev[7:8], pltpu.roll(gated, 1, 0))
    back2 = jnp.where(rid == 0, prev[6:7], jnp.where(rid == 1, prev[7:8], pltpu.roll(gated, 2, 0)))
    carry_ref[...] = gated[t_proj - 8:]
    cw = cw_ref[...]
    conv = cw[0:1] * back2 + cw[1:2] * back1 + cw[2:3] * gated
    conv_ref[0] = (proj(0) * conv).astype(BF16)

    cos = cos_ref[...]
    sin = sin_ref[...]
    lane = lax.broadcasted_iota(jnp.int32, (1, LANE), 1)
    low_half = (lane % HEAD_DIM) < HEAD_DIM // 2

    def rope(z):
        partner = jnp.where(low_half, pltpu.roll(z, LANE - HEAD_DIM // 2, 1),
                            pltpu.roll(z, HEAD_DIM // 2, 1))
        return z * cos + partner * sin

    zq = proj(3)
    zk = proj(4)
    scale = HEAD_DIM ** -0.5 * math.log2(math.e)
    for h in range(N_HEADS):
        sl = slice(h * LANE, (h + 1) * LANE)
        q_ref[0, :, sl] = (rope(zq[:, sl]) * scale).astype(BF16)
        k_ref[0, :, sl] = rope(zk[:, sl]).astype(BF16)
    v_ref[0] = proj(5).astype(BF16)


def _inproj(hb, w_in, conv_w, cos, sin, cfg):
    tp = cfg.t_proj
    row = lambda bb, j: (bb, j, 0)
    const = lambda bb, j: (0, 0)
    out_spec = pl.BlockSpec((1, tp, GROUP_W), row)
    out_shape = jax.ShapeDtypeStruct((cfg.batch, cfg.lp, GROUP_W), BF16)
    return pl.pallas_call(
        functools.partial(_inproj_kernel, t_proj=tp),
        grid=(cfg.batch, cfg.lp // tp),
        in_specs=[pl.BlockSpec((1, tp, D_MODEL), row),
                  pl.BlockSpec((D_MODEL, 6 * GROUP_W), const),
                  pl.BlockSpec((CONV_WIDTH, D_CONV), const),
                  pl.BlockSpec((tp, LANE), lambda bb, j: (j, 0)),
                  pl.BlockSpec((tp, LANE), lambda bb, j: (j, 0))],
        out_specs=[out_spec] * 4,
        out_shape=[out_shape] * 4,
        scratch_shapes=[pltpu.VMEM((8, D_CONV), F32)],
        compiler_params=_params(("parallel", "arbitrary")),
        name="inproj_conv_rope",
    )(hb, w_in, conv_w, cos, sin)


def _attn_kernel(lq1_ref, lk1_ref, lq2_ref, lk2_ref, g_ref, q_ref, k_ref, v_ref,
                 o_ref, m_ref, l_ref, acc_ref, *, t_q, t_k, lambda_init):
    j = pl.program_id(1)
    head = lambda h: slice(h * LANE, (h + 1) * LANE)
    first_map = lax.broadcasted_iota(jnp.int32, (1, LANE), 1) < HEAD_DIM
    zero = jnp.zeros((), BF16)
    q2 = []
    for h in range(N_HEADS):
        q = q_ref[0, :, head(h)]
        q2.append(jnp.concatenate([jnp.where(first_map, q, zero), jnp.where(first_map, zero, q)], axis=0))
    m_ref[...] = jnp.full_like(m_ref, NEG_BIG)
    l_ref[...] = jnp.zeros_like(l_ref)
    acc_ref[...] = jnp.zeros_like(acc_ref)

    def step(c, width, causal):
        start = pl.multiple_of(c * t_k, t_k)
        for h in range(N_HEADS):
            k = k_ref[0, pl.ds(start, width), head(h)]
            v = v_ref[0, pl.ds(start, width), head(h)]
            s = lax.dot_general(q2[h], k, (((1,), (1,)), ((), ())), preferred_element_type=F32)
            if causal:
                qpos = lax.broadcasted_iota(jnp.int32, (2 * t_q, 1), 0) % t_q
                kpos = lax.broadcasted_iota(jnp.int32, (1, width), 1) - (width - t_k)
                s = jnp.where(kpos <= qpos, s, NEG_BIG)
            m_prev = m_ref[h]
            m_new = jnp.maximum(m_prev, jnp.max(s, axis=-1, keepdims=True))
            alpha = jnp.exp2(m_prev - m_new)
            p = jnp.exp2((s - jnp.concatenate([m_new] * (width // LANE), axis=1)).astype(BF16))
            l_ref[h] = alpha * l_ref[h] + jnp.sum(p.astype(F32), axis=-1, keepdims=True)
            acc_ref[h] = alpha * acc_ref[h] + jnp.dot(p, v, preferred_element_type=F32)
            m_ref[h] = m_new

    def pair(i, carry):
        step(2 * i, 2 * t_k, causal=False)
        return carry

    lax.fori_loop(0, j // 2, pair, 0)

    @pl.when(j % 2 == 1)
    def _():
        step(j - 1, 2 * t_k, causal=True)

    @pl.when(j % 2 == 0)
    def _():
        step(j, t_k, causal=True)

    lam = (jnp.exp(jnp.sum(lq1_ref[...] * lk1_ref[...], axis=-1, keepdims=True))
           - jnp.exp(jnp.sum(lq2_ref[...] * lk2_ref[...], axis=-1, keepdims=True)) + lambda_init)
    for h in range(N_HEADS):
        o = acc_ref[h] / l_ref[h]
        d = o[:t_q] - lam * o[t_q:]
        d = d * lax.rsqrt(jnp.mean(d * d, axis=-1, keepdims=True) + RMS_EPS) * g_ref[...]
        o_ref[0, :, head(h)] = (d * (1.0 - lambda_init)).astype(BF16)


def _attention(q, k, v, lq1, lk1, lq2, lk2, subln_g, lambda_init, cfg):
    tq = cfg.t_q
    width = N_HEADS * V_DIM
    small = lambda bb, j: (0, 0)
    q_spec = pl.BlockSpec((1, tq, width), lambda bb, j: (bb, j, 0))
    kv_spec = pl.BlockSpec((1, cfg.lp, width), lambda bb, j: (bb, 0, 0))
    return pl.pallas_call(
        functools.partial(_attn_kernel, t_q=tq, t_k=cfg.t_k, lambda_init=lambda_init),
        grid=(cfg.batch, cfg.lp // tq),
        in_specs=[pl.BlockSpec((1, HEAD_DIM), small)] * 4
        + [pl.BlockSpec((1, V_DIM), small), q_spec, kv_spec, kv_spec],
        out_specs=q_spec,
        out_shape=jax.ShapeDtypeStruct((cfg.batch, cfg.lp, width), BF16),
        scratch_shapes=[pltpu.VMEM((N_HEADS, 2 * tq, LANE), F32),
                        pltpu.VMEM((N_HEADS, 2 * tq, LANE), F32),
                        pltpu.VMEM((N_HEADS, 2 * tq, V_DIM), F32)],
        compiler_params=_params(("parallel", "arbitrary")),
        name="diff_attention",
    )(lq1, lk1, lq2, lk2, subln_g, q, k, v)


def _outproj_kernel(conv_ref, attn_ref, w_ref, h_ref, g_ref, b_ref, ho_ref, *maybe_hbo_ref, alpha):
    mix = (jnp.dot(conv_ref[...], w_ref[:D_CONV], preferred_element_type=F32)
           + jnp.dot(attn_ref[...], w_ref[D_CONV:], preferred_element_type=F32))
    y = _layer_norm(alpha * h_ref[...] + mix, g_ref[...], b_ref[...])
    ho_ref[...] = y
    for hbo_ref in maybe_hbo_ref:
        hbo_ref[...] = y.astype(BF16)


def _outproj(conv, attn, w_out, h, g, b, cfg, alpha, with_bf16):
    tt = cfg.t_tok
    n_out = 2 if with_bf16 else 1
    row = lambda i: (i, 0)
    const = lambda i: (0, 0)
    return pl.pallas_call(
        functools.partial(_outproj_kernel, alpha=alpha),
        grid=(cfg.tokens // tt,),
        in_specs=[pl.BlockSpec((tt, D_CONV), row),
                  pl.BlockSpec((tt, D_MODEL - D_CONV), row),
                  pl.BlockSpec((D_MODEL, D_MODEL), const),
                  pl.BlockSpec((tt, D_MODEL), row),
                  pl.BlockSpec((1, D_MODEL), const),
                  pl.BlockSpec((1, D_MODEL), const)],
        out_specs=[pl.BlockSpec((tt, D_MODEL), row)] * n_out,
        out_shape=[jax.ShapeDtypeStruct((cfg.tokens, D_MODEL), F32),
                   jax.ShapeDtypeStruct((cfg.tokens, D_MODEL), BF16)][:n_out],
        compiler_params=_params(("parallel",)),
        name="outproj_ln",
    )(conv, attn, w_out, h, g, b)


def _swiglu_contrib(x, wg_ref, wu_ref, wd_ref):
    gate = jnp.dot(x, wg_ref[...], preferred_element_type=F32)
    up = jnp.dot(x, wu_ref[...], preferred_element_type=F32)
    act = gate * (1.0 / (1.0 + jnp.exp(-gate))) * up
    return jnp.dot(act.astype(BF16), wd_ref[...], preferred_element_type=F32)


def _ffn_kernel(x_ref, wg_ref, wu_ref, wd_ref, h_ref, g_ref, b_ref, ho_ref, hbo_ref, *, alpha, t_ff):
    x = x_ref[...]
    acc = None
    for c in range(wg_ref.shape[1] // t_ff):
        cols = pl.ds(c * t_ff, t_ff)
        contrib = _swiglu_contrib(x, wg_ref.at[:, cols], wu_ref.at[:, cols], wd_ref.at[cols, :])
        acc = contrib if acc is None else acc + contrib
    y = _layer_norm(alpha * h_ref[...] + acc, g_ref[...], b_ref[...])
    ho_ref[...] = y
    hbo_ref[...] = y.astype(BF16)


def _ffn(hb, wg, wu, wd, h, g, b, cfg, alpha):
    tt = cfg.t_tok
    row = lambda i: (i, 0)
    const = lambda i: (0, 0)
    resident = lambda shape: pl.BlockSpec(shape, const, pipeline_mode=pl.Buffered(1))
    return pl.pallas_call(
        functools.partial(_ffn_kernel, alpha=alpha, t_ff=cfg.t_ff_dense),
        grid=(cfg.tokens // tt,),
        in_specs=[pl.BlockSpec((tt, D_MODEL), row),
                  resident((D_MODEL, cfg.d_ff_dense)),
                  resident((D_MODEL, cfg.d_ff_dense)),
                  resident((cfg.d_ff_dense, D_MODEL)),
                  pl.BlockSpec((tt, D_MODEL), row),
                  pl.BlockSpec((1, D_MODEL), const),
                  pl.BlockSpec((1, D_MODEL), const)],
        out_specs=[pl.BlockSpec((tt, D_MODEL), row)] * 2,
        out_shape=[jax.ShapeDtypeStruct((cfg.tokens, D_MODEL), F32),
                   jax.ShapeDtypeStruct((cfg.tokens, D_MODEL), BF16)],
        compiler_params=_params(("parallel",)),
        name="dense_ffn_ln",
    )(hb, wg, wu, wd, h, g, b)


INFO_E1, INFO_E2, INFO_W1, INFO_W2, INFO_R1, INFO_R2 = range(6)
ROUTE_ROWS = 8


def _lane_pick(values, lane, index):
    return jnp.sum(jnp.where(lane == index, values, 0.0), axis=-1, keepdims=True)


def _router_kernel(h_ref, wr_ref, info_ref, route_ref, cnt_ref, seen_ref, *, n_experts):
    @pl.when(jnp.logical_and(pl.program_id(0) == 0, pl.program_id(1) == 0))
    def _():
        seen_ref[...] = jnp.zeros_like(seen_ref)

    h = h_ref[0]
    w = wr_ref[...]
    h_hi = h.astype(BF16)
    h_lo = (h - h_hi.astype(F32)).astype(BF16)
    w_hi = w.astype(BF16)
    w_lo = (w - w_hi.astype(F32)).astype(BF16)
    logits = (jnp.dot(h_hi, w_hi, preferred_element_type=F32)
              + jnp.dot(h_lo, w_hi, preferred_element_type=F32)
              + jnp.dot(h_hi, w_lo, preferred_element_type=F32))
    t_route = logits.shape[0]
    lane = lax.broadcasted_iota(jnp.int32, logits.shape, 1).astype(F32)
    logits = jnp.where(lane < n_experts, logits, -jnp.inf)
    v1 = jnp.max(logits, axis=-1, keepdims=True)
    e1 = jnp.min(jnp.where(logits == v1, lane, float(LANE)), axis=-1, keepdims=True)
    rest = jnp.where(lane == e1, -jnp.inf, logits)
    v2 = jnp.max(rest, axis=-1, keepdims=True)
    e2 = jnp.min(jnp.where(rest == v2, lane, float(LANE)), axis=-1, keepdims=True)
    ex = jnp.exp(v2 - v1)
    w1 = 1.0 / (1.0 + ex)
    w2 = ex / (1.0 + ex)

    chosen = jnp.where(lane == e1, 1.0, 0.0) + jnp.where(lane == e2, 1.0, 0.0)
    r = lax.broadcasted_iota(jnp.int32, (t_route, t_route), 0)
    c = lax.broadcasted_iota(jnp.int32, (t_route, t_route), 1)
    earlier = jnp.where(c < r, 1.0, 0.0).astype(BF16)
    before = jnp.dot(earlier, chosen.astype(BF16), preferred_element_type=F32) + seen_ref[...]
    r1 = _lane_pick(before, lane, e1)
    r2 = _lane_pick(before, lane, e2)
    seen_ref[...] += jnp.sum(chosen, axis=0, keepdims=True)
    cnt_ref[...] = seen_ref[...]

    record = jnp.zeros_like(logits)
    for slot, val in ((INFO_E1, e1), (INFO_E2, e2), (INFO_W1, w1), (INFO_W2, w2),
                      (INFO_R1, r1), (INFO_R2, r2)):
        record = jnp.where(lane == slot, val, record)
    info_ref[0] = record
    route_ref[0] = record.T[:ROUTE_ROWS]


def _seq_rows_spec(t_rows, width):
    return pl.BlockSpec((pl.Element(1), pl.Element(t_rows), pl.Element(width)),
                        lambda bb, j: (bb, pl.multiple_of(N_META + j * t_rows, N_META), 0))


def _router(h3, wr_pad, cfg):
    tr = cfg.t_route
    steps = cfg.seq // tr
    return pl.pallas_call(
        functools.partial(_router_kernel, n_experts=cfg.n_experts),
        grid=(cfg.batch, cfg.seq // tr),
        in_specs=[_seq_rows_spec(tr, D_MODEL),
                  pl.BlockSpec((D_MODEL, LANE), lambda bb, j: (0, 0))],
        out_specs=[pl.BlockSpec((1, tr, LANE), lambda bb, j: (bb, j, 0)),
                   pl.BlockSpec((1, ROUTE_ROWS, tr), lambda bb, j: (bb * steps + j, 0, 0)),
                   pl.BlockSpec((1, LANE), lambda bb, j: (0, 0))],
        out_shape=[jax.ShapeDtypeStruct((cfg.batch, cfg.seq, LANE), F32),
                   jax.ShapeDtypeStruct((cfg.batch * steps, ROUTE_ROWS, tr), F32),
                   jax.ShapeDtypeStruct((1, LANE), F32)],
        scratch_shapes=[pltpu.VMEM((1, LANE), F32)],
        compiler_params=_params(("arbitrary", "arbitrary")),
        name="router_top2",
    )(h3, wr_pad)


ZERO_ROWS = 8
ISSUE_UNROLL = 8
STAGES = 3


def _dispatch_kernel(pos_ref, fill_ref, h_ref, xs_ref, zero_ref, tile_ref, tile_sem, sem, *, t_route, n_experts):
    def zero_copy(p, n):
        start = pl.multiple_of(p * n, n)
        return pltpu.make_async_copy(zero_ref.at[pl.ds(0, n)], xs_ref.at[pl.ds(start, n)], sem.at[STAGES, 0])

    def zero_rows(lo, hi, n):
        def start(p, carry):
            zero_copy(p, n).start()
            return carry

        def wait(p, carry):
            zero_copy(p, n).wait()
            return carry

        lax.fori_loop(lo, hi, start, 0)
        lax.fori_loop(lo, hi, wait, 0)

    per_row = pl.num_programs(1)
    step = pl.program_id(0) * per_row + pl.program_id(1)
    last = pl.num_programs(0) * per_row - 1

    def tile_copy(s):
        first_row = pl.multiple_of(N_META + (s % per_row) * t_route, N_META)
        return pltpu.make_async_copy(h_ref.at[s // per_row, pl.ds(first_row, t_route)],
                                     tile_ref.at[s % STAGES], tile_sem.at[s % STAGES])

    def wait_rows(s):
        for slot in range(TOP_K):
            pltpu.make_async_copy(tile_ref.at[s % STAGES], xs_ref.at[pl.ds(0, t_route)],
                                  sem.at[s % STAGES, slot]).wait()

    @pl.when(step == 0)
    def _():
        tile_copy(step).start()
        zero_ref[...] = jnp.zeros_like(zero_ref)
        for e in range(n_experts):
            zero_rows(fill_ref[0, e], fill_ref[1, e], 1)
        zero_rows(fill_ref[0, n_experts], fill_ref[1, n_experts], ZERO_ROWS)

    tile_copy(step).wait()

    @pl.when(step >= STAGES - 1)
    def _():
        wait_rows(step - (STAGES - 1))

    @pl.when(step < last)
    def _():
        tile_copy(step + 1).start()

    cur = step % STAGES

    def issue(r, carry):
        for slot in range(TOP_K):
            pltpu.make_async_copy(tile_ref.at[cur, pl.ds(r, 1)],
                                  xs_ref.at[pl.ds(pos_ref[0, 0, slot * t_route + r], 1)],
                                  sem.at[cur, slot]).start(priority=slot)
        return carry

    lax.fori_loop(0, t_route, issue, 0, unroll=ISSUE_UNROLL)

    @pl.when(step == last)
    def _():
        for back in range(STAGES - 2, -1, -1):
            @pl.when(step >= back)
            def _():
                wait_rows(step - back)


def _dispatch(pos, fill, h3, cfg, rows):
    tr = cfg.t_route
    steps = cfg.seq // tr
    return pl.pallas_call(
        functools.partial(_dispatch_kernel, t_route=tr, n_experts=cfg.n_experts),
        grid=(cfg.batch, steps),
        in_specs=[pl.BlockSpec((1, 1, TOP_K * tr), lambda bb, j: (bb * steps + j, 0, 0),
                               memory_space=pltpu.SMEM),
                  pl.BlockSpec(memory_space=pltpu.SMEM),
                  pl.BlockSpec(memory_space=pl.ANY)],
        out_specs=pl.BlockSpec(memory_space=pl.ANY),
        out_shape=jax.ShapeDtypeStruct((rows, D_MODEL), F32),
        scratch_shapes=[pltpu.VMEM((ZERO_ROWS, D_MODEL), F32),
                        pltpu.VMEM((STAGES, tr, D_MODEL), F32),
                        pltpu.SemaphoreType.DMA((STAGES,)),
                        pltpu.SemaphoreType.DMA((STAGES + 1, TOP_K))],
        compiler_params=_params(("arbitrary", "arbitrary")),
        name="moe_dispatch",
    )(pos, fill, h3)


def _grouped_kernel(te_ref, nv_ref, x_ref, wg_ref, wu_ref, wd_ref, y_ref, xb_ref, *, t_chunk):
    f = pl.program_id(1)

    @pl.when(pl.program_id(0) < nv_ref[0])
    def _():
        @pl.when(f == 0)
        def _():
            xb_ref[...] = x_ref[...].astype(BF16)

        x = xb_ref[...]
        acc = None
        for c in range(wg_ref.shape[2] // t_chunk):
            cols = pl.ds(c * t_chunk, t_chunk)
            contrib = _swiglu_contrib(x, wg_ref.at[0, :, cols], wu_ref.at[0, :, cols], wd_ref.at[0, cols, :])
            acc = contrib if acc is None else acc + contrib

        @pl.when(f == 0)
        def _():
            y_ref[...] = acc

        @pl.when(f > 0)
        def _():
            y_ref[...] += acc

    @pl.when(pl.program_id(0) >= nv_ref[0])
    def _():
        y_ref[...] = jnp.zeros_like(y_ref)


def _grouped_ffn(tile_expert, n_valid, xs, wg, wu, wd, cfg):
    tg, tf = cfg.t_group, cfg.t_ff_expert
    n_f = cfg.d_ff_expert // tf
    rows = xs.shape[0]

    def live(i, nv):
        return jnp.minimum(i, nv[0] - 1)

    def f_idx(i, f, nv):
        return jnp.where(i < nv[0], f, n_f - 1)

    grid_spec = pltpu.PrefetchScalarGridSpec(
        num_scalar_prefetch=2,
        grid=(rows // tg, n_f),
        in_specs=[pl.BlockSpec((tg, D_MODEL), lambda i, f, te, nv: (live(i, nv), 0)),
                  pl.BlockSpec((1, D_MODEL, tf), lambda i, f, te, nv: (te[i], 0, f_idx(i, f, nv))),
                  pl.BlockSpec((1, D_MODEL, tf), lambda i, f, te, nv: (te[i], 0, f_idx(i, f, nv))),
                  pl.BlockSpec((1, tf, D_MODEL), lambda i, f, te, nv: (te[i], f_idx(i, f, nv), 0))],
        out_specs=pl.BlockSpec((tg, D_MODEL), lambda i, f, te, nv: (i, 0)),
        scratch_shapes=[pltpu.VMEM((tg, D_MODEL), BF16)],
    )
    return pl.pallas_call(
        functools.partial(_grouped_kernel, t_chunk=cfg.t_ff_dense),
        grid_spec=grid_spec,
        out_shape=jax.ShapeDtypeStruct((rows, D_MODEL), F32),
        compiler_params=_params(("arbitrary", "arbitrary")),
        name="moe_grouped_ffn",
    )(tile_expert, n_valid, xs, wg, wu, wd)


def _combine_kernel(pos_ref, next_pos_ref, info_ref, h_ref, g_ref, b_ref, ys_ref, out_ref, rows_ref, sem,
                    *, alpha, t_route):
    step = pl.program_id(0) * pl.num_programs(1) + pl.program_id(1)
    last = pl.num_programs(0) * pl.num_programs(1) - 1
    buf = step % 2

    def gather(tile_pos_ref, into):
        def issue(r, carry):
            for slot in range(TOP_K):
                pltpu.make_async_copy(ys_ref.at[pl.ds(tile_pos_ref[0, 0, slot * t_route + r], 1)],
                                      rows_ref.at[into, slot, pl.ds(r, 1)],
                                      sem.at[into, slot]).start(priority=slot)
            return carry

        lax.fori_loop(0, t_route, issue, 0, unroll=ISSUE_UNROLL)

    @pl.when(step == 0)
    def _():
        gather(pos_ref, 0)

    @pl.when(step < last)
    def _():
        gather(next_pos_ref, 1 - buf)

    for slot in range(TOP_K):
        pltpu.make_async_copy(ys_ref.at[pl.ds(0, t_route)], rows_ref.at[buf, slot], sem.at[buf, slot]).wait()

    info = info_ref[0]
    lane = lax.broadcasted_iota(jnp.int32, info.shape, 1)
    ffn = (_lane_pick(info, lane, INFO_W1) * rows_ref[buf, 0]
           + _lane_pick(info, lane, INFO_W2) * rows_ref[buf, 1])
    out_ref[0] = _layer_norm(alpha * h_ref[0] + ffn, g_ref[...], b_ref[...])


def _combine(pos, info, h3, g, b, ys, cfg, alpha):
    tr = cfg.t_route
    steps = cfg.seq // tr
    n_tiles = cfg.batch * steps
    const = lambda bb, j: (0, 0)

    def pos_spec(ahead):
        return pl.BlockSpec((1, 1, TOP_K * tr),
                            lambda bb, j: (jnp.minimum(bb * steps + j + ahead, n_tiles - 1), 0, 0),
                            memory_space=pltpu.SMEM)

    return pl.pallas_call(
        functools.partial(_combine_kernel, alpha=alpha, t_route=tr),
        grid=(cfg.batch, steps),
        in_specs=[pos_spec(0), pos_spec(1),
                  pl.BlockSpec((1, tr, LANE), lambda bb, j: (bb, j, 0)),
                  _seq_rows_spec(tr, D_MODEL),
                  pl.BlockSpec((1, D_MODEL), const),
                  pl.BlockSpec((1, D_MODEL), const),
                  pl.BlockSpec(memory_space=pl.ANY)],
        out_specs=pl.BlockSpec((1, tr, D_MODEL), lambda bb, j: (bb, j, 0)),
        out_shape=jax.ShapeDtypeStruct((cfg.batch, cfg.seq, D_MODEL), F32),
        scratch_shapes=[pltpu.VMEM((2, TOP_K, tr, D_MODEL), F32), pltpu.SemaphoreType.DMA((2, TOP_K))],
        compiler_params=_params(("arbitrary", "arbitrary")),
        name="moe_combine_ln",
    )(pos, pos, info, h3, g, b, ys)


def _routed_moe(h3, w_router, wg, wu, wd, g, b, cfg, alpha):
    n_e, tg, tr = cfg.n_experts, cfg.t_group, cfg.t_route
    n_tok = cfg.batch * cfg.seq
    rows = TOP_K * n_tok + n_e * tg
    wr_pad = jnp.pad(w_router, ((0, 0), (0, LANE - n_e)))
    info, route, seen = _router(h3, wr_pad, cfg)

    as_int = lambda row: route[:, row, :].astype(jnp.int32)
    counts = seen[0, :n_e].astype(jnp.int32)
    padded = (counts + tg - 1) // tg * tg
    ends = jnp.cumsum(padded)
    starts = ends - padded
    start_of = lambda e: jnp.sum(jnp.where(e[..., None] == jnp.arange(n_e), starts, 0), axis=-1)
    pos = jnp.stack([start_of(as_int(INFO_E1)) + as_int(INFO_R1),
                     start_of(as_int(INFO_E2)) + as_int(INFO_R2)], axis=1)
    pos = pos.reshape(n_tok // tr, 1, TOP_K * tr)
    fill = jnp.stack([jnp.append(starts + counts, ends[-1] // ZERO_ROWS),
                      jnp.append(ends, rows // ZERO_ROWS)])
    n_valid = (ends[-1] // tg).reshape(1)
    tile_start = jnp.arange(rows // tg, dtype=jnp.int32) * tg
    tile_expert = jnp.sum(tile_start[:, None] >= ends[None, :], axis=-1).astype(jnp.int32)
    tile_expert = tile_expert[jnp.minimum(jnp.arange(rows // tg), n_valid[0] - 1)]

    xs = _dispatch(pos, fill, h3, cfg, rows)
    ys = _grouped_ffn(tile_expert, n_valid, xs, wg, wu, wd, cfg)
    return _combine(pos, info, h3, g, b, ys, cfg, alpha)


def _rope_tables(lp):
    half = HEAD_DIM // 2
    inv = 1.0 / (ROPE_THETA ** (jnp.arange(0, HEAD_DIM, 2, dtype=F32) / HEAD_DIM))
    pos = jnp.arange(lp, dtype=F32)
    ang = pos[:, None] * inv[None, :]
    ang = jnp.tile(ang, (1, LANE // half))
    sign = jnp.where((jnp.arange(LANE) % HEAD_DIM) < half, -1.0, 1.0).astype(F32)
    return jnp.cos(ang), jnp.sin(ang) * sign[None, :]


def _forward(cfg, x, meta_tokens, ln_emb_g, ln_emb_b, w_in, conv_w, lambda_q1, lambda_k1,
             lambda_q2, lambda_k2, subln_g, w_out, ln_mix_g, ln_mix_b, ln_ffn_g, ln_ffn_b,
             w_gate_dense, w_up_dense, w_down_dense, w_router, w_gate_moe, w_up_moe, w_down_moe):
    alpha = (2 * cfg.depth) ** 0.25
    tokens = cfg.tokens
    vec = lambda a: a.reshape(1, -1)
    cos, sin = _rope_tables(cfg.lp)
    h, hb = _embed(x, meta_tokens, vec(ln_emb_g), vec(ln_emb_b), cfg)
    h = h.reshape(tokens, D_MODEL)
    for layer in range(cfg.depth):
        lambda_init = 0.8 - 0.6 * math.exp(-0.3 * layer)
        conv, q, k, v = _inproj(hb.reshape(cfg.batch, cfg.lp, D_MODEL), w_in[layer].astype(BF16),
                                     conv_w[layer], cos, sin, cfg)
        attn = _attention(q, k, v, vec(lambda_q1[layer]), vec(lambda_k1[layer]),
                          vec(lambda_q2[layer]), vec(lambda_k2[layer]), vec(subln_g[layer]),
                          lambda_init, cfg)
        dense = layer % 2 == 0
        h, *hb = _outproj(conv.reshape(tokens, D_CONV), attn.reshape(tokens, D_MODEL - D_CONV),
                          w_out[layer].astype(BF16), h, vec(ln_mix_g[layer]), vec(ln_mix_b[layer]),
                          cfg, alpha, with_bf16=dense)
        idx = layer // 2
        g, b = vec(ln_ffn_g[layer]), vec(ln_ffn_b[layer])
        if dense:
            hb, = hb
            h, hb = _ffn(hb, w_gate_dense[idx].astype(BF16), w_up_dense[idx].astype(BF16),
                         w_down_dense[idx].astype(BF16), h, g, b, cfg, alpha)
        else:
            assert layer == cfg.depth - 1
            return _routed_moe(h.reshape(cfg.batch, cfg.lp, D_MODEL), w_router[idx],
                               w_gate_moe[idx].astype(BF16), w_up_moe[idx].astype(BF16),
                               w_down_moe[idx].astype(BF16), g, b, cfg, alpha)
    return h.reshape(cfg.batch, cfg.lp, D_MODEL)[:, N_META:N_META + cfg.seq]


_CFG = Cfg(batch=8, seq=4096, depth=2, d_ff_dense=2816, d_ff_expert=3584, n_experts=8,
           t_embed=128, t_proj=1056, t_q=384, t_k=384, t_tok=1024, t_ff_dense=256, t_ff_expert=1792,
           t_route=1024, t_group=1024)


def kernel(x, meta_tokens, ln_emb_g, ln_emb_b, w_in, conv_w, lambda_q1, lambda_k1, lambda_q2, lambda_k2, subln_g, w_out, ln_mix_g, ln_mix_b, ln_ffn_g, ln_ffn_b, w_gate_dense, w_up_dense, w_down_dense, w_router, w_gate_moe, w_up_moe, w_down_moe):
    return _forward(_CFG, x, meta_tokens, ln_emb_g, ln_emb_b, w_in, conv_w, lambda_q1, lambda_k1,
                    lambda_q2, lambda_k2, subln_g, w_out, ln_mix_g, ln_mix_b, ln_ffn_g, ln_ffn_b,
                    w_gate_dense, w_up_dense, w_down_dense, w_router, w_gate_moe, w_up_moe, w_down_moe)
```

```python
import dataclasses
import functools
import math

import jax
import jax.numpy as jnp
from jax import lax
from jax.experimental import pallas as pl
from jax.experimental.pallas import tpu as pltpu

F32 = jnp.float32
BF16 = jnp.bfloat16

LANE = 128
D_MODEL = 1024
N_META = 16
PAD = LANE - N_META
D_CONV = D_MODEL // 2
CONV_WIDTH = 3
HEAD_DIM = 64
N_HEADS = 4
V_DIM = 2 * HEAD_DIM
QK_W = N_HEADS * 2 * HEAD_DIM
GROUP_W = 512
ROPE_THETA = 10000.0
TOP_K = 2
LN_EPS = 1e-5
RMS_EPS = 1e-5
NEG_BIG = -1e30
VMEM_LIMIT = 56 * 1024 * 1024


@dataclasses.dataclass(frozen=True)
class Cfg:
    batch: int
    seq: int
    depth: int
    d_ff_dense: int
    d_ff_expert: int
    n_experts: int
    t_embed: int
    n_embed: int
    t_proj: int
    t_q: int
    t_k: int
    t_tok: int
    t_ff_dense: int
    t_ff_expert: int
    t_route: int
    t_group: int

    @property
    def lp(self):
        return N_META + self.seq + PAD

    @property
    def tokens(self):
        return self.batch * self.lp


def _params(semantics):
    return pltpu.CompilerParams(dimension_semantics=semantics, vmem_limit_bytes=VMEM_LIMIT)


def _layer_norm(y, g, b):
    mu = jnp.mean(y, axis=-1, keepdims=True)
    yc = y - mu
    var = jnp.mean(yc * yc, axis=-1, keepdims=True)
    return yc * lax.rsqrt(var + LN_EPS) * g + b


def _embed_kernel(*refs, seq, n_blocks):
    x_refs, (meta_ref, g_ref, b_ref, h_ref) = refs[:n_blocks + 1], refs[n_blocks + 1:]
    j = pl.program_id(1)
    te = x_refs[1].shape[1]
    head = jnp.where(j == 0, meta_ref[...], x_refs[0][0, te - N_META:])
    rows = jnp.concatenate([head] + [r[0] for r in x_refs[1:n_blocks]] + [x_refs[n_blocks][0, :te - N_META]],
                           axis=0)
    rid = j * (n_blocks * te) + lax.broadcasted_iota(jnp.int32, (n_blocks * te, 1), 0)
    rows = jnp.where(rid < N_META + seq, rows, 0.0)
    h_ref[0] = _layer_norm(rows, g_ref[...], b_ref[...])


def _embed(x, meta_tokens, g, b, cfg):
    te, n = cfg.t_embed, cfg.n_embed
    last = cfg.seq // te - 1

    def x_spec(i):
        return pl.BlockSpec((1, te, D_MODEL), lambda bb, j: (bb, jnp.clip(n * j + i - 1, 0, last), 0))

    const = lambda bb, j: (0, 0)
    return pl.pallas_call(
        functools.partial(_embed_kernel, seq=cfg.seq, n_blocks=n),
        grid=(cfg.batch, cfg.lp // (n * te)),
        in_specs=[x_spec(i) for i in range(n + 1)]
        + [pl.BlockSpec((N_META, D_MODEL), const),
           pl.BlockSpec((1, D_MODEL), const),
           pl.BlockSpec((1, D_MODEL), const)],
        out_specs=pl.BlockSpec((1, n * te, D_MODEL), lambda bb, j: (bb, j, 0)),
        out_shape=jax.ShapeDtypeStruct((cfg.batch, cfg.lp, D_MODEL), F32),
        compiler_params=_params(("parallel", "arbitrary")),
        name="embed_ln",
    )(*([x] * (n + 1)), meta_tokens, g, b)


def _inproj_kernel(h_ref, w_ref, cw_ref, cos_ref, sin_ref,
                   conv_ref, q_ref, k_ref, v_ref, carry_ref, *, t_proj):
    j = pl.program_id(1)
    x = h_ref[0].astype(BF16)

    def proj(g):
        return jnp.dot(x, w_ref[:, g * GROUP_W:(g + 1) * GROUP_W], preferred_element_type=F32)

    rid = lax.broadcasted_iota(jnp.int32, (t_proj, 1), 0)
    gated = proj(1) * proj(2)

    @pl.when(j == 0)
    def _():
        carry_ref[...] = jnp.zeros_like(carry_ref)

    prev = carry_ref[...]
    back1 = jnp.where(rid == 0, prev[7:8], pltpu.roll(gated, 1, 0))
    back2 = jnp.where(rid == 0, prev[6:7], jnp.where(rid == 1, prev[7:8], pltpu.roll(gated, 2, 0)))
    carry_ref[...] = gated[t_proj - 8:]
    cw = cw_ref[...]
    conv = cw[0:1] * back2 + cw[1:2] * back1 + cw[2:3] * gated
    conv_ref[0] = (proj(0) * conv).astype(BF16)

    cos = cos_ref[...]
    sin = sin_ref[...]
    lane = lax.broadcasted_iota(jnp.int32, (1, LANE), 1)
    low_half = (lane % HEAD_DIM) < HEAD_DIM // 2

    def rope(z):
        partner = jnp.where(low_half, pltpu.roll(z, LANE - HEAD_DIM // 2, 1),
                            pltpu.roll(z, HEAD_DIM // 2, 1))
        return z * cos + partner * sin

    zq = proj(3)
    zk = proj(4)
    scale = HEAD_DIM ** -0.5 * math.log2(math.e)
    for h in range(N_HEADS):
        sl = slice(h * LANE, (h + 1) * LANE)
        q_ref[0, :, sl] = (rope(zq[:, sl]) * scale).astype(BF16)
        k_ref[0, :, sl] = rope(zk[:, sl]).astype(BF16)
    v_ref[0] = proj(5).astype(BF16)


def _inproj(h, w_in, conv_w, cos, sin, cfg):
    tp = cfg.t_proj
    row = lambda bb, j: (bb, j, 0)
    const = lambda bb, j: (0, 0)
    out_spec = pl.BlockSpec((1, tp, GROUP_W), row)
    out_shape = jax.ShapeDtypeStruct((cfg.batch, cfg.lp, GROUP_W), BF16)
    return pl.pallas_call(
        functools.partial(_inproj_kernel, t_proj=tp),
        grid=(cfg.batch, cfg.lp // tp),
        in_specs=[pl.BlockSpec((1, tp, D_MODEL), row),
                  pl.BlockSpec((D_MODEL, 6 * GROUP_W), const),
                  pl.BlockSpec((CONV_WIDTH, D_CONV), const),
                  pl.BlockSpec((tp, LANE), lambda bb, j: (j, 0)),
                  pl.BlockSpec((tp, LANE), lambda bb, j: (j, 0))],
        out_specs=[out_spec] * 4,
        out_shape=[out_shape] * 4,
        scratch_shapes=[pltpu.VMEM((8, D_CONV), F32)],
        compiler_params=_params(("parallel", "arbitrary")),
        name="inproj_conv_rope",
    )(h, w_in, conv_w, cos, sin)


def _attn_kernel(lq1_ref, lk1_ref, lq2_ref, lk2_ref, g_ref, q_ref, k_ref, v_ref,
                 o_ref, m_ref, l_ref, acc_ref, *, t_q, t_k, lambda_init):
    j = pl.program_id(1)
    head = lambda h: slice(h * LANE, (h + 1) * LANE)
    first_map = lax.broadcasted_iota(jnp.int32, (1, LANE), 1) < HEAD_DIM
    zero = jnp.zeros((), BF16)
    q2 = []
    for h in range(N_HEADS):
        q = q_ref[0, :, head(h)]
        q2.append(jnp.concatenate([jnp.where(first_map, q, zero), jnp.where(first_map, zero, q)], axis=0))
    m_ref[...] = jnp.full_like(m_ref, NEG_BIG)
    l_ref[...] = jnp.zeros_like(l_ref)
    acc_ref[...] = jnp.zeros_like(acc_ref)

    def step(c, width, causal):
        start = pl.multiple_of(c * t_k, t_k)
        for h in range(N_HEADS):
            k = k_ref[0, pl.ds(start, width), head(h)]
            v = v_ref[0, pl.ds(start, width), head(h)]
            s = lax.dot_general(q2[h], k, (((1,), (1,)), ((), ())), preferred_element_type=F32)
            if causal:
                qpos = lax.broadcasted_iota(jnp.int32, (2 * t_q, 1), 0) % t_q
                kpos = lax.broadcasted_iota(jnp.int32, (1, width), 1) - (width - t_k)
                s = jnp.where(kpos <= qpos, s, NEG_BIG)
            m_prev = m_ref[h]
            m_new = jnp.maximum(m_prev, jnp.max(s, axis=-1, keepdims=True))
            alpha = jnp.exp2(m_prev - m_new)
            p = jnp.exp2((s - jnp.concatenate([m_new] * (width // LANE), axis=1)).astype(BF16))
            l_ref[h] = alpha * l_ref[h] + jnp.sum(p.astype(F32), axis=-1, keepdims=True)
            acc_ref[h] = alpha * acc_ref[h] + jnp.dot(p, v, preferred_element_type=F32)
            m_ref[h] = m_new

    def pair(i, carry):
        step(2 * i, 2 * t_k, causal=False)
        return carry

    lax.fori_loop(0, j // 2, pair, 0)

    @pl.when(j % 2 == 1)
    def _():
        step(j - 1, 2 * t_k, causal=True)

    @pl.when(j % 2 == 0)
    def _():
        step(j, t_k, causal=True)

    lam = (jnp.exp(jnp.sum(lq1_ref[...] * lk1_ref[...], axis=-1, keepdims=True))
           - jnp.exp(jnp.sum(lq2_ref[...] * lk2_ref[...], axis=-1, keepdims=True)) + lambda_init)
    for h in range(N_HEADS):
        o = acc_ref[h] / l_ref[h]
        d = o[:t_q] - lam * o[t_q:]
        d = d * lax.rsqrt(jnp.mean(d * d, axis=-1, keepdims=True) + RMS_EPS) * g_ref[...]
        o_ref[0, :, head(h)] = (d * (1.0 - lambda_init)).astype(BF16)


def _attention(q, k, v, lq1, lk1, lq2, lk2, subln_g, lambda_init, cfg):
    tq = cfg.t_q
    width = N_HEADS * V_DIM
    small = lambda bb, j: (0, 0)
    q_spec = pl.BlockSpec((1, tq, width), lambda bb, j: (bb, j, 0))
    kv_spec = pl.BlockSpec((1, cfg.lp, width), lambda bb, j: (bb, 0, 0))
    return pl.pallas_call(
        functools.partial(_attn_kernel, t_q=tq, t_k=cfg.t_k, lambda_init=lambda_init),
        grid=(cfg.batch, cfg.lp // tq),
        in_specs=[pl.BlockSpec((1, HEAD_DIM), small)] * 4
        + [pl.BlockSpec((1, V_DIM), small), q_spec, kv_spec, kv_spec],
        out_specs=q_spec,
        out_shape=jax.ShapeDtypeStruct((cfg.batch, cfg.lp, width), BF16),
        scratch_shapes=[pltpu.VMEM((N_HEADS, 2 * tq, LANE), F32),
                        pltpu.VMEM((N_HEADS, 2 * tq, LANE), F32),
                        pltpu.VMEM((N_HEADS, 2 * tq, V_DIM), F32)],
        compiler_params=_params(("parallel", "arbitrary")),
        name="diff_attention",
    )(lq1, lk1, lq2, lk2, subln_g, q, k, v)


def _outproj_kernel(conv_ref, attn_ref, w_ref, h_ref, g_ref, b_ref, ho_ref, *, alpha):
    mix = (jnp.dot(conv_ref[...], w_ref[:D_CONV], preferred_element_type=F32)
           + jnp.dot(attn_ref[...], w_ref[D_CONV:], preferred_element_type=F32))
    ho_ref[...] = _layer_norm(alpha * h_ref[...] + mix, g_ref[...], b_ref[...])


def _outproj(conv, attn, w_out, h, g, b, cfg, alpha):
    tt = cfg.t_tok
    row = lambda i: (i, 0)
    const = lambda i: (0, 0)
    return pl.pallas_call(
        functools.partial(_outproj_kernel, alpha=alpha),
        grid=(cfg.tokens // tt,),
        in_specs=[pl.BlockSpec((tt, D_CONV), row),
                  pl.BlockSpec((tt, D_MODEL - D_CONV), row),
                  pl.BlockSpec((D_MODEL, D_MODEL), const),
                  pl.BlockSpec((tt, D_MODEL), row),
                  pl.BlockSpec((1, D_MODEL), const),
                  pl.BlockSpec((1, D_MODEL), const)],
        out_specs=pl.BlockSpec((tt, D_MODEL), row),
        out_shape=jax.ShapeDtypeStruct((cfg.tokens, D_MODEL), F32),
        compiler_params=_params(("parallel",)),
        name="outproj_ln",
    )(conv, attn, w_out, h, g, b)


def _swiglu_contrib(x, wg_ref, wu_ref, wd_ref):
    gate = jnp.dot(x, wg_ref[...], preferred_element_type=F32)
    up = jnp.dot(x, wu_ref[...], preferred_element_type=F32)
    act = gate * (1.0 / (1.0 + jnp.exp(-gate))) * up
    return jnp.dot(act.astype(BF16), wd_ref[...], preferred_element_type=F32)


def _ffn_kernel(wg_ref, wu_ref, wd_ref, h_ref, g_ref, b_ref, ho_ref, *, alpha, t_ff):
    h = h_ref[...]
    x = h.astype(BF16)
    acc = None
    for c in range(wg_ref.shape[1] // t_ff):
        cols = pl.ds(c * t_ff, t_ff)
        contrib = _swiglu_contrib(x, wg_ref.at[:, cols], wu_ref.at[:, cols], wd_ref.at[cols, :])
        acc = contrib if acc is None else acc + contrib
    ho_ref[...] = _layer_norm(alpha * h + acc, g_ref[...], b_ref[...])


def _ffn(wg, wu, wd, h, g, b, cfg, alpha):
    tt = cfg.t_tok
    row = lambda i: (i, 0)
    const = lambda i: (0, 0)
    resident = lambda shape: pl.BlockSpec(shape, const, pipeline_mode=pl.Buffered(1))
    return pl.pallas_call(
        functools.partial(_ffn_kernel, alpha=alpha, t_ff=cfg.t_ff_dense),
        grid=(cfg.tokens // tt,),
        in_specs=[resident((D_MODEL, cfg.d_ff_dense)),
                  resident((D_MODEL, cfg.d_ff_dense)),
                  resident((cfg.d_ff_dense, D_MODEL)),
                  pl.BlockSpec((tt, D_MODEL), row),
                  pl.BlockSpec((1, D_MODEL), const),
                  pl.BlockSpec((1, D_MODEL), const)],
        out_specs=pl.BlockSpec((tt, D_MODEL), row),
        out_shape=jax.ShapeDtypeStruct((cfg.tokens, D_MODEL), F32),
        compiler_params=_params(("parallel",)),
        name="dense_ffn_ln",
    )(wg, wu, wd, h, g, b)


INFO_E1, INFO_E2, INFO_W1, INFO_W2, INFO_R1, INFO_R2 = range(6)
ROUTE_ROWS = 8


def _lane_pick(values, lane, index):
    return jnp.sum(jnp.where(lane == index, values, 0.0), axis=-1, keepdims=True)


def _router_kernel(h_ref, wr_ref, info_ref, route_ref, cnt_ref, seen_ref, *, n_experts):
    @pl.when(jnp.logical_and(pl.program_id(0) == 0, pl.program_id(1) == 0))
    def _():
        seen_ref[...] = jnp.zeros_like(seen_ref)

    h = h_ref[0]
    w = wr_ref[...]
    h_hi = h.astype(BF16)
    h_lo = (h - h_hi.astype(F32)).astype(BF16)
    w_hi = w.astype(BF16)
    w_lo = (w - w_hi.astype(F32)).astype(BF16)
    logits = (jnp.dot(h_hi, w_hi, preferred_element_type=F32)
              + jnp.dot(h_lo, w_hi, preferred_element_type=F32)
              + jnp.dot(h_hi, w_lo, preferred_element_type=F32))
    t_route = logits.shape[0]
    lane = lax.broadcasted_iota(jnp.int32, logits.shape, 1).astype(F32)
    logits = jnp.where(lane < n_experts, logits, -jnp.inf)
    v1 = jnp.max(logits, axis=-1, keepdims=True)
    e1 = jnp.min(jnp.where(logits == v1, lane, float(LANE)), axis=-1, keepdims=True)
    rest = jnp.where(lane == e1, -jnp.inf, logits)
    v2 = jnp.max(rest, axis=-1, keepdims=True)
    e2 = jnp.min(jnp.where(rest == v2, lane, float(LANE)), axis=-1, keepdims=True)
    ex = jnp.exp(v2 - v1)
    w1 = 1.0 / (1.0 + ex)
    w2 = ex / (1.0 + ex)

    chosen = jnp.where(lane == e1, 1.0, 0.0) + jnp.where(lane == e2, 1.0, 0.0)
    r = lax.broadcasted_iota(jnp.int32, (t_route, t_route), 0)
    c = lax.broadcasted_iota(jnp.int32, (t_route, t_route), 1)
    earlier = jnp.where(c < r, 1.0, 0.0).astype(BF16)
    before = jnp.dot(earlier, chosen.astype(BF16), preferred_element_type=F32) + seen_ref[...]
    r1 = _lane_pick(before, lane, e1)
    r2 = _lane_pick(before, lane, e2)
    seen_ref[...] += jnp.sum(chosen, axis=0, keepdims=True)
    cnt_ref[...] = seen_ref[...]

    record = jnp.zeros_like(logits)
    for slot, val in ((INFO_E1, e1), (INFO_E2, e2), (INFO_W1, w1), (INFO_W2, w2),
                      (INFO_R1, r1), (INFO_R2, r2)):
        record = jnp.where(lane == slot, val, record)
    info_ref[0] = record
    route_ref[0] = record.T[:ROUTE_ROWS]


def _seq_rows_spec(t_rows, width):
    return pl.BlockSpec((pl.Element(1), pl.Element(t_rows), pl.Element(width)),
                        lambda bb, j: (bb, pl.multiple_of(N_META + j * t_rows, N_META), 0))


def _router(h3, wr_pad, cfg):
    tr = cfg.t_route
    steps = cfg.seq // tr
    return pl.pallas_call(
        functools.partial(_router_kernel, n_experts=cfg.n_experts),
        grid=(cfg.batch, cfg.seq // tr),
        in_specs=[_seq_rows_spec(tr, D_MODEL),
                  pl.BlockSpec((D_MODEL, LANE), lambda bb, j: (0, 0))],
        out_specs=[pl.BlockSpec((1, tr, LANE), lambda bb, j: (bb, j, 0)),
                   pl.BlockSpec((1, ROUTE_ROWS, tr), lambda bb, j: (bb * steps + j, 0, 0)),
                   pl.BlockSpec((1, LANE), lambda bb, j: (0, 0))],
        out_shape=[jax.ShapeDtypeStruct((cfg.batch, cfg.seq, LANE), F32),
                   jax.ShapeDtypeStruct((cfg.batch * steps, ROUTE_ROWS, tr), F32),
                   jax.ShapeDtypeStruct((1, LANE), F32)],
        scratch_shapes=[pltpu.VMEM((1, LANE), F32)],
        compiler_params=_params(("arbitrary", "arbitrary")),
        name="router_top2",
    )(h3, wr_pad)


ZERO_ROWS = 8
ISSUE_UNROLL = 8


def _dispatch_kernel(pos_ref, fill_ref, h_ref, xs_ref, zero_ref, sem, *, t_route, n_experts):
    def zero_copy(p, n):
        start = pl.multiple_of(p * n, n)
        return pltpu.make_async_copy(zero_ref.at[pl.ds(0, n)], xs_ref.at[pl.ds(start, n)], sem.at[2])

    def zero_rows(lo, hi, n):
        def start(p, carry):
            zero_copy(p, n).start()
            return carry

        def wait(p, carry):
            zero_copy(p, n).wait()
            return carry

        lax.fori_loop(lo, hi, start, 0)
        lax.fori_loop(lo, hi, wait, 0)

    @pl.when(jnp.logical_and(pl.program_id(0) == 0, pl.program_id(1) == 0))
    def _():
        zero_ref[...] = jnp.zeros_like(zero_ref)
        for e in range(n_experts):
            zero_rows(fill_ref[0, e], fill_ref[1, e], 1)
        zero_rows(fill_ref[0, n_experts], fill_ref[1, n_experts], ZERO_ROWS)

    def issue(r, carry):
        for slot in range(TOP_K):
            pltpu.make_async_copy(h_ref.at[0, pl.ds(r, 1)],
                                  xs_ref.at[pl.ds(pos_ref[0, 0, slot * t_route + r], 1)],
                                  sem.at[slot]).start(priority=slot)
        return carry

    lax.fori_loop(0, t_route, issue, 0, unroll=ISSUE_UNROLL)
    for slot in range(TOP_K):
        pltpu.make_async_copy(h_ref.at[0], xs_ref.at[pl.ds(0, t_route)], sem.at[slot]).wait()


def _dispatch(pos, fill, h3, cfg, rows):
    tr = cfg.t_route
    steps = cfg.seq // tr
    return pl.pallas_call(
        functools.partial(_dispatch_kernel, t_route=tr, n_experts=cfg.n_experts),
        grid=(cfg.batch, steps),
        in_specs=[pl.BlockSpec((1, 1, TOP_K * tr), lambda bb, j: (bb * steps + j, 0, 0),
                               memory_space=pltpu.SMEM),
                  pl.BlockSpec(memory_space=pltpu.SMEM),
                  _seq_rows_spec(tr, D_MODEL)],
        out_specs=pl.BlockSpec(memory_space=pl.ANY),
        out_shape=jax.ShapeDtypeStruct((rows, D_MODEL), F32),
        scratch_shapes=[pltpu.VMEM((ZERO_ROWS, D_MODEL), F32), pltpu.SemaphoreType.DMA((3,))],
        compiler_params=_params(("arbitrary", "arbitrary")),
        name="moe_dispatch",
    )(pos, fill, h3)


def _grouped_kernel(te_ref, nv_ref, x_ref, wg_ref, wu_ref, wd_ref, y_ref, xb_ref, *, t_chunk):
    f = pl.program_id(1)

    @pl.when(pl.program_id(0) < nv_ref[0])
    def _():
        @pl.when(f == 0)
        def _():
            xb_ref[...] = x_ref[...].astype(BF16)

        x = xb_ref[...]
        acc = None
        for c in range(wg_ref.shape[2] // t_chunk):
            cols = pl.ds(c * t_chunk, t_chunk)
            contrib = _swiglu_contrib(x, wg_ref.at[0, :, cols], wu_ref.at[0, :, cols], wd_ref.at[0, cols, :])
            acc = contrib if acc is None else acc + contrib

        @pl.when(f == 0)
        def _():
            y_ref[...] = acc

        @pl.when(f > 0)
        def _():
            y_ref[...] += acc

    @pl.when(pl.program_id(0) >= nv_ref[0])
    def _():
        y_ref[...] = jnp.zeros_like(y_ref)


def _grouped_ffn(tile_expert, n_valid, xs, wg, wu, wd, cfg):
    tg, tf = cfg.t_group, cfg.t_ff_expert
    n_f = cfg.d_ff_expert // tf
    rows = xs.shape[0]

    def live(i, nv):
        return jnp.minimum(i, nv[0] - 1)

    def f_idx(i, f, nv):
        return jnp.where(i < nv[0], f, n_f - 1)

    grid_spec = pltpu.PrefetchScalarGridSpec(
        num_scalar_prefetch=2,
        grid=(rows // tg, n_f),
        in_specs=[pl.BlockSpec((tg, D_MODEL), lambda i, f, te, nv: (live(i, nv), 0)),
                  pl.BlockSpec((1, D_MODEL, tf), lambda i, f, te, nv: (te[i], 0, f_idx(i, f, nv))),
                  pl.BlockSpec((1, D_MODEL, tf), lambda i, f, te, nv: (te[i], 0, f_idx(i, f, nv))),
                  pl.BlockSpec((1, tf, D_MODEL), lambda i, f, te, nv: (te[i], f_idx(i, f, nv), 0))],
        out_specs=pl.BlockSpec((tg, D_MODEL), lambda i, f, te, nv: (i, 0)),
        scratch_shapes=[pltpu.VMEM((tg, D_MODEL), BF16)],
    )
    return pl.pallas_call(
        functools.partial(_grouped_kernel, t_chunk=cfg.t_ff_dense),
        grid_spec=grid_spec,
        out_shape=jax.ShapeDtypeStruct((rows, D_MODEL), F32),
        compiler_params=_params(("arbitrary", "arbitrary")),
        name="moe_grouped_ffn",
    )(tile_expert, n_valid, xs, wg, wu, wd)


def _combine_kernel(pos_ref, next_pos_ref, info_ref, h_ref, g_ref, b_ref, ys_ref, out_ref, rows_ref, sem,
                    *, alpha, t_route):
    step = pl.program_id(0) * pl.num_programs(1) + pl.program_id(1)
    last = pl.num_programs(0) * pl.num_programs(1) - 1
    buf = step % 2

    def gather(tile_pos_ref, into):
        def issue(r, carry):
            for slot in range(TOP_K):
                pltpu.make_async_copy(ys_ref.at[pl.ds(tile_pos_ref[0, 0, slot * t_route + r], 1)],
                                      rows_ref.at[into, slot, pl.ds(r, 1)],
                                      sem.at[into, slot]).start(priority=slot)
            return carry

        lax.fori_loop(0, t_route, issue, 0, unroll=ISSUE_UNROLL)

    @pl.when(step == 0)
    def _():
        gather(pos_ref, 0)

    @pl.when(step < last)
    def _():
        gather(next_pos_ref, 1 - buf)

    for slot in range(TOP_K):
        pltpu.make_async_copy(ys_ref.at[pl.ds(0, t_route)], rows_ref.at[buf, slot], sem.at[buf, slot]).wait()

    info = info_ref[0]
    lane = lax.broadcasted_iota(jnp.int32, info.shape, 1)
    ffn = (_lane_pick(info, lane, INFO_W1) * rows_ref[buf, 0]
           + _lane_pick(info, lane, INFO_W2) * rows_ref[buf, 1])
    out_ref[0] = _layer_norm(alpha * h_ref[0] + ffn, g_ref[...], b_ref[...])


def _combine(pos, info, h3, g, b, ys, cfg, alpha):
    tr = cfg.t_route
    steps = cfg.seq // tr
    n_tiles = cfg.batch * steps
    const = lambda bb, j: (0, 0)

    def pos_spec(ahead):
        return pl.BlockSpec((1, 1, TOP_K * tr),
                            lambda bb, j: (jnp.minimum(bb * steps + j + ahead, n_tiles - 1), 0, 0),
                            memory_space=pltpu.SMEM)

    return pl.pallas_call(
        functools.partial(_combine_kernel, alpha=alpha, t_route=tr),
        grid=(cfg.batch, steps),
        in_specs=[pos_spec(0), pos_spec(1),
                  pl.BlockSpec((1, tr, LANE), lambda bb, j: (bb, j, 0)),
                  _seq_rows_spec(tr, D_MODEL),
                  pl.BlockSpec((1, D_MODEL), const),
                  pl.BlockSpec((1, D_MODEL), const),
                  pl.BlockSpec(memory_space=pl.ANY)],
        out_specs=pl.BlockSpec((1, tr, D_MODEL), lambda bb, j: (bb, j, 0)),
        out_shape=jax.ShapeDtypeStruct((cfg.batch, cfg.seq, D_MODEL), F32),
        scratch_shapes=[pltpu.VMEM((2, TOP_K, tr, D_MODEL), F32), pltpu.SemaphoreType.DMA((2, TOP_K))],
        compiler_params=_params(("arbitrary", "arbitrary")),
        name="moe_combine_ln",
    )(pos, pos, info, h3, g, b, ys)


def _routed_moe(h3, w_router, wg, wu, wd, g, b, cfg, alpha):
    n_e, tg, tr = cfg.n_experts, cfg.t_group, cfg.t_route
    n_tok = cfg.batch * cfg.seq
    rows = TOP_K * n_tok + n_e * tg
    wr_pad = jnp.pad(w_router, ((0, 0), (0, LANE - n_e)))
    info, route, seen = _router(h3, wr_pad, cfg)

    as_int = lambda row: route[:, row, :].astype(jnp.int32)
    counts = seen[0, :n_e].astype(jnp.int32)
    padded = (counts + tg - 1) // tg * tg
    ends = jnp.cumsum(padded)
    starts = ends - padded
    start_of = lambda e: jnp.sum(jnp.where(e[..., None] == jnp.arange(n_e), starts, 0), axis=-1)
    pos = jnp.stack([start_of(as_int(INFO_E1)) + as_int(INFO_R1),
                     start_of(as_int(INFO_E2)) + as_int(INFO_R2)], axis=1)
    pos = pos.reshape(n_tok // tr, 1, TOP_K * tr)
    fill = jnp.stack([jnp.append(starts + counts, ends[-1] // ZERO_ROWS),
                      jnp.append(ends, rows // ZERO_ROWS)])
    n_valid = (ends[-1] // tg).reshape(1)
    tile_start = jnp.arange(rows // tg, dtype=jnp.int32) * tg
    tile_expert = jnp.sum(tile_start[:, None] >= ends[None, :], axis=-1).astype(jnp.int32)
    tile_expert = tile_expert[jnp.minimum(jnp.arange(rows // tg), n_valid[0] - 1)]

    xs = _dispatch(pos, fill, h3, cfg, rows)
    ys = _grouped_ffn(tile_expert, n_valid, xs, wg, wu, wd, cfg)
    return _combine(pos, info, h3, g, b, ys, cfg, alpha)


def _rope_tables(lp):
    half = HEAD_DIM // 2
    inv = 1.0 / (ROPE_THETA ** (jnp.arange(0, HEAD_DIM, 2, dtype=F32) / HEAD_DIM))
    pos = jnp.arange(lp, dtype=F32)
    ang = pos[:, None] * inv[None, :]
    ang = jnp.tile(ang, (1, LANE // half))
    sign = jnp.where((jnp.arange(LANE) % HEAD_DIM) < half, -1.0, 1.0).astype(F32)
    return jnp.cos(ang), jnp.sin(ang) * sign[None, :]


def _forward(cfg, x, meta_tokens, ln_emb_g, ln_emb_b, w_in, conv_w, lambda_q1, lambda_k1,
             lambda_q2, lambda_k2, subln_g, w_out, ln_mix_g, ln_mix_b, ln_ffn_g, ln_ffn_b,
             w_gate_dense, w_up_dense, w_down_dense, w_router, w_gate_moe, w_up_moe, w_down_moe):
    alpha = (2 * cfg.depth) ** 0.25
    tokens = cfg.tokens
    vec = lambda a: a.reshape(1, -1)
    cos, sin = _rope_tables(cfg.lp)
    h = _embed(x, meta_tokens, vec(ln_emb_g), vec(ln_emb_b), cfg).reshape(tokens, D_MODEL)
    for layer in range(cfg.depth):
        lambda_init = 0.8 - 0.6 * math.exp(-0.3 * layer)
        conv, q, k, v = _inproj(h.reshape(cfg.batch, cfg.lp, D_MODEL), w_in[layer].astype(BF16),
                                conv_w[layer], cos, sin, cfg)
        attn = _attention(q, k, v, vec(lambda_q1[layer]), vec(lambda_k1[layer]),
                          vec(lambda_q2[layer]), vec(lambda_k2[layer]), vec(subln_g[layer]),
                          lambda_init, cfg)
        h = _outproj(conv.reshape(tokens, D_CONV), attn.reshape(tokens, D_MODEL - D_CONV),
                     w_out[layer].astype(BF16), h, vec(ln_mix_g[layer]), vec(ln_mix_b[layer]), cfg, alpha)
        idx = layer // 2
        g, b = vec(ln_ffn_g[layer]), vec(ln_ffn_b[layer])
        if layer % 2 == 0:
            h = _ffn(w_gate_dense[idx].astype(BF16), w_up_dense[idx].astype(BF16),
                     w_down_dense[idx].astype(BF16), h, g, b, cfg, alpha)
        else:
            assert layer == cfg.depth - 1
            return _routed_moe(h.reshape(cfg.batch, cfg.lp, D_MODEL), w_router[idx],
                               w_gate_moe[idx].astype(BF16), w_up_moe[idx].astype(BF16),
                               w_down_moe[idx].astype(BF16), g, b, cfg, alpha)
    return h.reshape(cfg.batch, cfg.lp, D_MODEL)[:, N_META:N_META + cfg.seq]


_CFG = Cfg(batch=8, seq=4096, depth=2, d_ff_dense=2816, d_ff_expert=3584, n_experts=8,
           t_embed=128, n_embed=11, t_proj=1056, t_q=384, t_k=384, t_tok=1024, t_ff_dense=256, t_ff_expert=1792,
           t_route=1024, t_group=1024)


def kernel(x, meta_tokens, ln_emb_g, ln_emb_b, w_in, conv_w, lambda_q1, lambda_k1, lambda_q2, lambda_k2, subln_g, w_out, ln_mix_g, ln_mix_b, ln_ffn_g, ln_ffn_b, w_gate_dense, w_up_dense, w_down_dense, w_router, w_gate_moe, w_up_moe, w_down_moe):
    return _forward(_CFG, x, meta_tokens, ln_emb_g, ln_emb_b, w_in, conv_w, lambda_q1, lambda_k1,
                    lambda_q2, lambda_k2, subln_g, w_out, ln_mix_g, ln_mix_b, ln_ffn_g, ln_ffn_b,
                    w_gate_dense, w_up_dense, w_down_dense, w_router, w_gate_moe, w_up_moe, w_down_moe)
```

```python
import dataclasses
import functools
import math

import jax
import jax.numpy as jnp
from jax import lax
from jax.experimental import pallas as pl
from jax.experimental.pallas import tpu as pltpu

F32 = jnp.float32
BF16 = jnp.bfloat16

LANE = 128
D_MODEL = 1024
N_META = 16
PAD = LANE - N_META
D_CONV = D_MODEL // 2
CONV_WIDTH = 3
HEAD_DIM = 64
N_HEADS = 4
V_DIM = 2 * HEAD_DIM
QK_W = N_HEADS * 2 * HEAD_DIM
GROUP_W = 512
ROPE_THETA = 10000.0
TOP_K = 2
LN_EPS = 1e-5
RMS_EPS = 1e-5
NEG_BIG = -1e30
VMEM_LIMIT = 56 * 1024 * 1024


@dataclasses.dataclass(frozen=True)
class Cfg:
    batch: int
    seq: int
    depth: int
    d_ff_dense: int
    d_ff_expert: int
    n_experts: int
    t_embed: int
    n_embed: int
    t_proj: int
    t_q: int
    t_k: int
    t_tok: int
    t_ff_dense: int
    t_ff_expert: int
    t_route: int
    t_group: int

    @property
    def lp(self):
        return N_META + self.seq + PAD

    @property
    def tokens(self):
        return self.batch * self.lp


def _params(semantics):
    return pltpu.CompilerParams(dimension_semantics=semantics, vmem_limit_bytes=VMEM_LIMIT)


def _layer_norm(y, g, b):
    mu = jnp.mean(y, axis=-1, keepdims=True)
    yc = y - mu
    var = jnp.mean(yc * yc, axis=-1, keepdims=True)
    return yc * lax.rsqrt(var + LN_EPS) * g + b


def _cast_job(w, n_steps):
    flat = w.reshape(-1, w.shape[-1])
    n_blocks = 1 << (n_steps.bit_length() - 1)
    rows = flat.shape[0] // n_blocks
    assert rows * n_blocks == flat.shape[0] and rows % 16 == 0, (w.shape, n_steps)
    return flat, (rows, flat.shape[1]), n_blocks


def _embed_kernel(*refs, seq, n_blocks):
    x_refs, (meta_ref, g_ref, b_ref, h_ref) = refs[:n_blocks + 1], refs[n_blocks + 1:]
    j = pl.program_id(1)
    te = x_refs[1].shape[1]
    head = jnp.where(j == 0, meta_ref[...], x_refs[0][0, te - N_META:])
    rows = jnp.concatenate([head] + [r[0] for r in x_refs[1:n_blocks]] + [x_refs[n_blocks][0, :te - N_META]],
                           axis=0)
    rid = j * (n_blocks * te) + lax.broadcasted_iota(jnp.int32, (n_blocks * te, 1), 0)
    rows = jnp.where(rid < N_META + seq, rows, 0.0)
    h_ref[0] = _layer_norm(rows, g_ref[...], b_ref[...])


def _embed(x, meta_tokens, g, b, cfg):
    te, n = cfg.t_embed, cfg.n_embed
    last = cfg.seq // te - 1

    def x_spec(i):
        return pl.BlockSpec((1, te, D_MODEL), lambda bb, j: (bb, jnp.clip(n * j + i - 1, 0, last), 0))

    const = lambda bb, j: (0, 0)
    return pl.pallas_call(
        functools.partial(_embed_kernel, seq=cfg.seq, n_blocks=n),
        grid=(cfg.batch, cfg.lp // (n * te)),
        in_specs=[x_spec(i) for i in range(n + 1)]
        + [pl.BlockSpec((N_META, D_MODEL), const),
           pl.BlockSpec((1, D_MODEL), const),
           pl.BlockSpec((1, D_MODEL), const)],
        out_specs=pl.BlockSpec((1, n * te, D_MODEL), lambda bb, j: (bb, j, 0)),
        out_shape=jax.ShapeDtypeStruct((cfg.batch, cfg.lp, D_MODEL), F32),
        compiler_params=_params(("parallel", "arbitrary")),
        name="embed_ln",
    )(*([x] * (n + 1)), meta_tokens, g, b)


def _inproj_kernel(*refs, t_proj, side_cast):
    if side_cast:
        (h_ref, w_ref, cw_ref, cos_ref, sin_ref, cast_src_ref,
         conv_ref, q_ref, k_ref, v_ref, cast_dst_ref, carry_ref) = refs
        cast_dst_ref[...] = cast_src_ref[...].astype(BF16)
    else:
        h_ref, w_ref, cw_ref, cos_ref, sin_ref, conv_ref, q_ref, k_ref, v_ref, carry_ref = refs
    j = pl.program_id(1)
    x = h_ref[0].astype(BF16)

    def proj(g):
        return jnp.dot(x, w_ref[:, g * GROUP_W:(g + 1) * GROUP_W], preferred_element_type=F32)

    rid = lax.broadcasted_iota(jnp.int32, (t_proj, 1), 0)
    gated = proj(1) * proj(2)

    @pl.when(j == 0)
    def _():
        carry_ref[...] = jnp.zeros_like(carry_ref)

    prev = carry_ref[...]
    back1 = jnp.where(rid == 0, prev[7:8], pltpu.roll(gated, 1, 0))
    back2 = jnp.where(rid == 0, prev[6:7], jnp.where(rid == 1, prev[7:8], pltpu.roll(gated, 2, 0)))
    carry_ref[...] = gated[t_proj - 8:]
    cw = cw_ref[...]
    conv = cw[0:1] * back2 + cw[1:2] * back1 + cw[2:3] * gated
    conv_ref[0] = (proj(0) * conv).astype(BF16)

    cos = cos_ref[...]
    sin = sin_ref[...]
    lane = lax.broadcasted_iota(jnp.int32, (1, LANE), 1)
    low_half = (lane % HEAD_DIM) < HEAD_DIM // 2

    def rope(z):
        partner = jnp.where(low_half, pltpu.roll(z, LANE - HEAD_DIM // 2, 1),
                            pltpu.roll(z, HEAD_DIM // 2, 1))
        return z * cos + partner * sin

    zq = proj(3)
    zk = proj(4)
    scale = HEAD_DIM ** -0.5 * math.log2(math.e)
    for h in range(N_HEADS):
        sl = slice(h * LANE, (h + 1) * LANE)
        q_ref[0, :, sl] = (rope(zq[:, sl]) * scale).astype(BF16)
        k_ref[0, :, sl] = rope(zk[:, sl]).astype(BF16)
    v_ref[0] = proj(5).astype(BF16)


def _inproj(h, w_in, conv_w, cos, sin, cfg, cast=None):
    tp = cfg.t_proj
    steps = cfg.lp // tp
    row = lambda bb, j: (bb, j, 0)
    const = lambda bb, j: (0, 0)
    out_spec = pl.BlockSpec((1, tp, GROUP_W), row)
    out_shape = jax.ShapeDtypeStruct((cfg.batch, cfg.lp, GROUP_W), BF16)
    in_specs = [pl.BlockSpec((1, tp, D_MODEL), row),
                pl.BlockSpec((D_MODEL, 6 * GROUP_W), const),
                pl.BlockSpec((CONV_WIDTH, D_CONV), const),
                pl.BlockSpec((tp, LANE), lambda bb, j: (j, 0)),
                pl.BlockSpec((tp, LANE), lambda bb, j: (j, 0))]
    args, out_specs, out_shapes = [h, w_in, conv_w, cos, sin], [out_spec] * 4, [out_shape] * 4
    if cast is not None:
        flat, block, n_blocks = _cast_job(cast, cfg.batch * steps)
        cast_spec = pl.BlockSpec(block, lambda bb, j: (jnp.minimum(bb * steps + j, n_blocks - 1), 0))
        in_specs.append(cast_spec)
        args.append(flat)
        out_specs.append(cast_spec)
        out_shapes.append(jax.ShapeDtypeStruct(flat.shape, BF16))
    outs = pl.pallas_call(
        functools.partial(_inproj_kernel, t_proj=tp, side_cast=cast is not None),
        grid=(cfg.batch, steps),
        in_specs=in_specs,
        out_specs=out_specs,
        out_shape=out_shapes,
        scratch_shapes=[pltpu.VMEM((8, D_CONV), F32)],
        compiler_params=_params(("arbitrary", "arbitrary")),
        name="inproj_conv_rope",
    )(*args)
    outs = list(outs)
    if cast is not None:
        outs[4] = outs[4].reshape(cast.shape)
    return outs


def _attn_kernel(lq1_ref, lk1_ref, lq2_ref, lk2_ref, g_ref, q_ref, k_ref, v_ref,
                 o_ref, m_ref, l_ref, acc_ref, *, t_q, t_k, lambda_init):
    j = pl.program_id(1)
    head = lambda h: slice(h * LANE, (h + 1) * LANE)
    first_map = lax.broadcasted_iota(jnp.int32, (1, LANE), 1) < HEAD_DIM
    zero = jnp.zeros((), BF16)
    q2 = []
    for h in range(N_HEADS):
        q = q_ref[0, :, head(h)]
        q2.append(jnp.concatenate([jnp.where(first_map, q, zero), jnp.where(first_map, zero, q)], axis=0))
    m_ref[...] = jnp.full_like(m_ref, NEG_BIG)
    l_ref[...] = jnp.zeros_like(l_ref)
    acc_ref[...] = jnp.zeros_like(acc_ref)

    def step(c, width, causal):
        start = pl.multiple_of(c * t_k, t_k)
        for h in range(N_HEADS):
            k = k_ref[0, pl.ds(start, width), head(h)]
            v = v_ref[0, pl.ds(start, width), head(h)]
            s = lax.dot_general(q2[h], k, (((1,), (1,)), ((), ())), preferred_element_type=F32)
            if causal:
                qpos = lax.broadcasted_iota(jnp.int32, (2 * t_q, 1), 0) % t_q
                kpos = lax.broadcasted_iota(jnp.int32, (1, width), 1) - (width - t_k)
                s = jnp.where(kpos <= qpos, s, NEG_BIG)
            m_prev = m_ref[h]
            m_new = jnp.maximum(m_prev, jnp.max(s, axis=-1, keepdims=True))
            alpha = jnp.exp2(m_prev - m_new)
            p = jnp.exp2((s - jnp.concatenate([m_new] * (width // LANE), axis=1)).astype(BF16))
            l_ref[h] = alpha * l_ref[h] + jnp.sum(p.astype(F32), axis=-1, keepdims=True)
            acc_ref[h] = alpha * acc_ref[h] + jnp.dot(p, v, preferred_element_type=F32)
            m_ref[h] = m_new

    def pair(i, carry):
        step(2 * i, 2 * t_k, causal=False)
        return carry

    lax.fori_loop(0, j // 2, pair, 0)

    @pl.when(j % 2 == 1)
    def _():
        step(j - 1, 2 * t_k, causal=True)

    @pl.when(j % 2 == 0)
    def _():
        step(j, t_k, causal=True)

    lam = (jnp.exp(jnp.sum(lq1_ref[...] * lk1_ref[...], axis=-1, keepdims=True))
           - jnp.exp(jnp.sum(lq2_ref[...] * lk2_ref[...], axis=-1, keepdims=True)) + lambda_init)
    for h in range(N_HEADS):
        o = acc_ref[h] / l_ref[h]
        d = o[:t_q] - lam * o[t_q:]
        d = d * lax.rsqrt(jnp.mean(d * d, axis=-1, keepdims=True) + RMS_EPS) * g_ref[...]
        o_ref[0, :, head(h)] = (d * (1.0 - lambda_init)).astype(BF16)


def _attention(q, k, v, lq1, lk1, lq2, lk2, subln_g, lambda_init, cfg):
    tq = cfg.t_q
    width = N_HEADS * V_DIM
    small = lambda bb, j: (0, 0)
    q_spec = pl.BlockSpec((1, tq, width), lambda bb, j: (bb, j, 0))
    kv_spec = pl.BlockSpec((1, cfg.lp, width), lambda bb, j: (bb, 0, 0))
    return pl.pallas_call(
        functools.partial(_attn_kernel, t_q=tq, t_k=cfg.t_k, lambda_init=lambda_init),
        grid=(cfg.batch, cfg.lp // tq),
        in_specs=[pl.BlockSpec((1, HEAD_DIM), small)] * 4
        + [pl.BlockSpec((1, V_DIM), small), q_spec, kv_spec, kv_spec],
        out_specs=q_spec,
        out_shape=jax.ShapeDtypeStruct((cfg.batch, cfg.lp, width), BF16),
        scratch_shapes=[pltpu.VMEM((N_HEADS, 2 * tq, LANE), F32),
                        pltpu.VMEM((N_HEADS, 2 * tq, LANE), F32),
                        pltpu.VMEM((N_HEADS, 2 * tq, V_DIM), F32)],
        compiler_params=_params(("parallel", "arbitrary")),
        name="diff_attention",
    )(lq1, lk1, lq2, lk2, subln_g, q, k, v)


def _outproj_kernel(conv_ref, attn_ref, w_ref, h_ref, g_ref, b_ref, ho_ref, *, alpha):
    mix = (jnp.dot(conv_ref[...], w_ref[:D_CONV], preferred_element_type=F32)
           + jnp.dot(attn_ref[...], w_ref[D_CONV:], preferred_element_type=F32))
    ho_ref[...] = _layer_norm(alpha * h_ref[...] + mix, g_ref[...], b_ref[...])


def _outproj(conv, attn, w_out, h, g, b, cfg, alpha):
    tt = cfg.t_tok
    row = lambda i: (i, 0)
    const = lambda i: (0, 0)
    return pl.pallas_call(
        functools.partial(_outproj_kernel, alpha=alpha),
        grid=(cfg.tokens // tt,),
        in_specs=[pl.BlockSpec((tt, D_CONV), row),
                  pl.BlockSpec((tt, D_MODEL - D_CONV), row),
                  pl.BlockSpec((D_MODEL, D_MODEL), const),
                  pl.BlockSpec((tt, D_MODEL), row),
                  pl.BlockSpec((1, D_MODEL), const),
                  pl.BlockSpec((1, D_MODEL), const)],
        out_specs=pl.BlockSpec((tt, D_MODEL), row),
        out_shape=jax.ShapeDtypeStruct((cfg.tokens, D_MODEL), F32),
        compiler_params=_params(("parallel",)),
        name="outproj_ln",
    )(conv, attn, w_out, h, g, b)


def _swiglu_contrib(x, wg_ref, wu_ref, wd_ref):
    gate = jnp.dot(x, wg_ref[...], preferred_element_type=F32)
    up = jnp.dot(x, wu_ref[...], preferred_element_type=F32)
    act = gate * (1.0 / (1.0 + jnp.exp(-gate))) * up
    return jnp.dot(act.astype(BF16), wd_ref[...], preferred_element_type=F32)


def _ffn_kernel(wg_ref, wu_ref, wd_ref, h_ref, g_ref, b_ref, *rest, alpha, t_ff):
    if len(rest) == 3:
        cast_src_ref, ho_ref, cast_dst_ref = rest
        cast_dst_ref[...] = cast_src_ref[...].astype(BF16)
    else:
        ho_ref, = rest
    h = h_ref[...]
    x = h.astype(BF16)
    acc = None
    for c in range(wg_ref.shape[1] // t_ff):
        cols = pl.ds(c * t_ff, t_ff)
        contrib = _swiglu_contrib(x, wg_ref.at[:, cols], wu_ref.at[:, cols], wd_ref.at[cols, :])
        acc = contrib if acc is None else acc + contrib
    ho_ref[...] = _layer_norm(alpha * h + acc, g_ref[...], b_ref[...])


def _ffn(wg, wu, wd, h, g, b, cfg, alpha, cast=None):
    tt = cfg.t_tok
    steps = cfg.tokens // tt
    row = lambda i: (i, 0)
    const = lambda i: (0, 0)
    resident = lambda shape: pl.BlockSpec(shape, const, pipeline_mode=pl.Buffered(1))
    in_specs = [resident((D_MODEL, cfg.d_ff_dense)),
                resident((D_MODEL, cfg.d_ff_dense)),
                resident((cfg.d_ff_dense, D_MODEL)),
                pl.BlockSpec((tt, D_MODEL), row),
                pl.BlockSpec((1, D_MODEL), const),
                pl.BlockSpec((1, D_MODEL), const)]
    args = [wg, wu, wd, h, g, b]
    out_specs = [pl.BlockSpec((tt, D_MODEL), row)]
    out_shapes = [jax.ShapeDtypeStruct((cfg.tokens, D_MODEL), F32)]
    if cast is not None:
        flat, block, n_blocks = _cast_job(cast, steps)
        cast_spec = pl.BlockSpec(block, lambda i: (jnp.minimum(i, n_blocks - 1), 0))
        in_specs.append(cast_spec)
        args.append(flat)
        out_specs.append(cast_spec)
        out_shapes.append(jax.ShapeDtypeStruct(flat.shape, BF16))
    outs = pl.pallas_call(
        functools.partial(_ffn_kernel, alpha=alpha, t_ff=cfg.t_ff_dense),
        grid=(steps,),
        in_specs=in_specs,
        out_specs=out_specs,
        out_shape=out_shapes,
        compiler_params=_params(("arbitrary",)),
        name="dense_ffn_ln",
    )(*args)
    outs = list(outs)
    if cast is not None:
        outs[1] = outs[1].reshape(cast.shape)
    return outs


INFO_E1, INFO_E2, INFO_W1, INFO_W2, INFO_R1, INFO_R2 = range(6)
ROUTE_ROWS = 8


def _lane_pick(values, lane, index):
    return jnp.sum(jnp.where(lane == index, values, 0.0), axis=-1, keepdims=True)


def _router_kernel(h_ref, wr_ref, info_ref, route_ref, cnt_ref, seen_ref, *, n_experts):
    @pl.when(jnp.logical_and(pl.program_id(0) == 0, pl.program_id(1) == 0))
    def _():
        seen_ref[...] = jnp.zeros_like(seen_ref)

    h = h_ref[0]
    w = wr_ref[...]
    h_hi = h.astype(BF16)
    h_lo = (h - h_hi.astype(F32)).astype(BF16)
    w_hi = w.astype(BF16)
    w_lo = (w - w_hi.astype(F32)).astype(BF16)
    logits = (jnp.dot(h_hi, w_hi, preferred_element_type=F32)
              + jnp.dot(h_lo, w_hi, preferred_element_type=F32)
              + jnp.dot(h_hi, w_lo, preferred_element_type=F32))
    t_route = logits.shape[0]
    lane = lax.broadcasted_iota(jnp.int32, logits.shape, 1).astype(F32)
    logits = jnp.where(lane < n_experts, logits, -jnp.inf)
    v1 = jnp.max(logits, axis=-1, keepdims=True)
    e1 = jnp.min(jnp.where(logits == v1, lane, float(LANE)), axis=-1, keepdims=True)
    rest = jnp.where(lane == e1, -jnp.inf, logits)
    v2 = jnp.max(rest, axis=-1, keepdims=True)
    e2 = jnp.min(jnp.where(rest == v2, lane, float(LANE)), axis=-1, keepdims=True)
    ex = jnp.exp(v2 - v1)
    w1 = 1.0 / (1.0 + ex)
    w2 = ex / (1.0 + ex)

    chosen = jnp.where(lane == e1, 1.0, 0.0) + jnp.where(lane == e2, 1.0, 0.0)
    r = lax.broadcasted_iota(jnp.int32, (t_route, t_route), 0)
    c = lax.broadcasted_iota(jnp.int32, (t_route, t_route), 1)
    earlier = jnp.where(c < r, 1.0, 0.0).astype(BF16)
    before = jnp.dot(earlier, chosen.astype(BF16), preferred_element_type=F32) + seen_ref[...]
    r1 = _lane_pick(before, lane, e1)
    r2 = _lane_pick(before, lane, e2)
    seen_ref[...] += jnp.sum(chosen, axis=0, keepdims=True)
    cnt_ref[...] = seen_ref[...]

    record = jnp.zeros_like(logits)
    for slot, val in ((INFO_E1, e1), (INFO_E2, e2), (INFO_W1, w1), (INFO_W2, w2),
                      (INFO_R1, r1), (INFO_R2, r2)):
        record = jnp.where(lane == slot, val, record)
    info_ref[0] = record
    route_ref[0] = record.T[:ROUTE_ROWS]


def _seq_rows_spec(t_rows, width):
    return pl.BlockSpec((pl.Element(1), pl.Element(t_rows), pl.Element(width)),
                        lambda bb, j: (bb, pl.multiple_of(N_META + j * t_rows, N_META), 0))


def _router(h3, wr_pad, cfg):
    tr = cfg.t_route
    steps = cfg.seq // tr
    return pl.pallas_call(
        functools.partial(_router_kernel, n_experts=cfg.n_experts),
        grid=(cfg.batch, cfg.seq // tr),
        in_specs=[_seq_rows_spec(tr, D_MODEL),
                  pl.BlockSpec((D_MODEL, LANE), lambda bb, j: (0, 0))],
        out_specs=[pl.BlockSpec((1, tr, LANE), lambda bb, j: (bb, j, 0)),
                   pl.BlockSpec((1, ROUTE_ROWS, tr), lambda bb, j: (bb * steps + j, 0, 0)),
                   pl.BlockSpec((1, LANE), lambda bb, j: (0, 0))],
        out_shape=[jax.ShapeDtypeStruct((cfg.batch, cfg.seq, LANE), F32),
                   jax.ShapeDtypeStruct((cfg.batch * steps, ROUTE_ROWS, tr), F32),
                   jax.ShapeDtypeStruct((1, LANE), F32)],
        scratch_shapes=[pltpu.VMEM((1, LANE), F32)],
        compiler_params=_params(("arbitrary", "arbitrary")),
        name="router_top2",
    )(h3, wr_pad)


ZERO_ROWS = 8
ISSUE_UNROLL = 8


def _dispatch_kernel(pos_ref, fill_ref, h_ref, xs_ref, zero_ref, sem, *, t_route, n_experts):
    def zero_copy(p, n):
        start = pl.multiple_of(p * n, n)
        return pltpu.make_async_copy(zero_ref.at[pl.ds(0, n)], xs_ref.at[pl.ds(start, n)], sem.at[2])

    def zero_rows(lo, hi, n):
        def start(p, carry):
            zero_copy(p, n).start()
            return carry

        def wait(p, carry):
            zero_copy(p, n).wait()
            return carry

        lax.fori_loop(lo, hi, start, 0)
        lax.fori_loop(lo, hi, wait, 0)

    @pl.when(jnp.logical_and(pl.program_id(0) == 0, pl.program_id(1) == 0))
    def _():
        zero_ref[...] = jnp.zeros_like(zero_ref)
        for e in range(n_experts):
            zero_rows(fill_ref[0, e], fill_ref[1, e], 1)
        zero_rows(fill_ref[0, n_experts], fill_ref[1, n_experts], ZERO_ROWS)

    def issue(r, carry):
        for slot in range(TOP_K):
            pltpu.make_async_copy(h_ref.at[0, pl.ds(r, 1)],
                                  xs_ref.at[pl.ds(pos_ref[0, 0, slot * t_route + r], 1)],
                                  sem.at[slot]).start(priority=slot)
        return carry

    lax.fori_loop(0, t_route, issue, 0, unroll=ISSUE_UNROLL)
    for slot in range(TOP_K):
        pltpu.make_async_copy(h_ref.at[0], xs_ref.at[pl.ds(0, t_route)], sem.at[slot]).wait()


def _dispatch(pos, fill, h3, cfg, rows):
    tr = cfg.t_route
    steps = cfg.seq // tr
    return pl.pallas_call(
        functools.partial(_dispatch_kernel, t_route=tr, n_experts=cfg.n_experts),
        grid=(cfg.batch, steps),
        in_specs=[pl.BlockSpec((1, 1, TOP_K * tr), lambda bb, j: (bb * steps + j, 0, 0),
                               memory_space=pltpu.SMEM),
                  pl.BlockSpec(memory_space=pltpu.SMEM),
                  _seq_rows_spec(tr, D_MODEL)],
        out_specs=pl.BlockSpec(memory_space=pl.ANY),
        out_shape=jax.ShapeDtypeStruct((rows, D_MODEL), F32),
        scratch_shapes=[pltpu.VMEM((ZERO_ROWS, D_MODEL), F32), pltpu.SemaphoreType.DMA((3,))],
        compiler_params=_params(("arbitrary", "arbitrary")),
        name="moe_dispatch",
    )(pos, fill, h3)


def _grouped_kernel(te_ref, nv_ref, x_ref, wg_ref, wu_ref, wd_ref, y_ref, xb_ref, *, t_chunk):
    f = pl.program_id(1)

    @pl.when(pl.program_id(0) < nv_ref[0])
    def _():
        @pl.when(f == 0)
        def _():
            xb_ref[...] = x_ref[...].astype(BF16)

        x = xb_ref[...]
        acc = None
        for c in range(wg_ref.shape[2] // t_chunk):
            cols = pl.ds(c * t_chunk, t_chunk)
            contrib = _swiglu_contrib(x, wg_ref.at[0, :, cols], wu_ref.at[0, :, cols], wd_ref.at[0, cols, :])
            acc = contrib if acc is None else acc + contrib

        @pl.when(f == 0)
        def _():
            y_ref[...] = acc

        @pl.when(f > 0)
        def _():
            y_ref[...] += acc

    @pl.when(pl.program_id(0) >= nv_ref[0])
    def _():
        y_ref[...] = jnp.zeros_like(y_ref)


def _grouped_ffn(tile_expert, n_valid, xs, wg, wu, wd, cfg):
    tg, tf = cfg.t_group, cfg.t_ff_expert
    n_f = cfg.d_ff_expert // tf
    rows = xs.shape[0]

    def live(i, nv):
        return jnp.minimum(i, nv[0] - 1)

    def f_idx(i, f, nv):
        return jnp.where(i < nv[0], f, n_f - 1)

    grid_spec = pltpu.PrefetchScalarGridSpec(
        num_scalar_prefetch=2,
        grid=(rows // tg, n_f),
        in_specs=[pl.BlockSpec((tg, D_MODEL), lambda i, f, te, nv: (live(i, nv), 0)),
                  pl.BlockSpec((1, D_MODEL, tf), lambda i, f, te, nv: (te[i], 0, f_idx(i, f, nv))),
                  pl.BlockSpec((1, D_MODEL, tf), lambda i, f, te, nv: (te[i], 0, f_idx(i, f, nv))),
                  pl.BlockSpec((1, tf, D_MODEL), lambda i, f, te, nv: (te[i], f_idx(i, f, nv), 0))],
        out_specs=pl.BlockSpec((tg, D_MODEL), lambda i, f, te, nv: (i, 0)),
        scratch_shapes=[pltpu.VMEM((tg, D_MODEL), BF16)],
    )
    return pl.pallas_call(
        functools.partial(_grouped_kernel, t_chunk=cfg.t_ff_dense),
        grid_spec=grid_spec,
        out_shape=jax.ShapeDtypeStruct((rows, D_MODEL), F32),
        compiler_params=_params(("arbitrary", "arbitrary")),
        name="moe_grouped_ffn",
    )(tile_expert, n_valid, xs, wg, wu, wd)


def _combine_kernel(pos_ref, next_pos_ref, info_ref, h_ref, g_ref, b_ref, ys_ref, out_ref, rows_ref, sem,
                    *, alpha, t_route):
    step = pl.program_id(0) * pl.num_programs(1) + pl.program_id(1)
    last = pl.num_programs(0) * pl.num_programs(1) - 1
    buf = step % 2

    def gather(tile_pos_ref, into):
        def issue(r, carry):
            for slot in range(TOP_K):
                pltpu.make_async_copy(ys_ref.at[pl.ds(tile_pos_ref[0, 0, slot * t_route + r], 1)],
                                      rows_ref.at[into, slot, pl.ds(r, 1)],
                                      sem.at[into, slot]).start(priority=slot)
            return carry

        lax.fori_loop(0, t_route, issue, 0, unroll=ISSUE_UNROLL)

    @pl.when(step == 0)
    def _():
        gather(pos_ref, 0)

    @pl.when(step < last)
    def _():
        gather(next_pos_ref, 1 - buf)

    for slot in range(TOP_K):
        pltpu.make_async_copy(ys_ref.at[pl.ds(0, t_route)], rows_ref.at[buf, slot], sem.at[buf, slot]).wait()

    info = info_ref[0]
    lane = lax.broadcasted_iota(jnp.int32, info.shape, 1)
    ffn = (_lane_pick(info, lane, INFO_W1) * rows_ref[buf, 0]
           + _lane_pick(info, lane, INFO_W2) * rows_ref[buf, 1])
    out_ref[0] = _layer_norm(alpha * h_ref[0] + ffn, g_ref[...], b_ref[...])


def _combine(pos, info, h3, g, b, ys, cfg, alpha):
    tr = cfg.t_route
    steps = cfg.seq // tr
    n_tiles = cfg.batch * steps
    const = lambda bb, j: (0, 0)

    def pos_spec(ahead):
        return pl.BlockSpec((1, 1, TOP_K * tr),
                            lambda bb, j: (jnp.minimum(bb * steps + j + ahead, n_tiles - 1), 0, 0),
                            memory_space=pltpu.SMEM)

    return pl.pallas_call(
        functools.partial(_combine_kernel, alpha=alpha, t_route=tr),
        grid=(cfg.batch, steps),
        in_specs=[pos_spec(0), pos_spec(1),
                  pl.BlockSpec((1, tr, LANE), lambda bb, j: (bb, j, 0)),
                  _seq_rows_spec(tr, D_MODEL),
                  pl.BlockSpec((1, D_MODEL), const),
                  pl.BlockSpec((1, D_MODEL), const),
                  pl.BlockSpec(memory_space=pl.ANY)],
        out_specs=pl.BlockSpec((1, tr, D_MODEL), lambda bb, j: (bb, j, 0)),
        out_shape=jax.ShapeDtypeStruct((cfg.batch, cfg.seq, D_MODEL), F32),
        scratch_shapes=[pltpu.VMEM((2, TOP_K, tr, D_MODEL), F32), pltpu.SemaphoreType.DMA((2, TOP_K))],
        compiler_params=_params(("arbitrary", "arbitrary")),
        name="moe_combine_ln",
    )(pos, pos, info, h3, g, b, ys)


def _routed_moe(h3, w_router, wg, wu, wd, g, b, cfg, alpha):
    n_e, tg, tr = cfg.n_experts, cfg.t_group, cfg.t_route
    n_tok = cfg.batch * cfg.seq
    rows = TOP_K * n_tok + n_e * tg
    wr_pad = jnp.pad(w_router, ((0, 0), (0, LANE - n_e)))
    info, route, seen = _router(h3, wr_pad, cfg)

    as_int = lambda row: route[:, row, :].astype(jnp.int32)
    counts = seen[0, :n_e].astype(jnp.int32)
    padded = (counts + tg - 1) // tg * tg
    ends = jnp.cumsum(padded)
    starts = ends - padded
    start_of = lambda e: jnp.sum(jnp.where(e[..., None] == jnp.arange(n_e), starts, 0), axis=-1)
    pos = jnp.stack([start_of(as_int(INFO_E1)) + as_int(INFO_R1),
                     start_of(as_int(INFO_E2)) + as_int(INFO_R2)], axis=1)
    pos = pos.reshape(n_tok // tr, 1, TOP_K * tr)
    fill = jnp.stack([jnp.append(starts + counts, ends[-1] // ZERO_ROWS),
                      jnp.append(ends, rows // ZERO_ROWS)])
    n_valid = (ends[-1] // tg).reshape(1)
    tile_start = jnp.arange(rows // tg, dtype=jnp.int32) * tg
    tile_expert = jnp.sum(tile_start[:, None] >= ends[None, :], axis=-1).astype(jnp.int32)
    tile_expert = tile_expert[jnp.minimum(jnp.arange(rows // tg), n_valid[0] - 1)]

    xs = _dispatch(pos, fill, h3, cfg, rows)
    ys = _grouped_ffn(tile_expert, n_valid, xs, wg, wu, wd, cfg)
    return _combine(pos, info, h3, g, b, ys, cfg, alpha)


def _rope_tables(lp):
    half = HEAD_DIM // 2
    inv = 1.0 / (ROPE_THETA ** (jnp.arange(0, HEAD_DIM, 2, dtype=F32) / HEAD_DIM))
    pos = jnp.arange(lp, dtype=F32)
    ang = pos[:, None] * inv[None, :]
    ang = jnp.tile(ang, (1, LANE // half))
    sign = jnp.where((jnp.arange(LANE) % HEAD_DIM) < half, -1.0, 1.0).astype(F32)
    return jnp.cos(ang), jnp.sin(ang) * sign[None, :]


def _forward(cfg, x, meta_tokens, ln_emb_g, ln_emb_b, w_in, conv_w, lambda_q1, lambda_k1,
             lambda_q2, lambda_k2, subln_g, w_out, ln_mix_g, ln_mix_b, ln_ffn_g, ln_ffn_b,
             w_gate_dense, w_up_dense, w_down_dense, w_router, w_gate_moe, w_up_moe, w_down_moe):
    alpha = (2 * cfg.depth) ** 0.25
    tokens = cfg.tokens
    vec = lambda a: a.reshape(1, -1)
    cos, sin = _rope_tables(cfg.lp)
    h = _embed(x, meta_tokens, vec(ln_emb_g), vec(ln_emb_b), cfg).reshape(tokens, D_MODEL)
    moe_bf16 = {}
    for layer in range(cfg.depth):
        lambda_init = 0.8 - 0.6 * math.exp(-0.3 * layer)
        dense = layer % 2 == 0
        idx = layer // 2
        feeds_moe = layer + 1 < cfg.depth if dense else True
        side = (w_gate_moe if dense else w_up_moe)[idx] if feeds_moe else None
        conv, q, k, v, *cast = _inproj(h.reshape(cfg.batch, cfg.lp, D_MODEL), w_in[layer].astype(BF16),
                                       conv_w[layer], cos, sin, cfg, cast=side)
        if feeds_moe:
            moe_bf16["gate" if dense else "up"] = cast[0]
        attn = _attention(q, k, v, vec(lambda_q1[layer]), vec(lambda_k1[layer]),
                          vec(lambda_q2[layer]), vec(lambda_k2[layer]), vec(subln_g[layer]),
                          lambda_init, cfg)
        h = _outproj(conv.reshape(tokens, D_CONV), attn.reshape(tokens, D_MODEL - D_CONV),
                     w_out[layer].astype(BF16), h, vec(ln_mix_g[layer]), vec(ln_mix_b[layer]), cfg, alpha)
        g, b = vec(ln_ffn_g[layer]), vec(ln_ffn_b[layer])
        if dense:
            h, *cast = _ffn(w_gate_dense[idx].astype(BF16), w_up_dense[idx].astype(BF16),
                            w_down_dense[idx].astype(BF16), h, g, b, cfg, alpha,
                            cast=w_down_moe[idx] if feeds_moe else None)
            if feeds_moe:
                moe_bf16["down"] = cast[0]
        else:
            assert layer == cfg.depth - 1
            return _routed_moe(h.reshape(cfg.batch, cfg.lp, D_MODEL), w_router[idx],
                               moe_bf16["gate"], moe_bf16["up"], moe_bf16["down"], g, b, cfg, alpha)
    return h.reshape(cfg.batch, cfg.lp, D_MODEL)[:, N_META:N_META + cfg.seq]


_CFG = Cfg(batch=8, seq=4096, depth=2, d_ff_dense=2816, d_ff_expert=3584, n_experts=8,
           t_embed=128, n_embed=11, t_proj=1056, t_q=384, t_k=384, t_tok=1024, t_ff_dense=256, t_ff_expert=1792,
           t_route=1024, t_group=1024)


def kernel(x, meta_tokens, ln_emb_g, ln_emb_b, w_in, conv_w, lambda_q1, lambda_k1, lambda_q2, lambda_k2, subln_g, w_out, ln_mix_g, ln_mix_b, ln_ffn_g, ln_ffn_b, w_gate_dense, w_up_dense, w_down_dense, w_router, w_gate_moe, w_up_moe, w_down_moe):
    return _forward(_CFG, x, meta_tokens, ln_emb_g, ln_emb_b, w_in, conv_w, lambda_q1, lambda_k1,
                    lambda_q2, lambda_k2, subln_g, w_out, ln_mix_g, ln_mix_b, ln_ffn_g, ln_ffn_b,
                    w_gate_dense, w_up_dense, w_down_dense, w_router, w_gate_moe, w_up_moe, w_down_moe)
```

```python
import dataclasses
import functools
import math

import jax
import jax.numpy as jnp
from jax import lax
from jax.experimental import pallas as pl
from jax.experimental.pallas import tpu as pltpu

F32 = jnp.float32
BF16 = jnp.bfloat16

LANE = 128
D_MODEL = 1024
N_META = 16
PAD = LANE - N_META
D_CONV = D_MODEL // 2
CONV_WIDTH = 3
HEAD_DIM = 64
N_HEADS = 4
V_DIM = 2 * HEAD_DIM
QK_W = N_HEADS * 2 * HEAD_DIM
GROUP_W = 512
ROPE_THETA = 10000.0
TOP_K = 2
LN_EPS = 1e-5
RMS_EPS = 1e-5
NEG_BIG = -1e30
VMEM_LIMIT = 56 * 1024 * 1024


@dataclasses.dataclass(frozen=True)
class Cfg:
    batch: int
    seq: int
    depth: int
    d_ff_dense: int
    d_ff_expert: int
    n_experts: int
    t_embed: int
    n_embed: int
    t_proj: int
    t_q: int
    t_k: int
    t_tok: int
    t_ff_dense: int
    t_ff_expert: int
    t_route: int
    t_group: int

    @property
    def lp(self):
        return N_META + self.seq + PAD

    @property
    def tokens(self):
        return self.batch * self.lp


def _params(semantics):
    return pltpu.CompilerParams(dimension_semantics=semantics, vmem_limit_bytes=VMEM_LIMIT)


def _layer_norm(y, g, b):
    mu = jnp.mean(y, axis=-1, keepdims=True)
    yc = y - mu
    var = jnp.mean(yc * yc, axis=-1, keepdims=True)
    return yc * lax.rsqrt(var + LN_EPS) * g + b


def _cast_job(w, n_steps):
    flat = w.reshape(-1, w.shape[-1])
    n_blocks = 1 << (n_steps.bit_length() - 1)
    rows = flat.shape[0] // n_blocks
    assert rows * n_blocks == flat.shape[0] and rows % 16 == 0, (w.shape, n_steps)
    return flat, (rows, flat.shape[1]), n_blocks


def _embed_kernel(*refs, seq, n_blocks):
    x_refs, (meta_ref, g_ref, b_ref, h_ref) = refs[:n_blocks + 1], refs[n_blocks + 1:]
    j = pl.program_id(1)
    te = x_refs[1].shape[1]
    head = jnp.where(j == 0, meta_ref[...], x_refs[0][0, te - N_META:])
    rows = jnp.concatenate([head] + [r[0] for r in x_refs[1:n_blocks]] + [x_refs[n_blocks][0, :te - N_META]],
                           axis=0)
    rid = j * (n_blocks * te) + lax.broadcasted_iota(jnp.int32, (n_blocks * te, 1), 0)
    rows = jnp.where(rid < N_META + seq, rows, 0.0)
    h_ref[0] = _layer_norm(rows, g_ref[...], b_ref[...])


def _embed(x, meta_tokens, g, b, cfg):
    te, n = cfg.t_embed, cfg.n_embed
    last = cfg.seq // te - 1

    def x_spec(i):
        return pl.BlockSpec((1, te, D_MODEL), lambda bb, j: (bb, jnp.clip(n * j + i - 1, 0, last), 0))

    const = lambda bb, j: (0, 0)
    return pl.pallas_call(
        functools.partial(_embed_kernel, seq=cfg.seq, n_blocks=n),
        grid=(cfg.batch, cfg.lp // (n * te)),
        in_specs=[x_spec(i) for i in range(n + 1)]
        + [pl.BlockSpec((N_META, D_MODEL), const),
           pl.BlockSpec((1, D_MODEL), const),
           pl.BlockSpec((1, D_MODEL), const)],
        out_specs=pl.BlockSpec((1, n * te, D_MODEL), lambda bb, j: (bb, j, 0)),
        out_shape=jax.ShapeDtypeStruct((cfg.batch, cfg.lp, D_MODEL), F32),
        compiler_params=_params(("parallel", "arbitrary")),
        name="embed_ln",
    )(*([x] * (n + 1)), meta_tokens, g, b)


def _inproj_kernel(*refs, t_proj, side_cast):
    if side_cast:
        (h_ref, w_ref, cw_ref, cos_ref, sin_ref, cast_src_ref,
         conv_ref, q_ref, k_ref, v_ref, cast_dst_ref, carry_ref) = refs
        cast_dst_ref[...] = cast_src_ref[...].astype(BF16)
    else:
        h_ref, w_ref, cw_ref, cos_ref, sin_ref, conv_ref, q_ref, k_ref, v_ref, carry_ref = refs
    j = pl.program_id(1)
    x = h_ref[0].astype(BF16)

    def proj(g):
        return jnp.dot(x, w_ref[:, g * GROUP_W:(g + 1) * GROUP_W], preferred_element_type=F32)

    rid = lax.broadcasted_iota(jnp.int32, (t_proj, 1), 0)
    gated = proj(1) * proj(2)

    @pl.when(j == 0)
    def _():
        carry_ref[...] = jnp.zeros_like(carry_ref)

    prev = carry_ref[...]
    back1 = jnp.where(rid == 0, prev[7:8], pltpu.roll(gated, 1, 0))
    back2 = jnp.where(rid == 0, prev[6:7], jnp.where(rid == 1, prev[7:8], pltpu.roll(gated, 2, 0)))
    carry_ref[...] = gated[t_proj - 8:]
    cw = cw_ref[...]
    conv = cw[0:1] * back2 + cw[1:2] * back1 + cw[2:3] * gated
    conv_ref[0] = (proj(0) * conv).astype(BF16)

    cos = cos_ref[...]
    sin = sin_ref[...]
    lane = lax.broadcasted_iota(jnp.int32, (1, LANE), 1)
    low_half = (lane % HEAD_DIM) < HEAD_DIM // 2

    def rope(z):
        partner = jnp.where(low_half, pltpu.roll(z, LANE - HEAD_DIM // 2, 1),
                            pltpu.roll(z, HEAD_DIM // 2, 1))
        return z * cos + partner * sin

    zq = proj(3)
    zk = proj(4)
    scale = HEAD_DIM ** -0.5 * math.log2(math.e)
    for h in range(N_HEADS):
        sl = slice(h * LANE, (h + 1) * LANE)
        q_ref[0, :, sl] = (rope(zq[:, sl]) * scale).astype(BF16)
        k_ref[0, :, sl] = rope(zk[:, sl]).astype(BF16)
    v_ref[0] = proj(5).astype(BF16)


def _inproj(h, w_in, conv_w, cos, sin, cfg, cast=None):
    tp = cfg.t_proj
    steps = cfg.lp // tp
    row = lambda bb, j: (bb, j, 0)
    const = lambda bb, j: (0, 0)
    out_spec = pl.BlockSpec((1, tp, GROUP_W), row)
    out_shape = jax.ShapeDtypeStruct((cfg.batch, cfg.lp, GROUP_W), BF16)
    in_specs = [pl.BlockSpec((1, tp, D_MODEL), row),
                pl.BlockSpec((D_MODEL, 6 * GROUP_W), const),
                pl.BlockSpec((CONV_WIDTH, D_CONV), const),
                pl.BlockSpec((tp, LANE), lambda bb, j: (j, 0)),
                pl.BlockSpec((tp, LANE), lambda bb, j: (j, 0))]
    args, out_specs, out_shapes = [h, w_in, conv_w, cos, sin], [out_spec] * 4, [out_shape] * 4
    if cast is not None:
        flat, block, n_blocks = _cast_job(cast, cfg.batch * steps)
        cast_spec = pl.BlockSpec(block, lambda bb, j: (jnp.minimum(bb * steps + j, n_blocks - 1), 0))
        in_specs.append(cast_spec)
        args.append(flat)
        out_specs.append(cast_spec)
        out_shapes.append(jax.ShapeDtypeStruct(flat.shape, BF16))
    outs = pl.pallas_call(
        functools.partial(_inproj_kernel, t_proj=tp, side_cast=cast is not None),
        grid=(cfg.batch, steps),
        in_specs=in_specs,
        out_specs=out_specs,
        out_shape=out_shapes,
        scratch_shapes=[pltpu.VMEM((8, D_CONV), F32)],
        compiler_params=_params(("arbitrary", "arbitrary")),
        name="inproj_conv_rope",
    )(*args)
    outs = list(outs)
    if cast is not None:
        outs[4] = outs[4].reshape(cast.shape)
    return outs


def _attn_kernel(lq1_ref, lk1_ref, lq2_ref, lk2_ref, g_ref, q_ref, k_ref, v_ref,
                 o_ref, m_ref, l_ref, acc_ref, *, t_q, t_k, lambda_init):
    j = pl.program_id(1)
    head = lambda h: slice(h * LANE, (h + 1) * LANE)
    first_map = lax.broadcasted_iota(jnp.int32, (1, LANE), 1) < HEAD_DIM
    zero = jnp.zeros((), BF16)
    q2 = []
    for h in range(N_HEADS):
        q = q_ref[0, :, head(h)]
        q2.append(jnp.concatenate([jnp.where(first_map, q, zero), jnp.where(first_map, zero, q)], axis=0))
    m_ref[...] = jnp.full_like(m_ref, NEG_BIG)
    l_ref[...] = jnp.zeros_like(l_ref)
    acc_ref[...] = jnp.zeros_like(acc_ref)

    def step(c, width, causal):
        start = pl.multiple_of(c * t_k, t_k)
        for h in range(N_HEADS):
            k = k_ref[0, pl.ds(start, width), head(h)]
            v = v_ref[0, pl.ds(start, width), head(h)]
            s = lax.dot_general(q2[h], k, (((1,), (1,)), ((), ())), preferred_element_type=F32)
            if causal:
                qpos = lax.broadcasted_iota(jnp.int32, (2 * t_q, 1), 0) % t_q
                kpos = lax.broadcasted_iota(jnp.int32, (1, width), 1) - (width - t_k)
                s = jnp.where(kpos <= qpos, s, NEG_BIG)
            m_prev = m_ref[h]
            m_new = jnp.maximum(m_prev, jnp.max(s, axis=-1, keepdims=True))
            alpha = jnp.exp2(m_prev - m_new)
            p = jnp.exp2((s - jnp.concatenate([m_new] * (width // LANE), axis=1)).astype(BF16))
            l_ref[h] = alpha * l_ref[h] + jnp.sum(p.astype(F32), axis=-1, keepdims=True)
            acc_ref[h] = alpha * acc_ref[h] + jnp.dot(p, v, preferred_element_type=F32)
            m_ref[h] = m_new

    def pair(i, carry):
        step(2 * i, 2 * t_k, causal=False)
        return carry

    lax.fori_loop(0, j // 2, pair, 0)

    @pl.when(j % 2 == 1)
    def _():
        step(j - 1, 2 * t_k, causal=True)

    @pl.when(j % 2 == 0)
    def _():
        step(j, t_k, causal=True)

    lam = (jnp.exp(jnp.sum(lq1_ref[...] * lk1_ref[...], axis=-1, keepdims=True))
           - jnp.exp(jnp.sum(lq2_ref[...] * lk2_ref[...], axis=-1, keepdims=True)) + lambda_init)
    for h in range(N_HEADS):
        o = acc_ref[h] / l_ref[h]
        d = o[:t_q] - lam * o[t_q:]
        d = d * lax.rsqrt(jnp.mean(d * d, axis=-1, keepdims=True) + RMS_EPS) * g_ref[...]
        o_ref[0, :, head(h)] = (d * (1.0 - lambda_init)).astype(BF16)


def _attention(q, k, v, lq1, lk1, lq2, lk2, subln_g, lambda_init, cfg):
    tq = cfg.t_q
    width = N_HEADS * V_DIM
    small = lambda bb, j: (0, 0)
    q_spec = pl.BlockSpec((1, tq, width), lambda bb, j: (bb, j, 0))
    kv_spec = pl.BlockSpec((1, cfg.lp, width), lambda bb, j: (bb, 0, 0))
    return pl.pallas_call(
        functools.partial(_attn_kernel, t_q=tq, t_k=cfg.t_k, lambda_init=lambda_init),
        grid=(cfg.batch, cfg.lp // tq),
        in_specs=[pl.BlockSpec((1, HEAD_DIM), small)] * 4
        + [pl.BlockSpec((1, V_DIM), small), q_spec, kv_spec, kv_spec],
        out_specs=q_spec,
        out_shape=jax.ShapeDtypeStruct((cfg.batch, cfg.lp, width), BF16),
        scratch_shapes=[pltpu.VMEM((N_HEADS, 2 * tq, LANE), F32),
                        pltpu.VMEM((N_HEADS, 2 * tq, LANE), F32),
                        pltpu.VMEM((N_HEADS, 2 * tq, V_DIM), F32)],
        compiler_params=_params(("parallel", "arbitrary")),
        name="diff_attention",
    )(lq1, lk1, lq2, lk2, subln_g, q, k, v)


def _outproj_kernel(conv_ref, attn_ref, w_ref, h_ref, g_ref, b_ref, ho_ref, *, alpha):
    mix = (jnp.dot(conv_ref[...], w_ref[:D_CONV], preferred_element_type=F32)
           + jnp.dot(attn_ref[...], w_ref[D_CONV:], preferred_element_type=F32))
    ho_ref[...] = _layer_norm(alpha * h_ref[...] + mix, g_ref[...], b_ref[...])


def _outproj(conv, attn, w_out, h, g, b, cfg, alpha):
    tt = cfg.t_tok
    row = lambda i: (i, 0)
    const = lambda i: (0, 0)
    return pl.pallas_call(
        functools.partial(_outproj_kernel, alpha=alpha),
        grid=(cfg.tokens // tt,),
        in_specs=[pl.BlockSpec((tt, D_CONV), row),
                  pl.BlockSpec((tt, D_MODEL - D_CONV), row),
                  pl.BlockSpec((D_MODEL, D_MODEL), const),
                  pl.BlockSpec((tt, D_MODEL), row),
                  pl.BlockSpec((1, D_MODEL), const),
                  pl.BlockSpec((1, D_MODEL), const)],
        out_specs=pl.BlockSpec((tt, D_MODEL), row),
        out_shape=jax.ShapeDtypeStruct((cfg.tokens, D_MODEL), F32),
        compiler_params=_params(("parallel",)),
        name="outproj_ln",
    )(conv, attn, w_out, h, g, b)


def _swiglu_contrib(x, wg_ref, wu_ref, wd_ref):
    gate = jnp.dot(x, wg_ref[...], preferred_element_type=F32)
    up = jnp.dot(x, wu_ref[...], preferred_element_type=F32)
    act = gate * (1.0 / (1.0 + jnp.exp(-gate))) * up
    return jnp.dot(act.astype(BF16), wd_ref[...], preferred_element_type=F32)


def _ffn_kernel(wg_ref, wu_ref, wd_ref, h_ref, g_ref, b_ref, *rest, alpha, t_ff):
    if len(rest) == 3:
        cast_src_ref, ho_ref, cast_dst_ref = rest
        cast_dst_ref[...] = cast_src_ref[...].astype(BF16)
    else:
        ho_ref, = rest
    h = h_ref[...]
    x = h.astype(BF16)
    acc = None
    for c in range(wg_ref.shape[1] // t_ff):
        cols = pl.ds(c * t_ff, t_ff)
        contrib = _swiglu_contrib(x, wg_ref.at[:, cols], wu_ref.at[:, cols], wd_ref.at[cols, :])
        acc = contrib if acc is None else acc + contrib
    ho_ref[...] = _layer_norm(alpha * h + acc, g_ref[...], b_ref[...])


def _ffn(wg, wu, wd, h, g, b, cfg, alpha, cast=None):
    tt = cfg.t_tok
    steps = cfg.tokens // tt
    row = lambda i: (i, 0)
    const = lambda i: (0, 0)
    resident = lambda shape: pl.BlockSpec(shape, const, pipeline_mode=pl.Buffered(1))
    in_specs = [resident((D_MODEL, cfg.d_ff_dense)),
                resident((D_MODEL, cfg.d_ff_dense)),
                resident((cfg.d_ff_dense, D_MODEL)),
                pl.BlockSpec((tt, D_MODEL), row),
                pl.BlockSpec((1, D_MODEL), const),
                pl.BlockSpec((1, D_MODEL), const)]
    args = [wg, wu, wd, h, g, b]
    out_specs = [pl.BlockSpec((tt, D_MODEL), row)]
    out_shapes = [jax.ShapeDtypeStruct((cfg.tokens, D_MODEL), F32)]
    if cast is not None:
        flat, block, n_blocks = _cast_job(cast, steps)
        cast_spec = pl.BlockSpec(block, lambda i: (jnp.minimum(i, n_blocks - 1), 0))
        in_specs.append(cast_spec)
        args.append(flat)
        out_specs.append(cast_spec)
        out_shapes.append(jax.ShapeDtypeStruct(flat.shape, BF16))
    outs = pl.pallas_call(
        functools.partial(_ffn_kernel, alpha=alpha, t_ff=cfg.t_ff_dense),
        grid=(steps,),
        in_specs=in_specs,
        out_specs=out_specs,
        out_shape=out_shapes,
        compiler_params=_params(("arbitrary",)),
        name="dense_ffn_ln",
    )(*args)
    outs = list(outs)
    if cast is not None:
        outs[1] = outs[1].reshape(cast.shape)
    return outs


INFO_E1, INFO_E2, INFO_W1, INFO_W2, INFO_R1, INFO_R2 = range(6)
ROUTE_ROWS = 8


def _lane_pick(values, lane, index):
    return jnp.sum(jnp.where(lane == index, values, 0.0), axis=-1, keepdims=True)


def _router_kernel(h_ref, wr_ref, info_ref, route_ref, cnt_ref, seen_ref, *, n_experts):
    @pl.when(jnp.logical_and(pl.program_id(0) == 0, pl.program_id(1) == 0))
    def _():
        seen_ref[...] = jnp.zeros_like(seen_ref)

    h = h_ref[0]
    w = wr_ref[...]
    h_hi = h.astype(BF16)
    h_lo = (h - h_hi.astype(F32)).astype(BF16)
    w_hi = w.astype(BF16)
    w_lo = (w - w_hi.astype(F32)).astype(BF16)
    logits = (jnp.dot(h_hi, w_hi, preferred_element_type=F32)
              + jnp.dot(h_lo, w_hi, preferred_element_type=F32)
              + jnp.dot(h_hi, w_lo, preferred_element_type=F32))
    t_route = logits.shape[0]
    lane = lax.broadcasted_iota(jnp.int32, logits.shape, 1).astype(F32)
    logits = jnp.where(lane < n_experts, logits, -jnp.inf)
    v1 = jnp.max(logits, axis=-1, keepdims=True)
    e1 = jnp.min(jnp.where(logits == v1, lane, float(LANE)), axis=-1, keepdims=True)
    rest = jnp.where(lane == e1, -jnp.inf, logits)
    v2 = jnp.max(rest, axis=-1, keepdims=True)
    e2 = jnp.min(jnp.where(rest == v2, lane, float(LANE)), axis=-1, keepdims=True)
    ex = jnp.exp(v2 - v1)
    w1 = 1.0 / (1.0 + ex)
    w2 = ex / (1.0 + ex)

    chosen = jnp.where(lane == e1, 1.0, 0.0) + jnp.where(lane == e2, 1.0, 0.0)
    r = lax.broadcasted_iota(jnp.int32, (t_route, t_route), 0)
    c = lax.broadcasted_iota(jnp.int32, (t_route, t_route), 1)
    earlier = jnp.where(c < r, 1.0, 0.0).astype(BF16)
    before = jnp.dot(earlier, chosen.astype(BF16), preferred_element_type=F32) + seen_ref[...]
    r1 = _lane_pick(before, lane, e1)
    r2 = _lane_pick(before, lane, e2)
    seen_ref[...] += jnp.sum(chosen, axis=0, keepdims=True)
    cnt_ref[...] = seen_ref[...]

    record = jnp.zeros_like(logits)
    for slot, val in ((INFO_E1, e1), (INFO_E2, e2), (INFO_W1, w1), (INFO_W2, w2),
                      (INFO_R1, r1), (INFO_R2, r2)):
        record = jnp.where(lane == slot, val, record)
    info_ref[0] = record
    route_ref[0] = record.T[:ROUTE_ROWS]


def _seq_rows_spec(t_rows, width):
    return pl.BlockSpec((pl.Element(1), pl.Element(t_rows), pl.Element(width)),
                        lambda bb, j: (bb, pl.multiple_of(N_META + j * t_rows, N_META), 0))


def _router(h3, wr_pad, cfg):
    tr = cfg.t_route
    steps = cfg.seq // tr
    return pl.pallas_call(
        functools.partial(_router_kernel, n_experts=cfg.n_experts),
        grid=(cfg.batch, cfg.seq // tr),
        in_specs=[_seq_rows_spec(tr, D_MODEL),
                  pl.BlockSpec((D_MODEL, LANE), lambda bb, j: (0, 0))],
        out_specs=[pl.BlockSpec((1, tr, LANE), lambda bb, j: (bb, j, 0)),
                   pl.BlockSpec((1, ROUTE_ROWS, tr), lambda bb, j: (bb * steps + j, 0, 0)),
                   pl.BlockSpec((1, LANE), lambda bb, j: (0, 0))],
        out_shape=[jax.ShapeDtypeStruct((cfg.batch, cfg.seq, LANE), F32),
                   jax.ShapeDtypeStruct((cfg.batch * steps, ROUTE_ROWS, tr), F32),
                   jax.ShapeDtypeStruct((1, LANE), F32)],
        scratch_shapes=[pltpu.VMEM((1, LANE), F32)],
        compiler_params=_params(("arbitrary", "arbitrary")),
        name="router_top2",
    )(h3, wr_pad)


ZERO_ROWS = 8
ISSUE_UNROLL = 8


def _dispatch_kernel(pos_ref, fill_ref, h_ref, xs_ref, zero_ref, sem, *, t_route, n_experts):
    def zero_copy(p, n):
        start = pl.multiple_of(p * n, n)
        return pltpu.make_async_copy(zero_ref.at[pl.ds(0, n)], xs_ref.at[pl.ds(start, n)], sem.at[2])

    def zero_rows(lo, hi, n):
        def start(p, carry):
            zero_copy(p, n).start()
            return carry

        def wait(p, carry):
            zero_copy(p, n).wait()
            return carry

        lax.fori_loop(lo, hi, start, 0)
        lax.fori_loop(lo, hi, wait, 0)

    @pl.when(jnp.logical_and(pl.program_id(0) == 0, pl.program_id(1) == 0))
    def _():
        zero_ref[...] = jnp.zeros_like(zero_ref)
        for e in range(n_experts):
            zero_rows(fill_ref[0, e], fill_ref[1, e], 1)
        zero_rows(fill_ref[0, n_experts], fill_ref[1, n_experts], ZERO_ROWS)

    def issue(r, carry):
        for slot in range(TOP_K):
            pltpu.make_async_copy(h_ref.at[0, pl.ds(r, 1)],
                                  xs_ref.at[pl.ds(pos_ref[0, 0, slot * t_route + r], 1)],
                                  sem.at[slot]).start(priority=slot)
        return carry

    lax.fori_loop(0, t_route, issue, 0, unroll=ISSUE_UNROLL)
    for slot in range(TOP_K):
        pltpu.make_async_copy(h_ref.at[0], xs_ref.at[pl.ds(0, t_route)], sem.at[slot]).wait()


def _dispatch(pos, fill, h3, cfg, rows):
    tr = cfg.t_route
    steps = cfg.seq // tr
    return pl.pallas_call(
        functools.partial(_dispatch_kernel, t_route=tr, n_experts=cfg.n_experts),
        grid=(cfg.batch, steps),
        in_specs=[pl.BlockSpec((1, 1, TOP_K * tr), lambda bb, j: (bb * steps + j, 0, 0),
                               memory_space=pltpu.SMEM),
                  pl.BlockSpec(memory_space=pltpu.SMEM),
                  _seq_rows_spec(tr, D_MODEL)],
        out_specs=pl.BlockSpec(memory_space=pl.ANY),
        out_shape=jax.ShapeDtypeStruct((rows, D_MODEL), F32),
        scratch_shapes=[pltpu.VMEM((ZERO_ROWS, D_MODEL), F32), pltpu.SemaphoreType.DMA((3,))],
        compiler_params=_params(("arbitrary", "arbitrary")),
        name="moe_dispatch",
    )(pos, fill, h3)


def _grouped_kernel(te_ref, nv_ref, x_ref, wg_ref, wu_ref, wd_ref, y_ref, xb_ref, *, t_chunk):
    f = pl.program_id(1)

    @pl.when(pl.program_id(0) < nv_ref[0])
    def _():
        @pl.when(f == 0)
        def _():
            xb_ref[...] = x_ref[...].astype(BF16)

        x = xb_ref[...]
        acc = None
        for c in range(wg_ref.shape[2] // t_chunk):
            cols = pl.ds(c * t_chunk, t_chunk)
            contrib = _swiglu_contrib(x, wg_ref.at[0, :, cols], wu_ref.at[0, :, cols], wd_ref.at[0, cols, :])
            acc = contrib if acc is None else acc + contrib

        @pl.when(f == 0)
        def _():
            y_ref[...] = acc

        @pl.when(f > 0)
        def _():
            y_ref[...] += acc

    @pl.when(pl.program_id(0) >= nv_ref[0])
    def _():
        y_ref[...] = jnp.zeros_like(y_ref)


def _grouped_ffn(tile_expert, n_valid, xs, wg, wu, wd, cfg):
    tg, tf = cfg.t_group, cfg.t_ff_expert
    n_f = cfg.d_ff_expert // tf
    rows = xs.shape[0]

    def live(i, nv):
        return jnp.minimum(i, nv[0] - 1)

    def f_idx(i, f, nv):
        return jnp.where(i < nv[0], f, n_f - 1)

    grid_spec = pltpu.PrefetchScalarGridSpec(
        num_scalar_prefetch=2,
        grid=(rows // tg, n_f),
        in_specs=[pl.BlockSpec((tg, D_MODEL), lambda i, f, te, nv: (live(i, nv), 0)),
                  pl.BlockSpec((1, D_MODEL, tf), lambda i, f, te, nv: (te[i], 0, f_idx(i, f, nv))),
                  pl.BlockSpec((1, D_MODEL, tf), lambda i, f, te, nv: (te[i], 0, f_idx(i, f, nv))),
                  pl.BlockSpec((1, tf, D_MODEL), lambda i, f, te, nv: (te[i], f_idx(i, f, nv), 0))],
        out_specs=pl.BlockSpec((tg, D_MODEL), lambda i, f, te, nv: (i, 0)),
        scratch_shapes=[pltpu.VMEM((tg, D_MODEL), BF16)],
    )
    return pl.pallas_call(
        functools.partial(_grouped_kernel, t_chunk=cfg.t_ff_dense),
        grid_spec=grid_spec,
        out_shape=jax.ShapeDtypeStruct((rows, D_MODEL), F32),
        compiler_params=_params(("arbitrary", "arbitrary")),
        name="moe_grouped_ffn",
    )(tile_expert, n_valid, xs, wg, wu, wd)


def _combine_kernel(pos_ref, next_pos_ref, info_ref, h_ref, g_ref, b_ref, ys_ref, out_ref, rows_ref, sem,
                    *, alpha, t_route):
    step = pl.program_id(0) * pl.num_programs(1) + pl.program_id(1)
    last = pl.num_programs(0) * pl.num_programs(1) - 1
    buf = step % 2

    def gather(tile_pos_ref, into):
        def issue(r, carry):
            for slot in range(TOP_K):
                pltpu.make_async_copy(ys_ref.at[pl.ds(tile_pos_ref[0, 0, slot * t_route + r], 1)],
                                      rows_ref.at[into, slot, pl.ds(r, 1)],
                                      sem.at[into, slot]).start(priority=slot)
            return carry

        lax.fori_loop(0, t_route, issue, 0, unroll=ISSUE_UNROLL)

    @pl.when(step == 0)
    def _():
        gather(pos_ref, 0)

    @pl.when(step < last)
    def _():
        gather(next_pos_ref, 1 - buf)

    for slot in range(TOP_K):
        pltpu.make_async_copy(ys_ref.at[pl.ds(0, t_route)], rows_ref.at[buf, slot], sem.at[buf, slot]).wait()

    info = info_ref[0]
    lane = lax.broadcasted_iota(jnp.int32, info.shape, 1)
    ffn = (_lane_pick(info, lane, INFO_W1) * rows_ref[buf, 0]
           + _lane_pick(info, lane, INFO_W2) * rows_ref[buf, 1])
    out_ref[0] = _layer_norm(alpha * h_ref[0] + ffn, g_ref[...], b_ref[...])


def _combine(pos, info, h3, g, b, ys, cfg, alpha):
    tr = cfg.t_route
    steps = cfg.seq // tr
    n_tiles = cfg.batch * steps
    const = lambda bb, j: (0, 0)

    def pos_spec(ahead):
        return pl.BlockSpec((1, 1, TOP_K * tr),
                            lambda bb, j: (jnp.minimum(bb * steps + j + ahead, n_tiles - 1), 0, 0),
                            memory_space=pltpu.SMEM)

    return pl.pallas_call(
        functools.partial(_combine_kernel, alpha=alpha, t_route=tr),
        grid=(cfg.batch, steps),
        in_specs=[pos_spec(0), pos_spec(1),
                  pl.BlockSpec((1, tr, LANE), lambda bb, j: (bb, j, 0)),
                  _seq_rows_spec(tr, D_MODEL),
                  pl.BlockSpec((1, D_MODEL), const),
                  pl.BlockSpec((1, D_MODEL), const),
                  pl.BlockSpec(memory_space=pl.ANY)],
        out_specs=pl.BlockSpec((1, tr, D_MODEL), lambda bb, j: (bb, j, 0)),
        out_shape=jax.ShapeDtypeStruct((cfg.batch, cfg.seq, D_MODEL), F32),
        scratch_shapes=[pltpu.VMEM((2, TOP_K, tr, D_MODEL), F32), pltpu.SemaphoreType.DMA((2, TOP_K))],
        compiler_params=_params(("arbitrary", "arbitrary")),
        name="moe_combine_ln",
    )(pos, pos, info, h3, g, b, ys)


def _routed_moe(h3, w_router, wg, wu, wd, g, b, cfg, alpha):
    n_e, tg, tr = cfg.n_experts, cfg.t_group, cfg.t_route
    n_tok = cfg.batch * cfg.seq
    rows = TOP_K * n_tok + n_e * tg
    wr_pad = jnp.pad(w_router, ((0, 0), (0, LANE - n_e)))
    info, route, seen = _router(h3, wr_pad, cfg)

    as_int = lambda row: route[:, row, :].astype(jnp.int32)
    counts = seen[0, :n_e].astype(jnp.int32)
    padded = (counts + tg - 1) // tg * tg
    ends = jnp.cumsum(padded)
    starts = ends - padded
    start_of = lambda e: jnp.sum(jnp.where(e[..., None] == jnp.arange(n_e), starts, 0), axis=-1)
    pos = jnp.stack([start_of(as_int(INFO_E1)) + as_int(INFO_R1),
                     start_of(as_int(INFO_E2)) + as_int(INFO_R2)], axis=1)
    pos = pos.reshape(n_tok // tr, 1, TOP_K * tr)
    fill = jnp.stack([jnp.append(starts + counts, ends[-1] // ZERO_ROWS),
                      jnp.append(ends, rows // ZERO_ROWS)])
    n_valid = (ends[-1] // tg).reshape(1)
    tile_start = jnp.arange(rows // tg, dtype=jnp.int32) * tg
    tile_expert = jnp.sum(tile_start[:, None] >= ends[None, :], axis=-1).astype(jnp.int32)
    tile_expert = tile_expert[jnp.minimum(jnp.arange(rows // tg), n_valid[0] - 1)]

    xs = _dispatch(pos, fill, h3, cfg, rows)
    ys = _grouped_ffn(tile_expert, n_valid, xs, wg, wu, wd, cfg)
    return _combine(pos, info, h3, g, b, ys, cfg, alpha)


def _rope_tables(lp):
    half = HEAD_DIM // 2
    inv = 1.0 / (ROPE_THETA ** (jnp.arange(0, HEAD_DIM, 2, dtype=F32) / HEAD_DIM))
    pos = jnp.arange(lp, dtype=F32)
    ang = pos[:, None] * inv[None, :]
    ang = jnp.tile(ang, (1, LANE // half))
    sign = jnp.where((jnp.arange(LANE) % HEAD_DIM) < half, -1.0, 1.0).astype(F32)
    return jnp.cos(ang), jnp.sin(ang) * sign[None, :]


def _forward(cfg, x, meta_tokens, ln_emb_g, ln_emb_b, w_in, conv_w, lambda_q1, lambda_k1,
             lambda_q2, lambda_k2, subln_g, w_out, ln_mix_g, ln_mix_b, ln_ffn_g, ln_ffn_b,
             w_gate_dense, w_up_dense, w_down_dense, w_router, w_gate_moe, w_up_moe, w_down_moe):
    alpha = (2 * cfg.depth) ** 0.25
    tokens = cfg.tokens
    vec = lambda a: a.reshape(1, -1)
    cos, sin = _rope_tables(cfg.lp)
    h = _embed(x, meta_tokens, vec(ln_emb_g), vec(ln_emb_b), cfg).reshape(tokens, D_MODEL)
    moe_bf16 = {}
    for layer in range(cfg.depth):
        lambda_init = 0.8 - 0.6 * math.exp(-0.3 * layer)
        dense = layer % 2 == 0
        idx = layer // 2
        feeds_moe = layer + 1 < cfg.depth if dense else True
        side = (w_gate_moe if dense else w_up_moe)[idx] if feeds_moe else None
        conv, q, k, v, *cast = _inproj(h.reshape(cfg.batch, cfg.lp, D_MODEL), w_in[layer].astype(BF16),
                                       conv_w[layer], cos, sin, cfg, cast=side)
        if feeds_moe:
            moe_bf16["gate" if dense else "up"] = cast[0]
        attn = _attention(q, k, v, vec(lambda_q1[layer]), vec(lambda_k1[layer]),
                          vec(lambda_q2[layer]), vec(lambda_k2[layer]), vec(subln_g[layer]),
                          lambda_init, cfg)
        h = _outproj(conv.reshape(tokens, D_CONV), attn.reshape(tokens, D_MODEL - D_CONV),
                     w_out[layer].astype(BF16), h, vec(ln_mix_g[layer]), vec(ln_mix_b[layer]), cfg, alpha)
        g, b = vec(ln_ffn_g[layer]), vec(ln_ffn_b[layer])
        if dense:
            h, *cast = _ffn(w_gate_dense[idx].astype(BF16), w_up_dense[idx].astype(BF16),
                            w_down_dense[idx].astype(BF16), h, g, b, cfg, alpha,
                            cast=w_down_moe[idx] if feeds_moe else None)
            if feeds_moe:
                moe_bf16["down"] = cast[0]
        else:
            assert layer == cfg.depth - 1
            return _routed_moe(h.reshape(cfg.batch, cfg.lp, D_MODEL), w_router[idx],
                               moe_bf16["gate"], moe_bf16["up"], moe_bf16["down"], g, b, cfg, alpha)
    return h.reshape(cfg.batch, cfg.lp, D_MODEL)[:, N_META:N_META + cfg.seq]


_CFG = Cfg(batch=8, seq=4096, depth=2, d_ff_dense=2816, d_ff_expert=3584, n_experts=8,
           t_embed=128, n_embed=11, t_proj=1056, t_q=384, t_k=384, t_tok=1024, t_ff_dense=256, t_ff_expert=3584,
           t_route=1024, t_group=512)


def kernel(x, meta_tokens, ln_emb_g, ln_emb_b, w_in, conv_w, lambda_q1, lambda_k1, lambda_q2, lambda_k2, subln_g, w_out, ln_mix_g, ln_mix_b, ln_ffn_g, ln_ffn_b, w_gate_dense, w_up_dense, w_down_dense, w_router, w_gate_moe, w_up_moe, w_down_moe):
    return _forward(_CFG, x, meta_tokens, ln_emb_g, ln_emb_b, w_in, conv_w, lambda_q1, lambda_k1,
                    lambda_q2, lambda_k2, subln_g, w_out, ln_mix_g, ln_mix_b, ln_ffn_g, ln_ffn_b,
                    w_gate_dense, w_up_dense, w_down_dense, w_router, w_gate_moe, w_up_moe, w_down_moe)
```

```python
import dataclasses
import functools
import math

import jax
import jax.numpy as jnp
from jax import lax
from jax.experimental import pallas as pl
from jax.experimental.pallas import tpu as pltpu

F32 = jnp.float32
BF16 = jnp.bfloat16

LANE = 128
D_MODEL = 1024
N_META = 16
PAD = LANE - N_META
D_CONV = D_MODEL // 2
CONV_WIDTH = 3
HEAD_DIM = 64
N_HEADS = 4
V_DIM = 2 * HEAD_DIM
QK_W = N_HEADS * 2 * HEAD_DIM
GROUP_W = 512
ROPE_THETA = 10000.0
TOP_K = 2
LN_EPS = 1e-5
RMS_EPS = 1e-5
NEG_BIG = -1e30
VMEM_LIMIT = 56 * 1024 * 1024


@dataclasses.dataclass(frozen=True)
class Cfg:
    batch: int
    seq: int
    depth: int
    d_ff_dense: int
    d_ff_expert: int
    n_experts: int
    t_embed: int
    n_embed: int
    t_proj: int
    t_q: int
    t_k: int
    t_tok: int
    t_ff_dense: int
    t_ff_expert: int
    t_route: int
    t_group: int

    @property
    def lp(self):
        return N_META + self.seq + PAD

    @property
    def tokens(self):
        return self.batch * self.lp


def _params(semantics):
    return pltpu.CompilerParams(dimension_semantics=semantics, vmem_limit_bytes=VMEM_LIMIT)


def _layer_norm(y, g, b):
    mu = jnp.mean(y, axis=-1, keepdims=True)
    yc = y - mu
    var = jnp.mean(yc * yc, axis=-1, keepdims=True)
    return yc * lax.rsqrt(var + LN_EPS) * g + b


def _cast_job(w, n_steps):
    flat = w.reshape(-1, w.shape[-1])
    n_blocks = 1 << (n_steps.bit_length() - 1)
    rows = flat.shape[0] // n_blocks
    assert rows * n_blocks == flat.shape[0] and rows % 16 == 0, (w.shape, n_steps)
    return flat, (rows, flat.shape[1]), n_blocks


def _embed_kernel(*refs, seq, n_blocks):
    x_refs, (meta_ref, g_ref, b_ref, h_ref) = refs[:n_blocks + 1], refs[n_blocks + 1:]
    j = pl.program_id(1)
    te = x_refs[1].shape[1]
    head = jnp.where(j == 0, meta_ref[...], x_refs[0][0, te - N_META:])
    rows = jnp.concatenate([head] + [r[0] for r in x_refs[1:n_blocks]] + [x_refs[n_blocks][0, :te - N_META]],
                           axis=0)
    rid = j * (n_blocks * te) + lax.broadcasted_iota(jnp.int32, (n_blocks * te, 1), 0)
    rows = jnp.where(rid < N_META + seq, rows, 0.0)
    h_ref[0] = _layer_norm(rows, g_ref[...], b_ref[...])


def _embed(x, meta_tokens, g, b, cfg):
    te, n = cfg.t_embed, cfg.n_embed
    last = cfg.seq // te - 1

    def x_spec(i):
        return pl.BlockSpec((1, te, D_MODEL), lambda bb, j: (bb, jnp.clip(n * j + i - 1, 0, last), 0))

    const = lambda bb, j: (0, 0)
    return pl.pallas_call(
        functools.partial(_embed_kernel, seq=cfg.seq, n_blocks=n),
        grid=(cfg.batch, cfg.lp // (n * te)),
        in_specs=[x_spec(i) for i in range(n + 1)]
        + [pl.BlockSpec((N_META, D_MODEL), const),
           pl.BlockSpec((1, D_MODEL), const),
           pl.BlockSpec((1, D_MODEL), const)],
        out_specs=pl.BlockSpec((1, n * te, D_MODEL), lambda bb, j: (bb, j, 0)),
        out_shape=jax.ShapeDtypeStruct((cfg.batch, cfg.lp, D_MODEL), F32),
        compiler_params=_params(("parallel", "arbitrary")),
        name="embed_ln",
    )(*([x] * (n + 1)), meta_tokens, g, b)


def _inproj_kernel(*refs, t_proj, side_cast):
    if side_cast:
        (h_ref, w_ref, cw_ref, cos_ref, sin_ref, cast_src_ref,
         conv_ref, q_ref, k_ref, v_ref, cast_dst_ref, carry_ref) = refs
        cast_dst_ref[...] = cast_src_ref[...].astype(BF16)
    else:
        h_ref, w_ref, cw_ref, cos_ref, sin_ref, conv_ref, q_ref, k_ref, v_ref, carry_ref = refs
    j = pl.program_id(1)
    x = h_ref[0].astype(BF16)

    def proj(g):
        return jnp.dot(x, w_ref[:, g * GROUP_W:(g + 1) * GROUP_W], preferred_element_type=F32)

    rid = lax.broadcasted_iota(jnp.int32, (t_proj, 1), 0)
    gated = proj(1) * proj(2)

    @pl.when(j == 0)
    def _():
        carry_ref[...] = jnp.zeros_like(carry_ref)

    prev = carry_ref[...]
    back1 = jnp.where(rid == 0, prev[7:8], pltpu.roll(gated, 1, 0))
    back2 = jnp.where(rid == 0, prev[6:7], jnp.where(rid == 1, prev[7:8], pltpu.roll(gated, 2, 0)))
    carry_ref[...] = gated[t_proj - 8:]
    cw = cw_ref[...]
    conv = cw[0:1] * back2 + cw[1:2] * back1 + cw[2:3] * gated
    conv_ref[0] = (proj(0) * conv).astype(BF16)

    cos = cos_ref[...]
    sin = sin_ref[...]
    lane = lax.broadcasted_iota(jnp.int32, (1, LANE), 1)
    low_half = (lane % HEAD_DIM) < HEAD_DIM // 2

    def rope(z):
        partner = jnp.where(low_half, pltpu.roll(z, LANE - HEAD_DIM // 2, 1),
                            pltpu.roll(z, HEAD_DIM // 2, 1))
        return z * cos + partner * sin

    zq = proj(3)
    zk = proj(4)
    scale = HEAD_DIM ** -0.5 * math.log2(math.e)
    for h in range(N_HEADS):
        sl = slice(h * LANE, (h + 1) * LANE)
        q_ref[0, :, sl] = (rope(zq[:, sl]) * scale).astype(BF16)
        k_ref[0, :, sl] = rope(zk[:, sl]).astype(BF16)
    v_ref[0] = proj(5).astype(BF16)


def _inproj(h, w_in, conv_w, cos, sin, cfg, cast=None):
    tp = cfg.t_proj
    steps = cfg.lp // tp
    row = lambda bb, j: (bb, j, 0)
    const = lambda bb, j: (0, 0)
    out_spec = pl.BlockSpec((1, tp, GROUP_W), row)
    out_shape = jax.ShapeDtypeStruct((cfg.batch, cfg.lp, GROUP_W), BF16)
    in_specs = [pl.BlockSpec((1, tp, D_MODEL), row),
                pl.BlockSpec((D_MODEL, 6 * GROUP_W), const),
                pl.BlockSpec((CONV_WIDTH, D_CONV), const),
                pl.BlockSpec((tp, LANE), lambda bb, j: (j, 0)),
                pl.BlockSpec((tp, LANE), lambda bb, j: (j, 0))]
    args, out_specs, out_shapes = [h, w_in, conv_w, cos, sin], [out_spec] * 4, [out_shape] * 4
    if cast is not None:
        flat, block, n_blocks = _cast_job(cast, cfg.batch * steps)
        cast_spec = pl.BlockSpec(block, lambda bb, j: (jnp.minimum(bb * steps + j, n_blocks - 1), 0))
        in_specs.append(cast_spec)
        args.append(flat)
        out_specs.append(cast_spec)
        out_shapes.append(jax.ShapeDtypeStruct(flat.shape, BF16))
    outs = pl.pallas_call(
        functools.partial(_inproj_kernel, t_proj=tp, side_cast=cast is not None),
        grid=(cfg.batch, steps),
        in_specs=in_specs,
        out_specs=out_specs,
        out_shape=out_shapes,
        scratch_shapes=[pltpu.VMEM((8, D_CONV), F32)],
        compiler_params=_params(("arbitrary", "arbitrary")),
        name="inproj_conv_rope",
    )(*args)
    outs = list(outs)
    if cast is not None:
        outs[4] = outs[4].reshape(cast.shape)
    return outs


def _attn_kernel(lq1_ref, lk1_ref, lq2_ref, lk2_ref, g_ref, q_ref, k_ref, v_ref,
                 o_ref, m_ref, l_ref, acc_ref, *, t_q, t_k, lambda_init):
    j = pl.program_id(1)
    head = lambda h: slice(h * LANE, (h + 1) * LANE)
    first_map = lax.broadcasted_iota(jnp.int32, (1, LANE), 1) < HEAD_DIM
    zero = jnp.zeros((), BF16)
    q2 = []
    for h in range(N_HEADS):
        q = q_ref[0, :, head(h)]
        q2.append(jnp.concatenate([jnp.where(first_map, q, zero), jnp.where(first_map, zero, q)], axis=0))
    m_ref[...] = jnp.full_like(m_ref, NEG_BIG)
    l_ref[...] = jnp.zeros_like(l_ref)
    acc_ref[...] = jnp.zeros_like(acc_ref)

    def step(c, width, causal):
        start = pl.multiple_of(c * t_k, t_k)
        for h in range(N_HEADS):
            k = k_ref[0, pl.ds(start, width), head(h)]
            v = v_ref[0, pl.ds(start, width), head(h)]
            s = lax.dot_general(q2[h], k, (((1,), (1,)), ((), ())), preferred_element_type=F32)
            if causal:
                qpos = lax.broadcasted_iota(jnp.int32, (2 * t_q, 1), 0) % t_q
                kpos = lax.broadcasted_iota(jnp.int32, (1, width), 1) - (width - t_k)
                s = jnp.where(kpos <= qpos, s, NEG_BIG)
            m_prev = m_ref[h]
            m_new = jnp.maximum(m_prev, jnp.max(s, axis=-1, keepdims=True))
            alpha = jnp.exp2(m_prev - m_new)
            p = jnp.exp2((s - jnp.concatenate([m_new] * (width // LANE), axis=1)).astype(BF16))
            l_ref[h] = alpha * l_ref[h] + jnp.sum(p.astype(F32), axis=-1, keepdims=True)
            acc_ref[h] = alpha * acc_ref[h] + jnp.dot(p, v, preferred_element_type=F32)
            m_ref[h] = m_new

    def two_pairs(i, carry):
        step(4 * i, 2 * t_k, causal=False)
        step(4 * i + 2, 2 * t_k, causal=False)
        return carry

    lax.fori_loop(0, j // 4, two_pairs, 0)

    @pl.when((j // 2) % 2 == 1)
    def _():
        step((j // 4) * 4, 2 * t_k, causal=False)

    @pl.when(j % 2 == 1)
    def _():
        step(j - 1, 2 * t_k, causal=True)

    @pl.when(j % 2 == 0)
    def _():
        step(j, t_k, causal=True)

    lam = (jnp.exp(jnp.sum(lq1_ref[...] * lk1_ref[...], axis=-1, keepdims=True))
           - jnp.exp(jnp.sum(lq2_ref[...] * lk2_ref[...], axis=-1, keepdims=True)) + lambda_init)
    for h in range(N_HEADS):
        o = acc_ref[h] / l_ref[h]
        d = o[:t_q] - lam * o[t_q:]
        d = d * lax.rsqrt(jnp.mean(d * d, axis=-1, keepdims=True) + RMS_EPS) * g_ref[...]
        o_ref[0, :, head(h)] = (d * (1.0 - lambda_init)).astype(BF16)


def _attention(q, k, v, lq1, lk1, lq2, lk2, subln_g, lambda_init, cfg):
    tq = cfg.t_q
    width = N_HEADS * V_DIM
    small = lambda bb, j: (0, 0)
    q_spec = pl.BlockSpec((1, tq, width), lambda bb, j: (bb, j, 0))
    kv_spec = pl.BlockSpec((1, cfg.lp, width), lambda bb, j: (bb, 0, 0))
    return pl.pallas_call(
        functools.partial(_attn_kernel, t_q=tq, t_k=cfg.t_k, lambda_init=lambda_init),
        grid=(cfg.batch, cfg.lp // tq),
        in_specs=[pl.BlockSpec((1, HEAD_DIM), small)] * 4
        + [pl.BlockSpec((1, V_DIM), small), q_spec, kv_spec, kv_spec],
        out_specs=q_spec,
        out_shape=jax.ShapeDtypeStruct((cfg.batch, cfg.lp, width), BF16),
        scratch_shapes=[pltpu.VMEM((N_HEADS, 2 * tq, LANE), F32),
                        pltpu.VMEM((N_HEADS, 2 * tq, LANE), F32),
                        pltpu.VMEM((N_HEADS, 2 * tq, V_DIM), F32)],
        compiler_params=_params(("parallel", "arbitrary")),
        name="diff_attention",
    )(lq1, lk1, lq2, lk2, subln_g, q, k, v)


def _outproj_kernel(conv_ref, attn_ref, w_ref, h_ref, g_ref, b_ref, ho_ref, *, alpha):
    mix = (jnp.dot(conv_ref[...], w_ref[:D_CONV], preferred_element_type=F32)
           + jnp.dot(attn_ref[...], w_ref[D_CONV:], preferred_element_type=F32))
    ho_ref[...] = _layer_norm(alpha * h_ref[...] + mix, g_ref[...], b_ref[...])


def _outproj(conv, attn, w_out, h, g, b, cfg, alpha):
    tt = cfg.t_tok
    row = lambda i: (i, 0)
    const = lambda i: (0, 0)
    return pl.pallas_call(
        functools.partial(_outproj_kernel, alpha=alpha),
        grid=(cfg.tokens // tt,),
        in_specs=[pl.BlockSpec((tt, D_CONV), row),
                  pl.BlockSpec((tt, D_MODEL - D_CONV), row),
                  pl.BlockSpec((D_MODEL, D_MODEL), const),
                  pl.BlockSpec((tt, D_MODEL), row),
                  pl.BlockSpec((1, D_MODEL), const),
                  pl.BlockSpec((1, D_MODEL), const)],
        out_specs=pl.BlockSpec((tt, D_MODEL), row),
        out_shape=jax.ShapeDtypeStruct((cfg.tokens, D_MODEL), F32),
        compiler_params=_params(("parallel",)),
        name="outproj_ln",
    )(conv, attn, w_out, h, g, b)


def _swiglu_contrib(x, wg_ref, wu_ref, wd_ref):
    gate = jnp.dot(x, wg_ref[...], preferred_element_type=F32)
    up = jnp.dot(x, wu_ref[...], preferred_element_type=F32)
    act = gate * (1.0 / (1.0 + jnp.exp(-gate))) * up
    return jnp.dot(act.astype(BF16), wd_ref[...], preferred_element_type=F32)


def _ffn_kernel(wg_ref, wu_ref, wd_ref, h_ref, g_ref, b_ref, *rest, alpha, t_ff):
    if len(rest) == 3:
        cast_src_ref, ho_ref, cast_dst_ref = rest
        cast_dst_ref[...] = cast_src_ref[...].astype(BF16)
    else:
        ho_ref, = rest
    h = h_ref[...]
    x = h.astype(BF16)
    acc = None
    for c in range(wg_ref.shape[1] // t_ff):
        cols = pl.ds(c * t_ff, t_ff)
        contrib = _swiglu_contrib(x, wg_ref.at[:, cols], wu_ref.at[:, cols], wd_ref.at[cols, :])
        acc = contrib if acc is None else acc + contrib
    ho_ref[...] = _layer_norm(alpha * h + acc, g_ref[...], b_ref[...])


def _ffn(wg, wu, wd, h, g, b, cfg, alpha, cast=None):
    tt = cfg.t_tok
    steps = cfg.tokens // tt
    row = lambda i: (i, 0)
    const = lambda i: (0, 0)
    resident = lambda shape: pl.BlockSpec(shape, const, pipeline_mode=pl.Buffered(1))
    in_specs = [resident((D_MODEL, cfg.d_ff_dense)),
                resident((D_MODEL, cfg.d_ff_dense)),
                resident((cfg.d_ff_dense, D_MODEL)),
                pl.BlockSpec((tt, D_MODEL), row),
                pl.BlockSpec((1, D_MODEL), const),
                pl.BlockSpec((1, D_MODEL), const)]
    args = [wg, wu, wd, h, g, b]
    out_specs = [pl.BlockSpec((tt, D_MODEL), row)]
    out_shapes = [jax.ShapeDtypeStruct((cfg.tokens, D_MODEL), F32)]
    if cast is not None:
        flat, block, n_blocks = _cast_job(cast, steps)
        cast_spec = pl.BlockSpec(block, lambda i: (jnp.minimum(i, n_blocks - 1), 0))
        in_specs.append(cast_spec)
        args.append(flat)
        out_specs.append(cast_spec)
        out_shapes.append(jax.ShapeDtypeStruct(flat.shape, BF16))
    outs = pl.pallas_call(
        functools.partial(_ffn_kernel, alpha=alpha, t_ff=cfg.t_ff_dense),
        grid=(steps,),
        in_specs=in_specs,
        out_specs=out_specs,
        out_shape=out_shapes,
        compiler_params=_params(("arbitrary",)),
        name="dense_ffn_ln",
    )(*args)
    outs = list(outs)
    if cast is not None:
        outs[1] = outs[1].reshape(cast.shape)
    return outs


INFO_E1, INFO_E2, INFO_W1, INFO_W2, INFO_R1, INFO_R2 = range(6)
ROUTE_ROWS = 8


def _lane_pick(values, lane, index):
    return jnp.sum(jnp.where(lane == index, values, 0.0), axis=-1, keepdims=True)


def _router_kernel(h_ref, wr_ref, info_ref, route_ref, cnt_ref, seen_ref, *, n_experts):
    @pl.when(jnp.logical_and(pl.program_id(0) == 0, pl.program_id(1) == 0))
    def _():
        seen_ref[...] = jnp.zeros_like(seen_ref)

    h = h_ref[0]
    w = wr_ref[...]
    h_hi = h.astype(BF16)
    h_lo = (h - h_hi.astype(F32)).astype(BF16)
    w_hi = w.astype(BF16)
    w_lo = (w - w_hi.astype(F32)).astype(BF16)
    logits = (jnp.dot(h_hi, w_hi, preferred_element_type=F32)
              + jnp.dot(h_lo, w_hi, preferred_element_type=F32)
              + jnp.dot(h_hi, w_lo, preferred_element_type=F32))
    t_route = logits.shape[0]
    lane = lax.broadcasted_iota(jnp.int32, logits.shape, 1).astype(F32)
    logits = jnp.where(lane < n_experts, logits, -jnp.inf)
    v1 = jnp.max(logits, axis=-1, keepdims=True)
    e1 = jnp.min(jnp.where(logits == v1, lane, float(LANE)), axis=-1, keepdims=True)
    rest = jnp.where(lane == e1, -jnp.inf, logits)
    v2 = jnp.max(rest, axis=-1, keepdims=True)
    e2 = jnp.min(jnp.where(rest == v2, lane, float(LANE)), axis=-1, keepdims=True)
    ex = jnp.exp(v2 - v1)
    w1 = 1.0 / (1.0 + ex)
    w2 = ex / (1.0 + ex)

    chosen = jnp.where(lane == e1, 1.0, 0.0) + jnp.where(lane == e2, 1.0, 0.0)
    r = lax.broadcasted_iota(jnp.int32, (t_route, t_route), 0)
    c = lax.broadcasted_iota(jnp.int32, (t_route, t_route), 1)
    earlier = jnp.where(c < r, 1.0, 0.0).astype(BF16)
    before = jnp.dot(earlier, chosen.astype(BF16), preferred_element_type=F32) + seen_ref[...]
    r1 = _lane_pick(before, lane, e1)
    r2 = _lane_pick(before, lane, e2)
    seen_ref[...] += jnp.sum(chosen, axis=0, keepdims=True)
    cnt_ref[...] = seen_ref[...]

    record = jnp.zeros_like(logits)
    for slot, val in ((INFO_E1, e1), (INFO_E2, e2), (INFO_W1, w1), (INFO_W2, w2),
                      (INFO_R1, r1), (INFO_R2, r2)):
        record = jnp.where(lane == slot, val, record)
    info_ref[0] = record
    route_ref[0] = record.T[:ROUTE_ROWS]


def _seq_rows_spec(t_rows, width):
    return pl.BlockSpec((pl.Element(1), pl.Element(t_rows), pl.Element(width)),
                        lambda bb, j: (bb, pl.multiple_of(N_META + j * t_rows, N_META), 0))


def _router(h3, wr_pad, cfg):
    tr = cfg.t_route
    steps = cfg.seq // tr
    return pl.pallas_call(
        functools.partial(_router_kernel, n_experts=cfg.n_experts),
        grid=(cfg.batch, cfg.seq // tr),
        in_specs=[_seq_rows_spec(tr, D_MODEL),
                  pl.BlockSpec((D_MODEL, LANE), lambda bb, j: (0, 0))],
        out_specs=[pl.BlockSpec((1, tr, LANE), lambda bb, j: (bb, j, 0)),
                   pl.BlockSpec((1, ROUTE_ROWS, tr), lambda bb, j: (bb * steps + j, 0, 0)),
                   pl.BlockSpec((1, LANE), lambda bb, j: (0, 0))],
        out_shape=[jax.ShapeDtypeStruct((cfg.batch, cfg.seq, LANE), F32),
                   jax.ShapeDtypeStruct((cfg.batch * steps, ROUTE_ROWS, tr), F32),
                   jax.ShapeDtypeStruct((1, LANE), F32)],
        scratch_shapes=[pltpu.VMEM((1, LANE), F32)],
        compiler_params=_params(("arbitrary", "arbitrary")),
        name="router_top2",
    )(h3, wr_pad)


ZERO_ROWS = 8
ISSUE_UNROLL = 8


def _dispatch_kernel(pos_ref, fill_ref, h_ref, xs_ref, zero_ref, sem, *, t_route, n_experts):
    def zero_copy(p, n):
        start = pl.multiple_of(p * n, n)
        return pltpu.make_async_copy(zero_ref.at[pl.ds(0, n)], xs_ref.at[pl.ds(start, n)], sem.at[2])

    def zero_rows(lo, hi, n):
        def start(p, carry):
            zero_copy(p, n).start()
            return carry

        def wait(p, carry):
            zero_copy(p, n).wait()
            return carry

        lax.fori_loop(lo, hi, start, 0)
        lax.fori_loop(lo, hi, wait, 0)

    @pl.when(jnp.logical_and(pl.program_id(0) == 0, pl.program_id(1) == 0))
    def _():
        zero_ref[...] = jnp.zeros_like(zero_ref)
        for e in range(n_experts):
            zero_rows(fill_ref[0, e], fill_ref[1, e], 1)
        zero_rows(fill_ref[0, n_experts], fill_ref[1, n_experts], ZERO_ROWS)

    def issue(r, carry):
        for slot in range(TOP_K):
            pltpu.make_async_copy(h_ref.at[0, pl.ds(r, 1)],
                                  xs_ref.at[pl.ds(pos_ref[0, 0, slot * t_route + r], 1)],
                                  sem.at[slot]).start(priority=slot)
        return carry

    lax.fori_loop(0, t_route, issue, 0, unroll=ISSUE_UNROLL)
    for slot in range(TOP_K):
        pltpu.make_async_copy(h_ref.at[0], xs_ref.at[pl.ds(0, t_route)], sem.at[slot]).wait()


def _dispatch(pos, fill, h3, cfg, rows):
    tr = cfg.t_route
    steps = cfg.seq // tr
    return pl.pallas_call(
        functools.partial(_dispatch_kernel, t_route=tr, n_experts=cfg.n_experts),
        grid=(cfg.batch, steps),
        in_specs=[pl.BlockSpec((1, 1, TOP_K * tr), lambda bb, j: (bb * steps + j, 0, 0),
                               memory_space=pltpu.SMEM),
                  pl.BlockSpec(memory_space=pltpu.SMEM),
                  _seq_rows_spec(tr, D_MODEL)],
        out_specs=pl.BlockSpec(memory_space=pl.ANY),
        out_shape=jax.ShapeDtypeStruct((rows, D_MODEL), F32),
        scratch_shapes=[pltpu.VMEM((ZERO_ROWS, D_MODEL), F32), pltpu.SemaphoreType.DMA((3,))],
        compiler_params=_params(("arbitrary", "arbitrary")),
        name="moe_dispatch",
    )(pos, fill, h3)


def _grouped_kernel(te_ref, nv_ref, x_ref, wg_ref, wu_ref, wd_ref, y_ref, xb_ref, *, t_chunk):
    f = pl.program_id(1)

    @pl.when(pl.program_id(0) < nv_ref[0])
    def _():
        @pl.when(f == 0)
        def _():
            xb_ref[...] = x_ref[...].astype(BF16)

        x = xb_ref[...]
        acc = None
        for c in range(wg_ref.shape[2] // t_chunk):
            cols = pl.ds(c * t_chunk, t_chunk)
            contrib = _swiglu_contrib(x, wg_ref.at[0, :, cols], wu_ref.at[0, :, cols], wd_ref.at[0, cols, :])
            acc = contrib if acc is None else acc + contrib

        @pl.when(f == 0)
        def _():
            y_ref[...] = acc

        @pl.when(f > 0)
        def _():
            y_ref[...] += acc

    @pl.when(pl.program_id(0) >= nv_ref[0])
    def _():
        y_ref[...] = jnp.zeros_like(y_ref)


def _grouped_ffn(tile_expert, n_valid, xs, wg, wu, wd, cfg):
    tg, tf = cfg.t_group, cfg.t_ff_expert
    n_f = cfg.d_ff_expert // tf
    rows = xs.shape[0]

    def live(i, nv):
        return jnp.minimum(i, nv[0] - 1)

    def f_idx(i, f, nv):
        return jnp.where(i < nv[0], f, n_f - 1)

    grid_spec = pltpu.PrefetchScalarGridSpec(
        num_scalar_prefetch=2,
        grid=(rows // tg, n_f),
        in_specs=[pl.BlockSpec((tg, D_MODEL), lambda i, f, te, nv: (live(i, nv), 0)),
                  pl.BlockSpec((1, D_MODEL, tf), lambda i, f, te, nv: (te[i], 0, f_idx(i, f, nv))),
                  pl.BlockSpec((1, D_MODEL, tf), lambda i, f, te, nv: (te[i], 0, f_idx(i, f, nv))),
                  pl.BlockSpec((1, tf, D_MODEL), lambda i, f, te, nv: (te[i], f_idx(i, f, nv), 0))],
        out_specs=pl.BlockSpec((tg, D_MODEL), lambda i, f, te, nv: (i, 0)),
        scratch_shapes=[pltpu.VMEM((tg, D_MODEL), BF16)],
    )
    return pl.pallas_call(
        functools.partial(_grouped_kernel, t_chunk=cfg.t_ff_dense),
        grid_spec=grid_spec,
        out_shape=jax.ShapeDtypeStruct((rows, D_MODEL), F32),
        compiler_params=_params(("arbitrary", "arbitrary")),
        name="moe_grouped_ffn",
    )(tile_expert, n_valid, xs, wg, wu, wd)


def _combine_kernel(pos_ref, next_pos_ref, info_ref, h_ref, g_ref, b_ref, ys_ref, out_ref, rows_ref, sem,
                    *, alpha, t_route):
    step = pl.program_id(0) * pl.num_programs(1) + pl.program_id(1)
    last = pl.num_programs(0) * pl.num_programs(1) - 1
    buf = step % 2

    def gather(tile_pos_ref, into):
        def issue(r, carry):
            for slot in range(TOP_K):
                pltpu.make_async_copy(ys_ref.at[pl.ds(tile_pos_ref[0, 0, slot * t_route + r], 1)],
                                      rows_ref.at[into, slot, pl.ds(r, 1)],
                                      sem.at[into, slot]).start(priority=slot)
            return carry

        lax.fori_loop(0, t_route, issue, 0, unroll=ISSUE_UNROLL)

    @pl.when(step == 0)
    def _():
        gather(pos_ref, 0)

    @pl.when(step < last)
    def _():
        gather(next_pos_ref, 1 - buf)

    for slot in range(TOP_K):
        pltpu.make_async_copy(ys_ref.at[pl.ds(0, t_route)], rows_ref.at[buf, slot], sem.at[buf, slot]).wait()

    info = info_ref[0]
    lane = lax.broadcasted_iota(jnp.int32, info.shape, 1)
    ffn = (_lane_pick(info, lane, INFO_W1) * rows_ref[buf, 0]
           + _lane_pick(info, lane, INFO_W2) * rows_ref[buf, 1])
    out_ref[0] = _layer_norm(alpha * h_ref[0] + ffn, g_ref[...], b_ref[...])


def _combine(pos, info, h3, g, b, ys, cfg, alpha):
    tr = cfg.t_route
    steps = cfg.seq // tr
    n_tiles = cfg.batch * steps
    const = lambda bb, j: (0, 0)

    def pos_spec(ahead):
        return pl.BlockSpec((1, 1, TOP_K * tr),
                            lambda bb, j: (jnp.minimum(bb * steps + j + ahead, n_tiles - 1), 0, 0),
                            memory_space=pltpu.SMEM)

    return pl.pallas_call(
        functools.partial(_combine_kernel, alpha=alpha, t_route=tr),
        grid=(cfg.batch, steps),
        in_specs=[pos_spec(0), pos_spec(1),
                  pl.BlockSpec((1, tr, LANE), lambda bb, j: (bb, j, 0)),
                  _seq_rows_spec(tr, D_MODEL),
                  pl.BlockSpec((1, D_MODEL), const),
                  pl.BlockSpec((1, D_MODEL), const),
                  pl.BlockSpec(memory_space=pl.ANY)],
        out_specs=pl.BlockSpec((1, tr, D_MODEL), lambda bb, j: (bb, j, 0)),
        out_shape=jax.ShapeDtypeStruct((cfg.batch, cfg.seq, D_MODEL), F32),
        scratch_shapes=[pltpu.VMEM((2, TOP_K, tr, D_MODEL), F32), pltpu.SemaphoreType.DMA((2, TOP_K))],
        compiler_params=_params(("arbitrary", "arbitrary")),
        name="moe_combine_ln",
    )(pos, pos, info, h3, g, b, ys)


def _routed_moe(h3, w_router, wg, wu, wd, g, b, cfg, alpha):
    n_e, tg, tr = cfg.n_experts, cfg.t_group, cfg.t_route
    n_tok = cfg.batch * cfg.seq
    rows = TOP_K * n_tok + n_e * tg
    wr_pad = jnp.pad(w_router, ((0, 0), (0, LANE - n_e)))
    info, route, seen = _router(h3, wr_pad, cfg)

    as_int = lambda row: route[:, row, :].astype(jnp.int32)
    counts = seen[0, :n_e].astype(jnp.int32)
    padded = (counts + tg - 1) // tg * tg
    ends = jnp.cumsum(padded)
    starts = ends - padded
    start_of = lambda e: jnp.sum(jnp.where(e[..., None] == jnp.arange(n_e), starts, 0), axis=-1)
    pos = jnp.stack([start_of(as_int(INFO_E1)) + as_int(INFO_R1),
                     start_of(as_int(INFO_E2)) + as_int(INFO_R2)], axis=1)
    pos = pos.reshape(n_tok // tr, 1, TOP_K * tr)
    fill = jnp.stack([jnp.append(starts + counts, ends[-1] // ZERO_ROWS),
                      jnp.append(ends, rows // ZERO_ROWS)])
    n_valid = (ends[-1] // tg).reshape(1)
    tile_start = jnp.arange(rows // tg, dtype=jnp.int32) * tg
    tile_expert = jnp.sum(tile_start[:, None] >= ends[None, :], axis=-1).astype(jnp.int32)
    tile_expert = tile_expert[jnp.minimum(jnp.arange(rows // tg), n_valid[0] - 1)]

    xs = _dispatch(pos, fill, h3, cfg, rows)
    ys = _grouped_ffn(tile_expert, n_valid, xs, wg, wu, wd, cfg)
    return _combine(pos, info, h3, g, b, ys, cfg, alpha)


def _rope_tables(lp):
    half = HEAD_DIM // 2
    inv = 1.0 / (ROPE_THETA ** (jnp.arange(0, HEAD_DIM, 2, dtype=F32) / HEAD_DIM))
    pos = jnp.arange(lp, dtype=F32)
    ang = pos[:, None] * inv[None, :]
    ang = jnp.tile(ang, (1, LANE // half))
    sign = jnp.where((jnp.arange(LANE) % HEAD_DIM) < half, -1.0, 1.0).astype(F32)
    return jnp.cos(ang), jnp.sin(ang) * sign[None, :]


def _forward(cfg, x, meta_tokens, ln_emb_g, ln_emb_b, w_in, conv_w, lambda_q1, lambda_k1,
             lambda_q2, lambda_k2, subln_g, w_out, ln_mix_g, ln_mix_b, ln_ffn_g, ln_ffn_b,
             w_gate_dense, w_up_dense, w_down_dense, w_router, w_gate_moe, w_up_moe, w_down_moe):
    alpha = (2 * cfg.depth) ** 0.25
    tokens = cfg.tokens
    vec = lambda a: a.reshape(1, -1)
    cos, sin = _rope_tables(cfg.lp)
    h = _embed(x, meta_tokens, vec(ln_emb_g), vec(ln_emb_b), cfg).reshape(tokens, D_MODEL)
    moe_bf16 = {}
    for layer in range(cfg.depth):
        lambda_init = 0.8 - 0.6 * math.exp(-0.3 * layer)
        dense = layer % 2 == 0
        idx = layer // 2
        feeds_moe = layer + 1 < cfg.depth if dense else True
        side = (w_gate_moe if dense else w_up_moe)[idx] if feeds_moe else None
        conv, q, k, v, *cast = _inproj(h.reshape(cfg.batch, cfg.lp, D_MODEL), w_in[layer].astype(BF16),
                                       conv_w[layer], cos, sin, cfg, cast=side)
        if feeds_moe:
            moe_bf16["gate" if dense else "up"] = cast[0]
        attn = _attention(q, k, v, vec(lambda_q1[layer]), vec(lambda_k1[layer]),
                          vec(lambda_q2[layer]), vec(lambda_k2[layer]), vec(subln_g[layer]),
                          lambda_init, cfg)
        h = _outproj(conv.reshape(tokens, D_CONV), attn.reshape(tokens, D_MODEL - D_CONV),
                     w_out[layer].astype(BF16), h, vec(ln_mix_g[layer]), vec(ln_mix_b[layer]), cfg, alpha)
        g, b = vec(ln_ffn_g[layer]), vec(ln_ffn_b[layer])
        if dense:
            h, *cast = _ffn(w_gate_dense[idx].astype(BF16), w_up_dense[idx].astype(BF16),
                            w_down_dense[idx].astype(BF16), h, g, b, cfg, alpha,
                            cast=w_down_moe[idx] if feeds_moe else None)
            if feeds_moe:
                moe_bf16["down"] = cast[0]
        else:
            assert layer == cfg.depth - 1
            return _routed_moe(h.reshape(cfg.batch, cfg.lp, D_MODEL), w_router[idx],
                               moe_bf16["gate"], moe_bf16["up"], moe_bf16["down"], g, b, cfg, alpha)
    return h.reshape(cfg.batch, cfg.lp, D_MODEL)[:, N_META:N_META + cfg.seq]


_CFG = Cfg(batch=8, seq=4096, depth=2, d_ff_dense=2816, d_ff_expert=3584, n_experts=8,
           t_embed=128, n_embed=11, t_proj=1056, t_q=384, t_k=384, t_tok=1024, t_ff_dense=256, t_ff_expert=3584,
           t_route=1024, t_group=512)


def kernel(x, meta_tokens, ln_emb_g, ln_emb_b, w_in, conv_w, lambda_q1, lambda_k1, lambda_q2, lambda_k2, subln_g, w_out, ln_mix_g, ln_mix_b, ln_ffn_g, ln_ffn_b, w_gate_dense, w_up_dense, w_down_dense, w_router, w_gate_moe, w_up_moe, w_down_moe):
    return _forward(_CFG, x, meta_tokens, ln_emb_g, ln_emb_b, w_in, conv_w, lambda_q1, lambda_k1,
                    lambda_q2, lambda_k2, subln_g, w_out, ln_mix_g, ln_mix_b, ln_ffn_g, ln_ffn_b,
                    w_gate_dense, w_up_dense, w_down_dense, w_router, w_gate_moe, w_up_moe, w_down_moe)
```

```python
import dataclasses
import functools
import math

import jax
import jax.numpy as jnp
from jax import lax
from jax.experimental import pallas as pl
from jax.experimental.pallas import tpu as pltpu

F32 = jnp.float32
BF16 = jnp.bfloat16

LANE = 128
D_MODEL = 1024
N_META = 16
PAD = LANE - N_META
D_CONV = D_MODEL // 2
CONV_WIDTH = 3
HEAD_DIM = 64
N_HEADS = 4
V_DIM = 2 * HEAD_DIM
QK_W = N_HEADS * 2 * HEAD_DIM
GROUP_W = 512
ROPE_THETA = 10000.0
TOP_K = 2
LN_EPS = 1e-5
RMS_EPS = 1e-5
NEG_BIG = -1e30
VMEM_LIMIT = 56 * 1024 * 1024


@dataclasses.dataclass(frozen=True)
class Cfg:
    batch: int
    seq: int
    depth: int
    d_ff_dense: int
    d_ff_expert: int
    n_experts: int
    t_embed: int
    n_embed: int
    t_proj: int
    t_q: int
    t_k: int
    t_tok: int
    t_ff_dense: int
    t_ff_expert: int
    t_route: int
    t_group: int

    @property
    def lp(self):
        return N_META + self.seq + PAD

    @property
    def tokens(self):
        return self.batch * self.lp


def _params(semantics):
    return pltpu.CompilerParams(dimension_semantics=semantics, vmem_limit_bytes=VMEM_LIMIT)


def _layer_norm(y, g, b):
    mu = jnp.mean(y, axis=-1, keepdims=True)
    yc = y - mu
    var = jnp.mean(yc * yc, axis=-1, keepdims=True)
    return yc * lax.rsqrt(var + LN_EPS) * g + b


def _cast_job(w, n_steps):
    flat = w.reshape(-1, w.shape[-1])
    n_blocks = 1 << (n_steps.bit_length() - 1)
    rows = flat.shape[0] // n_blocks
    assert rows * n_blocks == flat.shape[0] and rows % 16 == 0, (w.shape, n_steps)
    return flat, (rows, flat.shape[1]), n_blocks


def _embed_kernel(*refs, seq, n_blocks):
    x_refs, (meta_ref, g_ref, b_ref, h_ref) = refs[:n_blocks + 1], refs[n_blocks + 1:]
    j = pl.program_id(1)
    te = x_refs[1].shape[1]
    head = jnp.where(j == 0, meta_ref[...], x_refs[0][0, te - N_META:])
    rows = jnp.concatenate([head] + [r[0] for r in x_refs[1:n_blocks]] + [x_refs[n_blocks][0, :te - N_META]],
                           axis=0)
    rid = j * (n_blocks * te) + lax.broadcasted_iota(jnp.int32, (n_blocks * te, 1), 0)
    rows = jnp.where(rid < N_META + seq, rows, 0.0)
    h_ref[0] = _layer_norm(rows, g_ref[...], b_ref[...])


def _embed(x, meta_tokens, g, b, cfg):
    te, n = cfg.t_embed, cfg.n_embed
    last = cfg.seq // te - 1

    def x_spec(i):
        return pl.BlockSpec((1, te, D_MODEL), lambda bb, j: (bb, jnp.clip(n * j + i - 1, 0, last), 0))

    const = lambda bb, j: (0, 0)
    return pl.pallas_call(
        functools.partial(_embed_kernel, seq=cfg.seq, n_blocks=n),
        grid=(cfg.batch, cfg.lp // (n * te)),
        in_specs=[x_spec(i) for i in range(n + 1)]
        + [pl.BlockSpec((N_META, D_MODEL), const),
           pl.BlockSpec((1, D_MODEL), const),
           pl.BlockSpec((1, D_MODEL), const)],
        out_specs=pl.BlockSpec((1, n * te, D_MODEL), lambda bb, j: (bb, j, 0)),
        out_shape=jax.ShapeDtypeStruct((cfg.batch, cfg.lp, D_MODEL), F32),
        compiler_params=_params(("parallel", "arbitrary")),
        name="embed_ln",
    )(*([x] * (n + 1)), meta_tokens, g, b)


def _inproj_kernel(*refs, t_proj, side_cast):
    if side_cast:
        (h_ref, w_ref, cw_ref, cos_ref, sin_ref, cast_src_ref,
         conv_ref, q_ref, k_ref, v_ref, cast_dst_ref, carry_ref) = refs
        cast_dst_ref[...] = cast_src_ref[...].astype(BF16)
    else:
        h_ref, w_ref, cw_ref, cos_ref, sin_ref, conv_ref, q_ref, k_ref, v_ref, carry_ref = refs
    j = pl.program_id(1)
    x = h_ref[0].astype(BF16)

    def proj(g):
        return jnp.dot(x, w_ref[:, g * GROUP_W:(g + 1) * GROUP_W], preferred_element_type=F32)

    rid = lax.broadcasted_iota(jnp.int32, (t_proj, 1), 0)
    gated = proj(1) * proj(2)

    @pl.when(j == 0)
    def _():
        carry_ref[...] = jnp.zeros_like(carry_ref)

    prev = carry_ref[...]
    back1 = jnp.where(rid == 0, prev[7:8], pltpu.roll(gated, 1, 0))
    back2 = jnp.where(rid == 0, prev[6:7], jnp.where(rid == 1, prev[7:8], pltpu.roll(gated, 2, 0)))
    carry_ref[...] = gated[t_proj - 8:]
    cw = cw_ref[...]
    conv = cw[0:1] * back2 + cw[1:2] * back1 + cw[2:3] * gated
    conv_ref[0] = (proj(0) * conv).astype(BF16)

    cos = cos_ref[...]
    sin = sin_ref[...]
    lane = lax.broadcasted_iota(jnp.int32, (1, LANE), 1)
    low_half = (lane % HEAD_DIM) < HEAD_DIM // 2

    def rope(z):
        partner = jnp.where(low_half, pltpu.roll(z, LANE - HEAD_DIM // 2, 1),
                            pltpu.roll(z, HEAD_DIM // 2, 1))
        return z * cos + partner * sin

    zq = proj(3)
    zk = proj(4)
    scale = HEAD_DIM ** -0.5 * math.log2(math.e)
    for h in range(N_HEADS):
        sl = slice(h * LANE, (h + 1) * LANE)
        q_ref[0, :, sl] = (rope(zq[:, sl]) * scale).astype(BF16)
        k_ref[0, :, sl] = rope(zk[:, sl]).astype(BF16)
    v_ref[0] = proj(5).astype(BF16)


def _inproj(h, w_in, conv_w, cos, sin, cfg, cast=None):
    tp = cfg.t_proj
    steps = cfg.lp // tp
    row = lambda bb, j: (bb, j, 0)
    const = lambda bb, j: (0, 0)
    out_spec = pl.BlockSpec((1, tp, GROUP_W), row)
    out_shape = jax.ShapeDtypeStruct((cfg.batch, cfg.lp, GROUP_W), BF16)
    in_specs = [pl.BlockSpec((1, tp, D_MODEL), row),
                pl.BlockSpec((D_MODEL, 6 * GROUP_W), const),
                pl.BlockSpec((CONV_WIDTH, D_CONV), const),
                pl.BlockSpec((tp, LANE), lambda bb, j: (j, 0)),
                pl.BlockSpec((tp, LANE), lambda bb, j: (j, 0))]
    args, out_specs, out_shapes = [h, w_in, conv_w, cos, sin], [out_spec] * 4, [out_shape] * 4
    if cast is not None:
        flat, block, n_blocks = _cast_job(cast, cfg.batch * steps)
        cast_spec = pl.BlockSpec(block, lambda bb, j: (jnp.minimum(bb * steps + j, n_blocks - 1), 0))
        in_specs.append(cast_spec)
        args.append(flat)
        out_specs.append(cast_spec)
        out_shapes.append(jax.ShapeDtypeStruct(flat.shape, BF16))
    outs = pl.pallas_call(
        functools.partial(_inproj_kernel, t_proj=tp, side_cast=cast is not None),
        grid=(cfg.batch, steps),
        in_specs=in_specs,
        out_specs=out_specs,
        out_shape=out_shapes,
        scratch_shapes=[pltpu.VMEM((8, D_CONV), F32)],
        compiler_params=_params(("arbitrary", "arbitrary")),
        name="inproj_conv_rope",
    )(*args)
    outs = list(outs)
    if cast is not None:
        outs[4] = outs[4].reshape(cast.shape)
    return outs


def _attn_kernel(lq1_ref, lk1_ref, lq2_ref, lk2_ref, g_ref, q_ref, k_ref, v_ref,
                 o_ref, m_ref, l_ref, acc_ref, *, t_q, t_k, lambda_init):
    j = pl.program_id(1)
    head = lambda h: slice(h * LANE, (h + 1) * LANE)
    first_map = lax.broadcasted_iota(jnp.int32, (1, LANE), 1) < HEAD_DIM
    zero = jnp.zeros((), BF16)
    q2 = []
    for h in range(N_HEADS):
        q = q_ref[0, :, head(h)]
        q2.append(jnp.concatenate([jnp.where(first_map, q, zero), jnp.where(first_map, zero, q)], axis=0))
    m_ref[...] = jnp.full_like(m_ref, NEG_BIG)
    l_ref[...] = jnp.zeros_like(l_ref)
    acc_ref[...] = jnp.zeros_like(acc_ref)

    def step(c, width, causal):
        start = pl.multiple_of(c * t_k, t_k)
        for h in range(N_HEADS):
            k = k_ref[0, pl.ds(start, width), head(h)]
            v = v_ref[0, pl.ds(start, width), head(h)]
            s = lax.dot_general(q2[h], k, (((1,), (1,)), ((), ())), preferred_element_type=F32)
            if causal:
                qpos = lax.broadcasted_iota(jnp.int32, (2 * t_q, 1), 0) % t_q
                kpos = lax.broadcasted_iota(jnp.int32, (1, width), 1) - (width - t_k)
                s = jnp.where(kpos <= qpos, s, NEG_BIG)
            m_prev = m_ref[h]
            m_new = jnp.maximum(m_prev, jnp.max(s, axis=-1, keepdims=True))
            alpha = jnp.exp2(m_prev - m_new)
            p = jnp.exp2((s - jnp.concatenate([m_new] * (width // LANE), axis=1)).astype(BF16))
            l_ref[h] = alpha * l_ref[h] + jnp.sum(p.astype(F32), axis=-1, keepdims=True)
            acc_ref[h] = alpha * acc_ref[h] + jnp.dot(p, v, preferred_element_type=F32)
            m_ref[h] = m_new

    def two_pairs(i, carry):
        step(4 * i, 2 * t_k, causal=False)
        step(4 * i + 2, 2 * t_k, causal=False)
        return carry

    lax.fori_loop(0, j // 4, two_pairs, 0)

    @pl.when((j // 2) % 2 == 1)
    def _():
        step((j // 4) * 4, 2 * t_k, causal=False)

    @pl.when(j % 2 == 1)
    def _():
        step(j - 1, 2 * t_k, causal=True)

    @pl.when(j % 2 == 0)
    def _():
        step(j, t_k, causal=True)

    lam = (jnp.exp(jnp.sum(lq1_ref[...] * lk1_ref[...], axis=-1, keepdims=True))
           - jnp.exp(jnp.sum(lq2_ref[...] * lk2_ref[...], axis=-1, keepdims=True)) + lambda_init)
    for h in range(N_HEADS):
        o = acc_ref[h] / l_ref[h]
        d = o[:t_q] - lam * o[t_q:]
        d = d * lax.rsqrt(jnp.mean(d * d, axis=-1, keepdims=True) + RMS_EPS) * g_ref[...]
        o_ref[0, :, head(h)] = (d * (1.0 - lambda_init)).astype(BF16)


def _attention(q, k, v, lq1, lk1, lq2, lk2, subln_g, lambda_init, cfg):
    tq = cfg.t_q
    width = N_HEADS * V_DIM
    small = lambda bb, j: (0, 0)
    q_spec = pl.BlockSpec((1, tq, width), lambda bb, j: (bb, j, 0))
    kv_spec = pl.BlockSpec((1, cfg.lp, width), lambda bb, j: (bb, 0, 0))
    return pl.pallas_call(
        functools.partial(_attn_kernel, t_q=tq, t_k=cfg.t_k, lambda_init=lambda_init),
        grid=(cfg.batch, cfg.lp // tq),
        in_specs=[pl.BlockSpec((1, HEAD_DIM), small)] * 4
        + [pl.BlockSpec((1, V_DIM), small), q_spec, kv_spec, kv_spec],
        out_specs=q_spec,
        out_shape=jax.ShapeDtypeStruct((cfg.batch, cfg.lp, width), BF16),
        scratch_shapes=[pltpu.VMEM((N_HEADS, 2 * tq, LANE), F32),
                        pltpu.VMEM((N_HEADS, 2 * tq, LANE), F32),
                        pltpu.VMEM((N_HEADS, 2 * tq, V_DIM), F32)],
        compiler_params=_params(("parallel", "arbitrary")),
        name="diff_attention",
    )(lq1, lk1, lq2, lk2, subln_g, q, k, v)


def _outproj_kernel(conv_ref, attn_ref, w_ref, h_ref, g_ref, b_ref, ho_ref, *, alpha):
    mix = (jnp.dot(conv_ref[...], w_ref[:D_CONV], preferred_element_type=F32)
           + jnp.dot(attn_ref[...], w_ref[D_CONV:], preferred_element_type=F32))
    ho_ref[...] = _layer_norm(alpha * h_ref[...] + mix, g_ref[...], b_ref[...])


def _outproj(conv, attn, w_out, h, g, b, cfg, alpha):
    tt = cfg.t_tok
    row = lambda i: (i, 0)
    const = lambda i: (0, 0)
    return pl.pallas_call(
        functools.partial(_outproj_kernel, alpha=alpha),
        grid=(cfg.tokens // tt,),
        in_specs=[pl.BlockSpec((tt, D_CONV), row),
                  pl.BlockSpec((tt, D_MODEL - D_CONV), row),
                  pl.BlockSpec((D_MODEL, D_MODEL), const),
                  pl.BlockSpec((tt, D_MODEL), row),
                  pl.BlockSpec((1, D_MODEL), const),
                  pl.BlockSpec((1, D_MODEL), const)],
        out_specs=pl.BlockSpec((tt, D_MODEL), row),
        out_shape=jax.ShapeDtypeStruct((cfg.tokens, D_MODEL), F32),
        compiler_params=_params(("parallel",)),
        name="outproj_ln",
    )(conv, attn, w_out, h, g, b)


def _swiglu_contrib(x, wg_ref, wu_ref, wd_ref):
    gate = jnp.dot(x, wg_ref[...], preferred_element_type=F32)
    up = jnp.dot(x, wu_ref[...], preferred_element_type=F32)
    act = gate * (1.0 / (1.0 + jnp.exp(-gate))) * up
    return jnp.dot(act.astype(BF16), wd_ref[...], preferred_element_type=F32)


def _ffn_kernel(wg_ref, wu_ref, wd_ref, h_ref, g_ref, b_ref, *rest, alpha, t_ff):
    if len(rest) == 3:
        cast_src_ref, ho_ref, cast_dst_ref = rest
        cast_dst_ref[...] = cast_src_ref[...].astype(BF16)
    else:
        ho_ref, = rest
    h = h_ref[...]
    x = h.astype(BF16)
    acc = None
    for c in range(wg_ref.shape[1] // t_ff):
        cols = pl.ds(c * t_ff, t_ff)
        contrib = _swiglu_contrib(x, wg_ref.at[:, cols], wu_ref.at[:, cols], wd_ref.at[cols, :])
        acc = contrib if acc is None else acc + contrib
    ho_ref[...] = _layer_norm(alpha * h + acc, g_ref[...], b_ref[...])


def _ffn(wg, wu, wd, h, g, b, cfg, alpha, cast=None):
    tt = cfg.t_tok
    steps = cfg.tokens // tt
    row = lambda i: (i, 0)
    const = lambda i: (0, 0)
    resident = lambda shape: pl.BlockSpec(shape, const, pipeline_mode=pl.Buffered(1))
    in_specs = [resident((D_MODEL, cfg.d_ff_dense)),
                resident((D_MODEL, cfg.d_ff_dense)),
                resident((cfg.d_ff_dense, D_MODEL)),
                pl.BlockSpec((tt, D_MODEL), row),
                pl.BlockSpec((1, D_MODEL), const),
                pl.BlockSpec((1, D_MODEL), const)]
    args = [wg, wu, wd, h, g, b]
    out_specs = [pl.BlockSpec((tt, D_MODEL), row)]
    out_shapes = [jax.ShapeDtypeStruct((cfg.tokens, D_MODEL), F32)]
    if cast is not None:
        flat, block, n_blocks = _cast_job(cast, steps)
        cast_spec = pl.BlockSpec(block, lambda i: (jnp.minimum(i, n_blocks - 1), 0))
        in_specs.append(cast_spec)
        args.append(flat)
        out_specs.append(cast_spec)
        out_shapes.append(jax.ShapeDtypeStruct(flat.shape, BF16))
    outs = pl.pallas_call(
        functools.partial(_ffn_kernel, alpha=alpha, t_ff=cfg.t_ff_dense),
        grid=(steps,),
        in_specs=in_specs,
        out_specs=out_specs,
        out_shape=out_shapes,
        compiler_params=_params(("arbitrary",)),
        name="dense_ffn_ln",
    )(*args)
    outs = list(outs)
    if cast is not None:
        outs[1] = outs[1].reshape(cast.shape)
    return outs


INFO_E1, INFO_E2, INFO_W1, INFO_W2, INFO_R1, INFO_R2 = range(6)
ROUTE_ROWS = 8


def _lane_pick(values, lane, index):
    return jnp.sum(jnp.where(lane == index, values, 0.0), axis=-1, keepdims=True)


def _router_kernel(h_ref, wr_ref, info_ref, route_ref, cnt_ref, seen_ref, *, n_experts):
    @pl.when(jnp.logical_and(pl.program_id(0) == 0, pl.program_id(1) == 0))
    def _():
        seen_ref[...] = jnp.zeros_like(seen_ref)

    h = h_ref[0]
    w = wr_ref[...]
    h_hi = h.astype(BF16)
    h_lo = (h - h_hi.astype(F32)).astype(BF16)
    w_hi = w.astype(BF16)
    w_lo = (w - w_hi.astype(F32)).astype(BF16)
    logits = (jnp.dot(h_hi, w_hi, preferred_element_type=F32)
              + jnp.dot(h_lo, w_hi, preferred_element_type=F32)
              + jnp.dot(h_hi, w_lo, preferred_element_type=F32))
    t_route = logits.shape[0]
    lane = lax.broadcasted_iota(jnp.int32, logits.shape, 1).astype(F32)
    logits = jnp.where(lane < n_experts, logits, -jnp.inf)
    v1 = jnp.max(logits, axis=-1, keepdims=True)
    e1 = jnp.min(jnp.where(logits == v1, lane, float(LANE)), axis=-1, keepdims=True)
    rest = jnp.where(lane == e1, -jnp.inf, logits)
    v2 = jnp.max(rest, axis=-1, keepdims=True)
    e2 = jnp.min(jnp.where(rest == v2, lane, float(LANE)), axis=-1, keepdims=True)
    ex = jnp.exp(v2 - v1)
    w1 = 1.0 / (1.0 + ex)
    w2 = ex / (1.0 + ex)

    chosen = jnp.where(lane == e1, 1.0, 0.0) + jnp.where(lane == e2, 1.0, 0.0)
    r = lax.broadcasted_iota(jnp.int32, (LANE, LANE), 0)
    c = lax.broadcasted_iota(jnp.int32, (LANE, LANE), 1)
    earlier = jnp.where(c < r, 1.0, 0.0).astype(BF16)
    prefix = seen_ref[...]
    blocks = []
    for blk in range(t_route // LANE):
        rows = chosen[blk * LANE:(blk + 1) * LANE]
        blocks.append(jnp.dot(earlier, rows.astype(BF16), preferred_element_type=F32) + prefix)
        prefix = prefix + jnp.sum(rows, axis=0, keepdims=True)
    before = jnp.concatenate(blocks, axis=0)
    r1 = _lane_pick(before, lane, e1)
    r2 = _lane_pick(before, lane, e2)
    seen_ref[...] = prefix
    cnt_ref[...] = prefix

    record = jnp.zeros_like(logits)
    for slot, val in ((INFO_E1, e1), (INFO_E2, e2), (INFO_W1, w1), (INFO_W2, w2),
                      (INFO_R1, r1), (INFO_R2, r2)):
        record = jnp.where(lane == slot, val, record)
    info_ref[0] = record
    route_ref[0] = record.T[:ROUTE_ROWS]


def _seq_rows_spec(t_rows, width):
    return pl.BlockSpec((pl.Element(1), pl.Element(t_rows), pl.Element(width)),
                        lambda bb, j: (bb, pl.multiple_of(N_META + j * t_rows, N_META), 0))


def _router(h3, wr_pad, cfg):
    tr = cfg.t_route
    steps = cfg.seq // tr
    return pl.pallas_call(
        functools.partial(_router_kernel, n_experts=cfg.n_experts),
        grid=(cfg.batch, cfg.seq // tr),
        in_specs=[_seq_rows_spec(tr, D_MODEL),
                  pl.BlockSpec((D_MODEL, LANE), lambda bb, j: (0, 0))],
        out_specs=[pl.BlockSpec((1, tr, LANE), lambda bb, j: (bb, j, 0)),
                   pl.BlockSpec((1, ROUTE_ROWS, tr), lambda bb, j: (bb * steps + j, 0, 0)),
                   pl.BlockSpec((1, LANE), lambda bb, j: (0, 0))],
        out_shape=[jax.ShapeDtypeStruct((cfg.batch, cfg.seq, LANE), F32),
                   jax.ShapeDtypeStruct((cfg.batch * steps, ROUTE_ROWS, tr), F32),
                   jax.ShapeDtypeStruct((1, LANE), F32)],
        scratch_shapes=[pltpu.VMEM((1, LANE), F32)],
        compiler_params=_params(("arbitrary", "arbitrary")),
        name="router_top2",
    )(h3, wr_pad)


ZERO_ROWS = 8
ISSUE_UNROLL = 8


def _dispatch_kernel(pos_ref, fill_ref, h_ref, xs_ref, zero_ref, sem, *, t_route, n_experts):
    def zero_copy(p, n):
        start = pl.multiple_of(p * n, n)
        return pltpu.make_async_copy(zero_ref.at[pl.ds(0, n)], xs_ref.at[pl.ds(start, n)], sem.at[2])

    def zero_rows(lo, hi, n):
        def start(p, carry):
            zero_copy(p, n).start()
            return carry

        def wait(p, carry):
            zero_copy(p, n).wait()
            return carry

        lax.fori_loop(lo, hi, start, 0)
        lax.fori_loop(lo, hi, wait, 0)

    @pl.when(jnp.logical_and(pl.program_id(0) == 0, pl.program_id(1) == 0))
    def _():
        zero_ref[...] = jnp.zeros_like(zero_ref)
        for e in range(n_experts):
            zero_rows(fill_ref[0, e], fill_ref[1, e], 1)
        zero_rows(fill_ref[0, n_experts], fill_ref[1, n_experts], ZERO_ROWS)

    def issue(r, carry):
        for slot in range(TOP_K):
            pltpu.make_async_copy(h_ref.at[0, pl.ds(r, 1)],
                                  xs_ref.at[pl.ds(pos_ref[0, 0, slot * t_route + r], 1)],
                                  sem.at[slot]).start(priority=slot)
        return carry

    lax.fori_loop(0, t_route, issue, 0, unroll=ISSUE_UNROLL)
    for slot in range(TOP_K):
        pltpu.make_async_copy(h_ref.at[0], xs_ref.at[pl.ds(0, t_route)], sem.at[slot]).wait()


def _dispatch(pos, fill, h3, cfg, rows):
    tr = cfg.t_route
    steps = cfg.seq // tr
    return pl.pallas_call(
        functools.partial(_dispatch_kernel, t_route=tr, n_experts=cfg.n_experts),
        grid=(cfg.batch, steps),
        in_specs=[pl.BlockSpec((1, 1, TOP_K * tr), lambda bb, j: (bb * steps + j, 0, 0),
                               memory_space=pltpu.SMEM),
                  pl.BlockSpec(memory_space=pltpu.SMEM),
                  _seq_rows_spec(tr, D_MODEL)],
        out_specs=pl.BlockSpec(memory_space=pl.ANY),
        out_shape=jax.ShapeDtypeStruct((rows, D_MODEL), F32),
        scratch_shapes=[pltpu.VMEM((ZERO_ROWS, D_MODEL), F32), pltpu.SemaphoreType.DMA((3,))],
        compiler_params=_params(("arbitrary", "arbitrary")),
        name="moe_dispatch",
    )(pos, fill, h3)


def _grouped_kernel(te_ref, nv_ref, x_ref, wg_ref, wu_ref, wd_ref, y_ref, xb_ref, *, t_chunk):
    f = pl.program_id(1)

    @pl.when(pl.program_id(0) < nv_ref[0])
    def _():
        @pl.when(f == 0)
        def _():
            xb_ref[...] = x_ref[...].astype(BF16)

        x = xb_ref[...]
        acc = None
        for c in range(wg_ref.shape[2] // t_chunk):
            cols = pl.ds(c * t_chunk, t_chunk)
            contrib = _swiglu_contrib(x, wg_ref.at[0, :, cols], wu_ref.at[0, :, cols], wd_ref.at[0, cols, :])
            acc = contrib if acc is None else acc + contrib

        @pl.when(f == 0)
        def _():
            y_ref[...] = acc

        @pl.when(f > 0)
        def _():
            y_ref[...] += acc

    @pl.when(pl.program_id(0) >= nv_ref[0])
    def _():
        y_ref[...] = jnp.zeros_like(y_ref)


def _grouped_ffn(tile_expert, n_valid, xs, wg, wu, wd, cfg):
    tg, tf = cfg.t_group, cfg.t_ff_expert
    n_f = cfg.d_ff_expert // tf
    rows = xs.shape[0]

    def live(i, nv):
        return jnp.minimum(i, nv[0] - 1)

    def f_idx(i, f, nv):
        return jnp.where(i < nv[0], f, n_f - 1)

    grid_spec = pltpu.PrefetchScalarGridSpec(
        num_scalar_prefetch=2,
        grid=(rows // tg, n_f),
        in_specs=[pl.BlockSpec((tg, D_MODEL), lambda i, f, te, nv: (live(i, nv), 0)),
                  pl.BlockSpec((1, D_MODEL, tf), lambda i, f, te, nv: (te[i], 0, f_idx(i, f, nv))),
                  pl.BlockSpec((1, D_MODEL, tf), lambda i, f, te, nv: (te[i], 0, f_idx(i, f, nv))),
                  pl.BlockSpec((1, tf, D_MODEL), lambda i, f, te, nv: (te[i], f_idx(i, f, nv), 0))],
        out_specs=pl.BlockSpec((tg, D_MODEL), lambda i, f, te, nv: (i, 0)),
        scratch_shapes=[pltpu.VMEM((tg, D_MODEL), BF16)],
    )
    return pl.pallas_call(
        functools.partial(_grouped_kernel, t_chunk=cfg.t_ff_dense),
        grid_spec=grid_spec,
        out_shape=jax.ShapeDtypeStruct((rows, D_MODEL), F32),
        compiler_params=_params(("arbitrary", "arbitrary")),
        name="moe_grouped_ffn",
    )(tile_expert, n_valid, xs, wg, wu, wd)


def _combine_kernel(pos_ref, next_pos_ref, info_ref, h_ref, g_ref, b_ref, ys_ref, out_ref, rows_ref, sem,
                    *, alpha, t_route):
    step = pl.program_id(0) * pl.num_programs(1) + pl.program_id(1)
    last = pl.num_programs(0) * pl.num_programs(1) - 1
    buf = step % 2

    def gather(tile_pos_ref, into):
        def issue(r, carry):
            for slot in range(TOP_K):
                pltpu.make_async_copy(ys_ref.at[pl.ds(tile_pos_ref[0, 0, slot * t_route + r], 1)],
                                      rows_ref.at[into, slot, pl.ds(r, 1)],
                                      sem.at[into, slot]).start(priority=slot)
            return carry

        lax.fori_loop(0, t_route, issue, 0, unroll=ISSUE_UNROLL)

    @pl.when(step == 0)
    def _():
        gather(pos_ref, 0)

    @pl.when(step < last)
    def _():
        gather(next_pos_ref, 1 - buf)

    for slot in range(TOP_K):
        pltpu.make_async_copy(ys_ref.at[pl.ds(0, t_route)], rows_ref.at[buf, slot], sem.at[buf, slot]).wait()

    info = info_ref[0]
    lane = lax.broadcasted_iota(jnp.int32, info.shape, 1)
    ffn = (_lane_pick(info, lane, INFO_W1) * rows_ref[buf, 0]
           + _lane_pick(info, lane, INFO_W2) * rows_ref[buf, 1])
    out_ref[0] = _layer_norm(alpha * h_ref[0] + ffn, g_ref[...], b_ref[...])


def _combine(pos, info, h3, g, b, ys, cfg, alpha):
    tr = cfg.t_route
    steps = cfg.seq // tr
    n_tiles = cfg.batch * steps
    const = lambda bb, j: (0, 0)

    def pos_spec(ahead):
        return pl.BlockSpec((1, 1, TOP_K * tr),
                            lambda bb, j: (jnp.minimum(bb * steps + j + ahead, n_tiles - 1), 0, 0),
                            memory_space=pltpu.SMEM)

    return pl.pallas_call(
        functools.partial(_combine_kernel, alpha=alpha, t_route=tr),
        grid=(cfg.batch, steps),
        in_specs=[pos_spec(0), pos_spec(1),
                  pl.BlockSpec((1, tr, LANE), lambda bb, j: (bb, j, 0)),
                  _seq_rows_spec(tr, D_MODEL),
                  pl.BlockSpec((1, D_MODEL), const),
                  pl.BlockSpec((1, D_MODEL), const),
                  pl.BlockSpec(memory_space=pl.ANY)],
        out_specs=pl.BlockSpec((1, tr, D_MODEL), lambda bb, j: (bb, j, 0)),
        out_shape=jax.ShapeDtypeStruct((cfg.batch, cfg.seq, D_MODEL), F32),
        scratch_shapes=[pltpu.VMEM((2, TOP_K, tr, D_MODEL), F32), pltpu.SemaphoreType.DMA((2, TOP_K))],
        compiler_params=_params(("arbitrary", "arbitrary")),
        name="moe_combine_ln",
    )(pos, pos, info, h3, g, b, ys)


def _routed_moe(h3, w_router, wg, wu, wd, g, b, cfg, alpha):
    n_e, tg, tr = cfg.n_experts, cfg.t_group, cfg.t_route
    n_tok = cfg.batch * cfg.seq
    rows = TOP_K * n_tok + n_e * tg
    wr_pad = jnp.pad(w_router, ((0, 0), (0, LANE - n_e)))
    info, route, seen = _router(h3, wr_pad, cfg)

    as_int = lambda row: route[:, row, :].astype(jnp.int32)
    counts = seen[0, :n_e].astype(jnp.int32)
    padded = (counts + tg - 1) // tg * tg
    ends = jnp.cumsum(padded)
    starts = ends - padded
    start_of = lambda e: jnp.sum(jnp.where(e[..., None] == jnp.arange(n_e), starts, 0), axis=-1)
    pos = jnp.stack([start_of(as_int(INFO_E1)) + as_int(INFO_R1),
                     start_of(as_int(INFO_E2)) + as_int(INFO_R2)], axis=1)
    pos = pos.reshape(n_tok // tr, 1, TOP_K * tr)
    fill = jnp.stack([jnp.append(starts + counts, ends[-1] // ZERO_ROWS),
                      jnp.append(ends, rows // ZERO_ROWS)])
    n_valid = (ends[-1] // tg).reshape(1)
    tile_start = jnp.arange(rows // tg, dtype=jnp.int32) * tg
    tile_expert = jnp.sum(tile_start[:, None] >= ends[None, :], axis=-1).astype(jnp.int32)
    tile_expert = tile_expert[jnp.minimum(jnp.arange(rows // tg), n_valid[0] - 1)]

    xs = _dispatch(pos, fill, h3, cfg, rows)
    ys = _grouped_ffn(tile_expert, n_valid, xs, wg, wu, wd, cfg)
    return _combine(pos, info, h3, g, b, ys, cfg, alpha)


def _rope_tables(lp):
    half = HEAD_DIM // 2
    inv = 1.0 / (ROPE_THETA ** (jnp.arange(0, HEAD_DIM, 2, dtype=F32) / HEAD_DIM))
    pos = jnp.arange(lp, dtype=F32)
    ang = pos[:, None] * inv[None, :]
    ang = jnp.tile(ang, (1, LANE // half))
    sign = jnp.where((jnp.arange(LANE) % HEAD_DIM) < half, -1.0, 1.0).astype(F32)
    return jnp.cos(ang), jnp.sin(ang) * sign[None, :]


def _forward(cfg, x, meta_tokens, ln_emb_g, ln_emb_b, w_in, conv_w, lambda_q1, lambda_k1,
             lambda_q2, lambda_k2, subln_g, w_out, ln_mix_g, ln_mix_b, ln_ffn_g, ln_ffn_b,
             w_gate_dense, w_up_dense, w_down_dense, w_router, w_gate_moe, w_up_moe, w_down_moe):
    alpha = (2 * cfg.depth) ** 0.25
    tokens = cfg.tokens
    vec = lambda a: a.reshape(1, -1)
    cos, sin = _rope_tables(cfg.lp)
    h = _embed(x, meta_tokens, vec(ln_emb_g), vec(ln_emb_b), cfg).reshape(tokens, D_MODEL)
    moe_bf16 = {}
    for layer in range(cfg.depth):
        lambda_init = 0.8 - 0.6 * math.exp(-0.3 * layer)
        dense = layer % 2 == 0
        idx = layer // 2
        feeds_moe = layer + 1 < cfg.depth if dense else True
        side = (w_gate_moe if dense else w_up_moe)[idx] if feeds_moe else None
        conv, q, k, v, *cast = _inproj(h.reshape(cfg.batch, cfg.lp, D_MODEL), w_in[layer].astype(BF16),
                                       conv_w[layer], cos, sin, cfg, cast=side)
        if feeds_moe:
            moe_bf16["gate" if dense else "up"] = cast[0]
        attn = _attention(q, k, v, vec(lambda_q1[layer]), vec(lambda_k1[layer]),
                          vec(lambda_q2[layer]), vec(lambda_k2[layer]), vec(subln_g[layer]),
                          lambda_init, cfg)
        h = _outproj(conv.reshape(tokens, D_CONV), attn.reshape(tokens, D_MODEL - D_CONV),
                     w_out[layer].astype(BF16), h, vec(ln_mix_g[layer]), vec(ln_mix_b[layer]), cfg, alpha)
        g, b = vec(ln_ffn_g[layer]), vec(ln_ffn_b[layer])
        if dense:
            h, *cast = _ffn(w_gate_dense[idx].astype(BF16), w_up_dense[idx].astype(BF16),
                            w_down_dense[idx].astype(BF16), h, g, b, cfg, alpha,
                            cast=w_down_moe[idx] if feeds_moe else None)
            if feeds_moe:
                moe_bf16["down"] = cast[0]
        else:
            assert layer == cfg.depth - 1
            return _routed_moe(h.reshape(cfg.batch, cfg.lp, D_MODEL), w_router[idx],
                               moe_bf16["gate"], moe_bf16["up"], moe_bf16["down"], g, b, cfg, alpha)
    return h.reshape(cfg.batch, cfg.lp, D_MODEL)[:, N_META:N_META + cfg.seq]


_CFG = Cfg(batch=8, seq=4096, depth=2, d_ff_dense=2816, d_ff_expert=3584, n_experts=8,
           t_embed=128, n_embed=11, t_proj=1056, t_q=384, t_k=384, t_tok=1024, t_ff_dense=256, t_ff_expert=3584,
           t_route=1024, t_group=512)


def kernel(x, meta_tokens, ln_emb_g, ln_emb_b, w_in, conv_w, lambda_q1, lambda_k1, lambda_q2, lambda_k2, subln_g, w_out, ln_mix_g, ln_mix_b, ln_ffn_g, ln_ffn_b, w_gate_dense, w_up_dense, w_down_dense, w_router, w_gate_moe, w_up_moe, w_down_moe):
    return _forward(_CFG, x, meta_tokens, ln_emb_g, ln_emb_b, w_in, conv_w, lambda_q1, lambda_k1,
                    lambda_q2, lambda_k2, subln_g, w_out, ln_mix_g, ln_mix_b, ln_ffn_g, ln_ffn_b,
                    w_gate_dense, w_up_dense, w_down_dense, w_router, w_gate_moe, w_up_moe, w_down_moe)
```
